```python
import jax, jax.numpy as jnp
from jax import lax
import numpy as np

D_MODEL = 1024
BATCH = 8
SEQ = 4096
DEPTH = 2

D_MIX = D_MODEL
S5_WIDTH = D_MIX // 4
S5_GROUP = 16
S5_NGROUPS = S5_WIDTH // S5_GROUP
S5_STATE = 64
SSD_WIDTH = (3 * D_MIX) // 8
SSD_HEADDIM = 64
SSD_HEADS = SSD_WIDTH // SSD_HEADDIM
SSD_NGROUPS = 2
SSD_STATE = 128
SSD_CHUNK = 128
GDN_WIDTH = D_MIX - S5_WIDTH - SSD_WIDTH
GDN_HEADDIM = 64
GDN_HEADS = GDN_WIDTH // GDN_HEADDIM
GDN_CHUNK = 64
CONV_K = 4
D_FF = 3584
N_EXPERTS = 8
TOP_K = 2
N_DENSE = (DEPTH + 1) // 2
N_MOE = DEPTH // 2
EPS = 1e-6

SSD_CONV_DIM = SSD_WIDTH + 2 * SSD_NGROUPS * SSD_STATE
GDN_CONV_DIM = 3 * GDN_WIDTH
IN_SIZES = (S5_WIDTH, SSD_WIDTH, SSD_CONV_DIM, SSD_HEADS, GDN_CONV_DIM, GDN_WIDTH, GDN_HEADS, GDN_HEADS)
D_IN_PROJ = sum(IN_SIZES)
IN_SPLITS = tuple(sum(IN_SIZES[:i + 1]) for i in range(len(IN_SIZES) - 1))

kernel_name = "hymba_s5_ssd_gdn_moe_trunk"


def rms_norm(x, w):
    xf = x.astype(jnp.float32)
    y = xf * lax.rsqrt(jnp.mean(xf * xf, axis=-1, keepdims=True) + EPS)
    return (y * w.astype(jnp.float32)).astype(x.dtype)


def grouped_rms_norm(y, w, groups):
    shp = y.shape
    yg = y.reshape(shp[:-1] + (groups, shp[-1] // groups))
    yg = yg * lax.rsqrt(jnp.mean(yg * yg, axis=-1, keepdims=True) + EPS)
    return yg.reshape(shp) * w


def causal_conv(x, w):
    k, c = w.shape
    return lax.conv_general_dilated(x, w[:, None, :], window_strides=(1,), padding=[(k - 1, 0)],
                                    dimension_numbers=('NWC', 'WIO', 'NWC'), feature_group_count=c)


def swiglu(h, w_gate, w_up, w_down):
    return (jax.nn.silu(h @ w_gate) * (h @ w_up)) @ w_down


def s5_mixer(u, a_re, a_im, b_re, b_im, c_re, c_im, d_skip, log_step, w_glu, norm_w):
    a_re, a_im, b_re, b_im, c_re, c_im, d_skip, log_step, w_glu, norm_w = (
        t.astype(jnp.float32) for t in (a_re, a_im, b_re, b_im, c_re, c_im, d_skip, log_step, w_glu, norm_w))
    bsz, seq, _ = u.shape
    ug = u.reshape(bsz, seq, S5_NGROUPS, S5_GROUP)
    step = jnp.exp(log_step)[:, None]
    mag = jnp.exp(a_re * step)
    ang = a_im * step
    lb_re, lb_im = mag * jnp.cos(ang), mag * jnp.sin(ang)
    den = a_re * a_re + a_im * a_im
    f_re = ((lb_re - 1.0) * a_re + lb_im * a_im) / den
    f_im = (lb_im * a_re - (lb_re - 1.0) * a_im) / den
    bb_re = f_re[..., None] * b_re - f_im[..., None] * b_im
    bb_im = f_re[..., None] * b_im + f_im[..., None] * b_re
    bu_re = jnp.einsum('gph,blgh->blgp', bb_re, ug)
    bu_im = jnp.einsum('gph,blgh->blgp', bb_im, ug)
    ar = jnp.broadcast_to(lb_re, bu_re.shape)
    ai = jnp.broadcast_to(lb_im, bu_im.shape)

    def combine(e1, e2):
        a1r, a1i, b1r, b1i = e1
        a2r, a2i, b2r, b2i = e2
        return (a2r * a1r - a2i * a1i, a2r * a1i + a2i * a1r,
                a2r * b1r - a2i * b1i + b2r, a2r * b1i + a2i * b1r + b2i)

    _, _, s_re, s_im = lax.associative_scan(combine, (ar, ai, bu_re, bu_im), axis=1)
    y = (jnp.einsum('ghp,blgp->blgh', c_re, s_re) - jnp.einsum('ghp,blgp->blgh', c_im, s_im)
         + d_skip * ug).reshape(bsz, seq, S5_WIDTH)
    y = jax.nn.gelu(y)
    y = y * jax.nn.sigmoid(y @ w_glu)
    return rms_norm(y, norm_w)


def ssd_chunked(x, a, b, c):
    bsz, seq, nh, hp = x.shape
    ns = b.shape[-1]
    nc, q = seq // SSD_CHUNK, SSD_CHUNK
    x = x.reshape(bsz, nc, q, nh, hp)
    b = b.reshape(bsz, nc, q, nh, ns)
    c = c.reshape(bsz, nc, q, nh, ns)
    a_cs = jnp.cumsum(a.reshape(bsz, nc, q, nh).transpose(0, 1, 3, 2), axis=-1)
    causal = jnp.tril(jnp.ones((q, q), dtype=bool))
    decay = jnp.exp(jnp.where(causal, a_cs[..., :, None] - a_cs[..., None, :], -jnp.inf))
    scores = jnp.einsum('bclhn,bcshn->bchls', c, b) * decay
    y_diag = jnp.einsum('bchls,bcshp->bclhp', scores, x)
    decay_states = jnp.exp(a_cs[..., -1:] - a_cs).transpose(0, 1, 3, 2)
    states = jnp.einsum('bclhn,bclhp->bchpn', b * decay_states[..., None], x)
    chunk_decay = jnp.exp(a_cs[..., -1])

    def step(prev, inp):
        st, dec = inp
        return prev * dec[..., None, None] + st, prev

    init = jnp.zeros((bsz, nh, hp, ns), x.dtype)
    _, prev_states = lax.scan(step, init, (jnp.moveaxis(states, 1, 0), jnp.moveaxis(chunk_decay, 1, 0)))
    prev_states = jnp.moveaxis(prev_states, 0, 1)
    y_off = jnp.einsum('bclhn,bchpn->bclhp', c, prev_states) * jnp.exp(a_cs).transpose(0, 1, 3, 2)[..., None]
    return (y_diag + y_off).reshape(bsz, seq, nh, hp)


def ssd_mixer(z, xbc, dt, conv_w, conv_b, dt_bias, a_log, d_skip, norm_w):
    conv_w, conv_b, dt_bias, a_log, d_skip, norm_w = (
        t.astype(jnp.float32) for t in (conv_w, conv_b, dt_bias, a_log, d_skip, norm_w))
    bsz, seq, _ = xbc.shape
    xbc = jax.nn.silu(causal_conv(xbc, conv_w) + conv_b)
    xs, bm, cm = jnp.split(xbc, (SSD_WIDTH, SSD_WIDTH + SSD_NGROUPS * SSD_STATE), axis=-1)
    xs = xs.reshape(bsz, seq, SSD_HEADS, SSD_HEADDIM)
    rep = SSD_HEADS // SSD_NGROUPS
    bm = jnp.repeat(bm.reshape(bsz, seq, SSD_NGROUPS, SSD_STATE), rep, axis=2)
    cm = jnp.repeat(cm.reshape(bsz, seq, SSD_NGROUPS, SSD_STATE), rep, axis=2)
    dt = jax.nn.softplus(dt + dt_bias)
    a = -jnp.exp(a_log)
    y = ssd_chunked(xs * dt[..., None], dt * a, bm, cm)
    y = (y + d_skip[:, None] * xs).reshape(bsz, seq, SSD_WIDTH)
    return grouped_rms_norm(y * jax.nn.silu(z), norm_w, SSD_NGROUPS)


def gated_delta_chunked(q, k, v, g, beta):
    bsz, seq, nh, hd = q.shape
    cs, nc = GDN_CHUNK, seq // GDN_CHUNK

    def chunks(t):
        return t.reshape(bsz, nc, cs, nh, -1).transpose(0, 3, 1, 2, 4)

    q, k, v = chunks(q), chunks(k), chunks(v)
    beta = beta.reshape(bsz, nc, cs, nh).transpose(0, 3, 1, 2)
    g = jnp.cumsum(g.reshape(bsz, nc, cs, nh).transpose(0, 3, 1, 2), axis=-1)
    k_beta = k * beta[..., None]
    v_beta = v * beta[..., None]
    causal = jnp.tril(jnp.ones((cs, cs), dtype=bool))
    strict = jnp.tril(jnp.ones((cs, cs), dtype=bool), k=-1)
    decay = jnp.exp(jnp.where(causal, g[..., :, None] - g[..., None, :], -jnp.inf))
    lmat = jnp.where(strict, jnp.einsum('bhncd,bhnsd->bhncs', k_beta, k) * decay, 0.0)
    eye = jnp.eye(cs, dtype=q.dtype)
    t_inv = lax.linalg.triangular_solve(eye + lmat, jnp.broadcast_to(eye, lmat.shape),
                                        left_side=True, lower=True)
    value = jnp.einsum('bhncs,bhnsd->bhncd', t_inv, v_beta)
    k_cumdecay = jnp.einsum('bhncs,bhnsd->bhncd', t_inv, k_beta * jnp.exp(g)[..., None])
    qk_intra = jnp.where(causal, jnp.einsum('bhncd,bhnsd->bhncs', q, k) * decay, 0.0)
    q_decay = q * jnp.exp(g)[..., None]
    k_to_end = k * jnp.exp(g[..., -1:] - g)[..., None]
    chunk_decay = jnp.exp(g[..., -1])

    def step(state, inp):
        qk_i, val_i, kcd_i, qd_i, kend_i, dec_i = inp
        v_new = val_i - jnp.einsum('bhcd,bhde->bhce', kcd_i, state)
        o_i = jnp.einsum('bhcd,bhde->bhce', qd_i, state) + jnp.einsum('bhcs,bhse->bhce', qk_i, v_new)
        state = state * dec_i[..., None, None] + jnp.einsum('bhcd,bhce->bhde', kend_i, v_new)
        return state, o_i

    xs = tuple(jnp.moveaxis(t, 2, 0) for t in (qk_intra, value, k_cumdecay, q_decay, k_to_end, chunk_decay))
    init = jnp.zeros((bsz, nh, hd, hd), q.dtype)
    _, o = lax.scan(step, init, xs)
    return o.transpose(1, 0, 3, 2, 4).reshape(bsz, seq, nh, hd)


def gdn_mixer(qkv, z, beta_in, a_in, conv_w, a_log, dt_bias, norm_w):
    conv_w, a_log, dt_bias, norm_w = (t.astype(jnp.float32) for t in (conv_w, a_log, dt_bias, norm_w))
    bsz, seq, _ = qkv.shape
    qkv = jax.nn.silu(causal_conv(qkv, conv_w))
    q, k, v = (t.reshape(bsz, seq, GDN_HEADS, GDN_HEADDIM) for t in jnp.split(qkv, 3, axis=-1))
    q = q * lax.rsqrt(jnp.sum(q * q, axis=-1, keepdims=True) + EPS) * (GDN_HEADDIM ** -0.5)
    k = k * lax.rsqrt(jnp.sum(k * k, axis=-1, keepdims=True) + EPS)
    beta = jax.nn.sigmoid(beta_in)
    g = -jnp.exp(a_log) * jax.nn.softplus(a_in + dt_bias)
    o = gated_delta_chunked(q, k, v, g, beta)
    o = rms_norm(o, norm_w) * jax.nn.silu(z.reshape(bsz, seq, GDN_HEADS, GDN_HEADDIM))
    return o.reshape(bsz, seq, GDN_WIDTH)


def moe_ffn(h, w_router, w_gate, w_up, w_down):
    logits = (h @ w_router).astype(jnp.float32)
    top_v, top_i = lax.top_k(logits, TOP_K)
    top_w = jax.nn.softmax(top_v, axis=-1)
    combine = jnp.sum(jax.nn.one_hot(top_i, N_EXPERTS, dtype=jnp.float32) * top_w[..., None], axis=-2)
    out = jnp.zeros_like(h)
    for e in range(N_EXPERTS):
        out = out + combine[..., e:e + 1].astype(h.dtype) * swiglu(h, w_gate[e], w_up[e], w_down[e])
    return out


def setup_inputs(seed: int = 0) -> dict:
    key = jax.random.key(seed)
    ks = iter(jax.random.split(key, 48))

    def nrm(shape, scale):
        return jax.random.normal(next(ks), shape, jnp.float32) * scale

    def unif(shape, lo, hi):
        return jax.random.uniform(next(ks), shape, jnp.float32, lo, hi)

    def gain(shape):
        return 1.0 + nrm(shape, 0.01)

    def dt_bias_init(shape):
        dt = jnp.exp(unif(shape, float(np.log(1e-3)), float(np.log(1e-1))))
        return dt + jnp.log(-jnp.expm1(-dt))

    L_, G, P, H = DEPTH, S5_NGROUPS, S5_STATE, S5_GROUP
    n_idx = jnp.arange(P, dtype=jnp.float32)
    inp = {}
    inp['x'] = nrm((BATCH, SEQ, D_MODEL), 1.0)
    inp['norm_mix'] = gain((L_, D_MODEL))
    inp['w_in'] = nrm((L_, D_MODEL, D_IN_PROJ), D_MODEL ** -0.5)
    inp['w_out'] = nrm((L_, D_MIX, D_MODEL), D_MIX ** -0.5)
    inp['s5_a_re'] = -0.5 * jnp.exp(nrm((L_, G, P), 0.05))
    inp['s5_a_im'] = jnp.pi * n_idx + nrm((L_, G, P), 0.01)
    inp['s5_b_re'] = nrm((L_, G, P, H), (2.0 * H) ** -0.5)
    inp['s5_b_im'] = nrm((L_, G, P, H), (2.0 * H) ** -0.5)
    inp['s5_c_re'] = nrm((L_, G, H, P), (2.0 * P) ** -0.5)
    inp['s5_c_im'] = nrm((L_, G, H, P), (2.0 * P) ** -0.5)
    inp['s5_d'] = nrm((L_, G, H), 1.0)
    inp['s5_log_step'] = unif((L_, G), float(np.log(1e-3)), float(np.log(1e-1)))
    inp['s5_w_glu'] = nrm((L_, S5_WIDTH, S5_WIDTH), S5_WIDTH ** -0.5)
    inp['s5_norm'] = gain((L_, S5_WIDTH))
    inp['ssd_conv_w'] = nrm((L_, CONV_K, SSD_CONV_DIM), CONV_K ** -0.5)
    inp['ssd_conv_b'] = nrm((L_, SSD_CONV_DIM), 0.02)
    inp['ssd_dt_bias'] = dt_bias_init((L_, SSD_HEADS))
    inp['ssd_a_log'] = jnp.log(unif((L_, SSD_HEADS), 1.0, 16.0))
    inp['ssd_d'] = 1.0 + nrm((L_, SSD_HEADS), 0.1)
    inp['ssd_norm'] = gain((L_, SSD_WIDTH))
    inp['gdn_conv_w'] = nrm((L_, CONV_K, GDN_CONV_DIM), CONV_K ** -0.5)
    inp['gdn_a_log'] = jnp.log(unif((L_, GDN_HEADS), 1.0, 16.0))
    inp['gdn_dt_bias'] = dt_bias_init((L_, GDN_HEADS))
    inp['gdn_norm'] = gain((L_, GDN_HEADDIM))
    inp['norm_ffn'] = gain((L_, D_MODEL))
    inp['ff_w_gate'] = nrm((N_DENSE, D_MODEL, D_FF), D_MODEL ** -0.5)
    inp['ff_w_up'] = nrm((N_DENSE, D_MODEL, D_FF), D_MODEL ** -0.5)
    inp['ff_w_down'] = nrm((N_DENSE, D_FF, D_MODEL), D_FF ** -0.5)
    inp['moe_router'] = nrm((N_MOE, D_MODEL, N_EXPERTS), D_MODEL ** -0.5)
    inp['moe_w_gate'] = nrm((N_MOE, N_EXPERTS, D_MODEL, D_FF), D_MODEL ** -0.5)
    inp['moe_w_up'] = nrm((N_MOE, N_EXPERTS, D_MODEL, D_FF), D_MODEL ** -0.5)
    inp['moe_w_down'] = nrm((N_MOE, N_EXPERTS, D_FF, D_MODEL), D_FF ** -0.5)
    inp['norm_final'] = gain((D_MODEL,))
    return inp


def reference(x, norm_mix, w_in, w_out, s5_a_re, s5_a_im, s5_b_re, s5_b_im, s5_c_re, s5_c_im,
              s5_d, s5_log_step, s5_w_glu, s5_norm, ssd_conv_w, ssd_conv_b, ssd_dt_bias, ssd_a_log,
              ssd_d, ssd_norm, gdn_conv_w, gdn_a_log, gdn_dt_bias, gdn_norm, norm_ffn,
              ff_w_gate, ff_w_up, ff_w_down, moe_router, moe_w_gate, moe_w_up, moe_w_down, norm_final):
    for layer in range(DEPTH):
        hn = rms_norm(x, norm_mix[layer])
        proj = (hn @ w_in[layer]).astype(jnp.float32)
        u_s5, z_ssd, xbc_ssd, dt_ssd, qkv_gdn, z_gdn, b_gdn, a_gdn = jnp.split(proj, IN_SPLITS, axis=-1)
        y_s5 = s5_mixer(u_s5, s5_a_re[layer], s5_a_im[layer], s5_b_re[layer], s5_b_im[layer],
                        s5_c_re[layer], s5_c_im[layer], s5_d[layer], s5_log_step[layer],
                        s5_w_glu[layer], s5_norm[layer])
        y_ssd = ssd_mixer(z_ssd, xbc_ssd, dt_ssd, ssd_conv_w[layer], ssd_conv_b[layer],
                          ssd_dt_bias[layer], ssd_a_log[layer], ssd_d[layer], ssd_norm[layer])
        y_gdn = gdn_mixer(qkv_gdn, z_gdn, b_gdn, a_gdn, gdn_conv_w[layer], gdn_a_log[layer],
                          gdn_dt_bias[layer], gdn_norm[layer])
        mix = jnp.concatenate([y_s5, y_ssd, y_gdn], axis=-1).astype(x.dtype)
        x = x + mix @ w_out[layer]
        hn = rms_norm(x, norm_ffn[layer])
        i = layer // 2
        if layer % 2 == 0:
            x = x + swiglu(hn, ff_w_gate[i], ff_w_up[i], ff_w_down[i])
        else:
            x = x + moe_ffn(hn, moe_router[i], moe_w_gate[i], moe_w_up[i], moe_w_down[i])
    return rms_norm(x, norm_final)
```

```python
import functools
import math

import jax
import jax.numpy as jnp
import numpy as np
from jax import lax
from jax.experimental import pallas as pl
from jax.experimental.pallas import tpu as pltpu

F32 = jnp.float32
BF16 = jnp.bfloat16
EPS = 1e-6

S5_GROUP = 16
S5_STATE = 64
SSD_HEADDIM = 64
SSD_NGROUPS = 2
SSD_STATE = 128
GDN_HEADDIM = 64
CONV_K = 4
TOP_K = 2

LANES = 128
SUBLANES = 8
VMEM_LIMIT_BYTES = 56 * 1024 * 1024

S5_CHUNK = 64
SSD_CHUNK = 128
GDN_CHUNK = 64
GDN_INV_BLOCK = 16
SEQ_BLOCK = 512
ROW_BLOCK = 512
FFN_ROW_BLOCK = 1024
FFN_COL_BLOCK = 512


def _params(*semantics):
    return pltpu.CompilerParams(dimension_semantics=semantics, vmem_limit_bytes=VMEM_LIMIT_BYTES)


def _dot(a, b):
    return jnp.dot(a.astype(BF16), b.astype(BF16), preferred_element_type=F32)


def _dot_nt(a, b):
    return lax.dot_general(a.astype(BF16), b.astype(BF16), (((1,), (1,)), ((), ())),
                           preferred_element_type=F32)


def _dot_tn(a, b):
    return lax.dot_general(a.astype(BF16), b.astype(BF16), (((0,), (0,)), ((), ())),
                           preferred_element_type=F32)


def _split3(a):
    hi = a.astype(BF16)
    r1 = a - hi.astype(F32)
    mid = r1.astype(BF16)
    lo = (r1 - mid.astype(F32)).astype(BF16)
    return hi, mid, lo


def _dot_sel(a, sel):
    hi, mid, lo = _split3(a)
    out = jnp.dot(hi, sel, preferred_element_type=F32)
    out += jnp.dot(mid, sel, preferred_element_type=F32)
    out += jnp.dot(lo, sel, preferred_element_type=F32)
    return out


def _sel_dot(sel, a):
    hi, mid, lo = _split3(a)
    out = jnp.dot(sel, hi, preferred_element_type=F32)
    out += jnp.dot(sel, mid, preferred_element_type=F32)
    out += jnp.dot(sel, lo, preferred_element_type=F32)
    return out


def _sigmoid(x):
    return 1.0 / (1.0 + jnp.exp(-x))


def _silu(x):
    return x * _sigmoid(x)


def _softplus(x):
    return jnp.maximum(x, 0.0) + jnp.log(1.0 + jnp.exp(-jnp.abs(x)))


def _gelu_tanh(x):
    c = math.sqrt(2.0 / math.pi)
    return 0.5 * x * (1.0 + jnp.tanh(c * (x + 0.044715 * (x * x * x))))


def _rms(x, w):
    return x * lax.rsqrt(jnp.mean(x * x, axis=-1, keepdims=True) + EPS) * w


def _iota2(shape, axis):
    return lax.broadcasted_iota(jnp.int32, shape, axis)


def _inproj_kernel(sizes, x_ref, g_ref, w_ref, *out_refs):
    h = _rms(x_ref[...], g_ref[...])
    p = jnp.dot(h.astype(BF16), w_ref[...], preferred_element_type=F32)
    off = 0
    for o_ref, sz in zip(out_refs, sizes):
        o_ref[...] = p[:, off:off + sz]
        off += sz


def _inproj(x2, gain, w_bf16, sizes):
    n, d = x2.shape
    tm = min(ROW_BLOCK, n)
    total = sum(sizes)
    return pl.pallas_call(
        functools.partial(_inproj_kernel, sizes),
        grid=(n // tm,),
        in_specs=[pl.BlockSpec((tm, d), lambda i: (i, 0)),
                  pl.BlockSpec((1, d), lambda i: (0, 0)),
                  pl.BlockSpec((d, total), lambda i: (0, 0))],
        out_specs=[pl.BlockSpec((tm, sz), lambda i: (i, 0)) for sz in sizes],
        out_shape=[jax.ShapeDtypeStruct((n, sz), F32) for sz in sizes],
        compiler_params=_params("parallel"),
        name="inproj",
    )(x2, gain.reshape(1, d), w_bf16)


def _s5_tables(a_re, a_im, b_re, b_im, c_re, c_im, d_skip, log_step, q, n_chunks):
    hp = lax.Precision.HIGHEST
    g, p = a_re.shape
    h = b_re.shape[-1]
    step = jnp.exp(log_step)[:, None]
    mag = jnp.exp(a_re * step)
    ang = a_im * step
    lb_re, lb_im = mag * jnp.cos(ang), mag * jnp.sin(ang)
    den = a_re * a_re + a_im * a_im
    f_re = ((lb_re - 1.0) * a_re + lb_im * a_im) / den
    f_im = (lb_im * a_re - (lb_re - 1.0) * a_im) / den
    bb_re = f_re[..., None] * b_re - f_im[..., None] * b_im
    bb_im = f_re[..., None] * b_im + f_im[..., None] * b_re

    def power(e):
        e = e.astype(F32)[:, None, None]
        m = jnp.exp(a_re * step * e)
        return m * jnp.cos(ang * e), m * jnp.sin(ang * e)

    pr, pi = power(jnp.arange(q + 1))
    wr = pr[..., None] * bb_re - pi[..., None] * bb_im
    wi = pr[..., None] * bb_im + pi[..., None] * bb_re
    kern = (jnp.einsum('ghp,tgpk->tghk', c_re, wr[:q], precision=hp)
            - jnp.einsum('ghp,tgpk->tghk', c_im, wi[:q], precision=hp))
    kern = kern.at[0].add(d_skip[:, :, None] * jnp.eye(h, dtype=F32))
    t_idx = jnp.arange(q)[:, None] - jnp.arange(q)[None, :]
    toep = jnp.where((t_idx >= 0)[:, :, None, None, None], kern[jnp.clip(t_idx, 0, q - 1)], 0.0)
    tt = toep.transpose(2, 1, 4, 0, 3).reshape(g, q * h, q * h)
    rev = q - 1 - jnp.arange(q)
    mt = jnp.concatenate([wr[rev], wi[rev]], axis=2)
    mt = mt.transpose(1, 0, 3, 2).reshape(g, q * h, 2 * p)
    p1r, p1i = pr[1:], pi[1:]
    n_re = c_re[None] * p1r[:, :, None, :] - c_im[None] * p1i[:, :, None, :]
    n_im = -(c_re[None] * p1i[:, :, None, :] + c_im[None] * p1r[:, :, None, :])
    nt = jnp.concatenate([n_re, n_im], axis=3).transpose(1, 3, 0, 2).reshape(g, 2 * p, q * h)
    n_steps = max(1, int(math.ceil(math.log2(max(n_chunks, 2)))))
    sr, si = power(q * (2 ** jnp.arange(n_steps)))
    ak = jnp.concatenate([sr, sr], axis=2).transpose(1, 0, 2)
    bk = jnp.concatenate([-si, si], axis=2).transpose(1, 0, 2)
    return tt, mt, nt, ak, bk


def _s5_kernel(n_chunks, n_steps, u_ref, tt_ref, mt_ref, nt_ref, ak_ref, bk_ref, y_ref):
    hp = lax.Precision.HIGHEST
    u = u_ref[0]
    y = jnp.dot(u, tt_ref[0], precision=hp, preferred_element_type=F32)
    s = jnp.dot(u, mt_ref[0], precision=hp, preferred_element_type=F32)
    rows, width = s.shape
    chunk = _iota2((rows, width), 0) % n_chunks
    ak = ak_ref[0]
    bk = bk_ref[0]
    for k in range(n_steps):
        sh = 2 ** k
        prev = jnp.where(chunk >= sh, pltpu.roll(s, sh, axis=0), 0.0)
        s = s + ak[k:k + 1] * prev + bk[k:k + 1] * pltpu.roll(prev, width // 2, axis=1)
    s_in = jnp.where(chunk >= 1, pltpu.roll(s, 1, axis=0), 0.0)
    y_ref[0] = y + jnp.dot(s_in, nt_ref[0], precision=hp, preferred_element_type=F32)


def _s5_scan(u, tables, bsz, seq):
    tt, mt, nt, ak, bk = tables
    g, qh, _ = tt.shape
    h = S5_GROUP
    q = qh // h
    n_chunks = seq // q
    rows = bsz * n_chunks
    n_steps = ak.shape[1]
    ug = u.reshape(bsz, n_chunks, q, g, h).transpose(3, 0, 1, 2, 4).reshape(g, rows, qh)
    yg = pl.pallas_call(
        functools.partial(_s5_kernel, n_chunks, n_steps),
        grid=(g,),
        in_specs=[pl.BlockSpec((1, rows, qh), lambda i: (i, 0, 0)),
                  pl.BlockSpec((1, qh, qh), lambda i: (i, 0, 0)),
                  pl.BlockSpec((1, qh, mt.shape[2]), lambda i: (i, 0, 0)),
                  pl.BlockSpec((1, nt.shape[1], qh), lambda i: (i, 0, 0)),
                  pl.BlockSpec((1,) + ak.shape[1:], lambda i: (i, 0, 0)),
                  pl.BlockSpec((1,) + bk.shape[1:], lambda i: (i, 0, 0))],
        out_specs=pl.BlockSpec((1, rows, qh), lambda i: (i, 0, 0)),
        out_shape=jax.ShapeDtypeStruct((g, rows, qh), F32),
        compiler_params=_params("parallel"),
        name="s5_scan",
    )(ug, tt, mt, nt, ak, bk)
    return yg.reshape(g, bsz, n_chunks, q, h).transpose(1, 2, 3, 0, 4).reshape(bsz * seq, g * h)


def _causal_conv_block(x_ref, w_ref, pad_ref, first):
    t = x_ref.shape[0]
    halo = SUBLANES

    @pl.when(first)
    def _():
        pad_ref[0:halo, :] = jnp.zeros((halo, pad_ref.shape[1]), F32)

    @pl.when(jnp.logical_not(first))
    def _():
        pad_ref[0:halo, :] = pad_ref[t:t + halo, :]

    pad_ref[halo:halo + t, :] = x_ref[...]
    acc = w_ref[CONV_K - 1:CONV_K, :] * pad_ref[halo:halo + t, :]
    for j in range(1, CONV_K):
        acc += w_ref[CONV_K - 1 - j:CONV_K - j, :] * pad_ref[halo - j:halo - j + t, :]
    return acc


def _ssd_kernel(heads, z_ref, xbc_ref, sm_ref, cw_ref, cb_ref, dtb_ref, a_ref, dexp_ref, nw_ref, ex_ref,
                o_ref, pad_ref, xc_ref, dt_ref, st_ref):
    t = xbc_ref.shape[0]
    q = min(SSD_CHUNK, t)
    width = heads * SSD_HEADDIM
    gw = SSD_NGROUPS * SSD_STATE
    first = pl.program_id(1) == 0

    @pl.when(first)
    def _():
        st_ref[...] = jnp.zeros(st_ref.shape, F32)

    xc_ref[...] = _silu(_causal_conv_block(xbc_ref, cw_ref, pad_ref, first) + cb_ref[...])
    dt_ref[...] = _softplus(sm_ref[...] + dtb_ref[...])

    row = _iota2((q, q), 0)
    col = _iota2((q, q), 1)
    causal = row >= col
    tril = jnp.where(causal, 1.0, 0.0).astype(BF16)
    lane = _iota2((1, width), 1)
    rep = heads // SSD_NGROUPS

    def chunk_body(ci, carry):
        r0 = pl.multiple_of(ci * q, q)
        xs = xc_ref[pl.ds(r0, q), 0:width]
        bm = xc_ref[pl.ds(r0, q), width:width + gw]
        cm = xc_ref[pl.ds(r0, q), width + gw:width + 2 * gw]
        dt = dt_ref[pl.ds(r0, q), :]
        acs = _sel_dot(tril, dt * a_ref[...])
        acs_t = acs.T
        xdt = xs * _dot_sel(dt, ex_ref[...])
        pieces = []
        for gi in range(SSD_NGROUPS):
            bg = bm[:, gi * SSD_STATE:(gi + 1) * SSD_STATE]
            cg = cm[:, gi * SSD_STATE:(gi + 1) * SSD_STATE]
            cb = _dot_nt(cg, bg)
            for hh in range(gi * rep, (gi + 1) * rep):
                acol = acs[:, hh:hh + 1]
                arow = acs_t[hh:hh + 1, :]
                alast = acs[q - 1:q, hh:hh + 1]
                diff = jnp.where(causal, acol - arow, 0.0)
                decay = jnp.where(causal, jnp.exp(diff), 0.0)
                xh = xdt[:, hh * SSD_HEADDIM:(hh + 1) * SSD_HEADDIM]
                y_diag = _dot(cb * decay, xh)
                state = st_ref[hh]
                y_off = _dot(cg, state) * jnp.exp(acol)
                st_ref[hh] = state * jnp.exp(alast) + _dot_tn(bg * jnp.exp(alast - acol), xh)
                pieces.append(y_diag + y_off)
        y = jnp.concatenate(pieces, axis=1) + dexp_ref[...] * xs
        y = y * _silu(z_ref[pl.ds(r0, q), :])
        gsz = width // SSD_NGROUPS
        y2 = y * y
        out = jnp.zeros_like(y)
        for gi in range(SSD_NGROUPS):
            m = (lane >= gi * gsz) & (lane < (gi + 1) * gsz)
            ms = jnp.sum(jnp.where(m, y2, 0.0), axis=-1, keepdims=True) / gsz
            out = jnp.where(m, y * lax.rsqrt(ms + EPS), out)
        o_ref[pl.ds(r0, q), :] = out * nw_ref[...]
        return carry

    lax.fori_loop(0, t // q, chunk_body, 0)


def _ssd(z, xbc, small, conv_w, conv_b, dt_bias, a_log, d_skip, norm_w, bsz, seq):
    n, width = z.shape
    heads = a_log.shape[0]
    cdim = xbc.shape[1]
    t = min(SEQ_BLOCK, seq)
    nblk = seq // t
    pad = lambda v: jnp.zeros((1, LANES), F32).at[0, :heads].set(v)
    expand = (jnp.arange(LANES)[:, None] == (jnp.arange(width)[None, :] // SSD_HEADDIM)).astype(BF16)
    tok = lambda b, c: (b * nblk + c, 0)
    const = lambda b, c: (0, 0)
    return pl.pallas_call(
        functools.partial(_ssd_kernel, heads),
        grid=(bsz, nblk),
        in_specs=[pl.BlockSpec((t, width), tok),
                  pl.BlockSpec((t, cdim), tok),
                  pl.BlockSpec((t, LANES), tok),
                  pl.BlockSpec((CONV_K, cdim), const),
                  pl.BlockSpec((1, cdim), const),
                  pl.BlockSpec((1, LANES), const),
                  pl.BlockSpec((1, LANES), const),
                  pl.BlockSpec((1, width), const),
                  pl.BlockSpec((1, width), const),
                  pl.BlockSpec((LANES, width), const)],
        out_specs=pl.BlockSpec((t, width), tok),
        out_shape=jax.ShapeDtypeStruct((n, width), F32),
        scratch_shapes=[pltpu.VMEM((SUBLANES + t, cdim), F32),
                        pltpu.VMEM((t, cdim), F32),
                        pltpu.VMEM((t, LANES), F32),
                        pltpu.VMEM((heads, SSD_STATE, SSD_HEADDIM), F32)],
        compiler_params=_params("parallel", "arbitrary"),
        name="ssd",
    )(z, xbc, small, conv_w, conv_b.reshape(1, cdim), pad(dt_bias), pad(-jnp.exp(a_log)),
      jnp.repeat(d_skip, SSD_HEADDIM).reshape(1, width), norm_w.reshape(1, width), expand)


def _unit_lower_inverse(lm, eye, blk):
    c = lm.shape[0]
    ld = jnp.where(blk, lm, 0.0)
    off = lm - ld
    dinv = eye - ld
    pw = ld
    span = 2
    while span < GDN_INV_BLOCK:
        pw = _dot(pw, pw)
        dinv = _dot(dinv, eye + pw)
        span *= 2
    m = _dot(dinv, off)
    out = eye - m
    pw = m
    span = 2
    while span < c // GDN_INV_BLOCK:
        pw = _dot(pw, pw)
        out = _dot(out, eye + pw)
        span *= 2
    return _dot(out, dinv)


def _gdn_kernel(heads, qkv_ref, z_ref, sm_ref, cw_ref, dtb_ref, a_ref, nw_ref, exb_ref, exg_ref, ones_ref,
                o_ref, pad_ref, xc_ref, bg_ref, st_ref):
    t = qkv_ref.shape[0]
    c = min(GDN_CHUNK, t)
    hd = GDN_HEADDIM
    width = heads * hd
    first = pl.program_id(1) == 0

    @pl.when(first)
    def _():
        st_ref[...] = jnp.zeros(st_ref.shape, F32)

    xc_ref[...] = _silu(_causal_conv_block(qkv_ref, cw_ref, pad_ref, first))
    sm = sm_ref[...]
    bg_ref[0] = _sigmoid(sm)
    bg_ref[1] = a_ref[...] * _softplus(sm + dtb_ref[...])

    row = _iota2((c, c), 0)
    col = _iota2((c, c), 1)
    causal = row >= col
    strict = row > col
    eye = jnp.where(row == col, 1.0, 0.0)
    blk = (row // GDN_INV_BLOCK) == (col // GDN_INV_BLOCK)
    tril = jnp.where(causal, 1.0, 0.0).astype(BF16)

    def seg_sum(v):
        hi = v.astype(BF16)
        lo = (v - hi.astype(F32)).astype(BF16)
        return (jnp.dot(hi, ones_ref[...], preferred_element_type=F32)
                + jnp.dot(lo, ones_ref[...], preferred_element_type=F32))

    def chunk_body(ci, carry):
        r0 = pl.multiple_of(ci * c, c)
        qf = xc_ref[pl.ds(r0, c), 0:width]
        kf = xc_ref[pl.ds(r0, c), width:2 * width]
        vf = xc_ref[pl.ds(r0, c), 2 * width:3 * width]
        qf = qf * lax.rsqrt(seg_sum(qf * qf) + EPS) * (hd ** -0.5)
        kf = kf * lax.rsqrt(seg_sum(kf * kf) + EPS)
        beta = _dot_sel(bg_ref[0, pl.ds(r0, c), :], exb_ref[...])
        gcs = _sel_dot(tril, bg_ref[1, pl.ds(r0, c), :])
        gcs_t = gcs.T
        gw = _dot_sel(gcs, exg_ref[...])
        gexp = jnp.exp(gw)
        gend = jnp.exp(gw[c - 1:c, :] - gw)
        kb = kf * beta
        vb = vf * beta
        kcd_in = kb * gexp
        qd = qf * gexp
        kend = kf * gend
        pieces = []
        for hh in range(heads):
            sl = slice(hh * hd, (hh + 1) * hd)
            lg = 2 * heads + hh
            gcol = gcs[:, lg:lg + 1]
            grow = gcs_t[lg:lg + 1, :]
            diff = jnp.where(causal, gcol - grow, 0.0)
            decay = jnp.where(causal, jnp.exp(diff), 0.0)
            kh = kf[:, sl]
            lm = jnp.where(strict, _dot_nt(kb[:, sl], kh) * decay, 0.0)
            tinv = _unit_lower_inverse(lm, eye, blk)
            value = _dot(tinv, vb[:, sl])
            kcd = _dot(tinv, kcd_in[:, sl])
            qk = _dot_nt(qf[:, sl], kh) * decay
            state = st_ref[hh]
            v_new = value - _dot(kcd, state)
            pieces.append(_dot(qd[:, sl], state) + _dot(qk, v_new))
            glast = gcs[c - 1:c, lg:lg + 1]
            st_ref[hh] = state * jnp.exp(glast) + _dot_tn(kend[:, sl], v_new)
        o = jnp.concatenate(pieces, axis=1)
        o = o * lax.rsqrt(seg_sum(o * o) / hd + EPS) * nw_ref[...]
        o_ref[pl.ds(r0, c), :] = o * _silu(z_ref[pl.ds(r0, c), :])
        return carry

    lax.fori_loop(0, t // c, chunk_body, 0)


def _gdn(qkv, z, small, conv_w, a_log, dt_bias, norm_w, bsz, seq):
    n, width = z.shape
    heads = a_log.shape[0]
    cdim = qkv.shape[1]
    t = min(SEQ_BLOCK, seq)
    nblk = seq // t
    pad = lambda v, off: jnp.zeros((1, LANES), F32).at[0, off:off + heads].set(v)
    head_of = jnp.arange(width)[None, :] // GDN_HEADDIM
    lanes = jnp.arange(LANES)[:, None]
    exb = (lanes == heads + head_of).astype(BF16)
    exg = (lanes == 2 * heads + head_of).astype(BF16)
    ones = (head_of.T == head_of).astype(BF16)
    tok = lambda b, c: (b * nblk + c, 0)
    const = lambda b, c: (0, 0)
    return pl.pallas_call(
        functools.partial(_gdn_kernel, heads),
        grid=(bsz, nblk),
        in_specs=[pl.BlockSpec((t, cdim), tok),
                  pl.BlockSpec((t, width), tok),
                  pl.BlockSpec((t, LANES), tok),
                  pl.BlockSpec((CONV_K, cdim), const),
                  pl.BlockSpec((1, LANES), const),
                  pl.BlockSpec((1, LANES), const),
                  pl.BlockSpec((1, width), const),
                  pl.BlockSpec((LANES, width), const),
                  pl.BlockSpec((LANES, width), const),
                  pl.BlockSpec((width, width), const)],
        out_specs=pl.BlockSpec((t, width), tok),
        out_shape=jax.ShapeDtypeStruct((n, width), F32),
        scratch_shapes=[pltpu.VMEM((SUBLANES + t, cdim), F32),
                        pltpu.VMEM((t, cdim), F32),
                        pltpu.VMEM((2, t, LANES), F32),
                        pltpu.VMEM((heads, GDN_HEADDIM, GDN_HEADDIM), F32)],
        compiler_params=_params("parallel", "arbitrary"),
        name="gdn",
    )(qkv, z, small, conv_w, pad(dt_bias, 2 * heads), pad(-jnp.exp(a_log), 2 * heads),
      jnp.tile(norm_w, heads).reshape(1, width), exb, exg, ones)


def _outproj_kernel(x_ref, s5_ref, ssd_ref, gdn_ref, wglu_ref, s5n_ref, wout_ref, o_ref):
    y = _gelu_tanh(s5_ref[...])
    y = y * _sigmoid(jnp.dot(y.astype(BF16), wglu_ref[...], preferred_element_type=F32))
    y = _rms(y, s5n_ref[...])
    mix = jnp.concatenate([y, ssd_ref[...], gdn_ref[...]], axis=1)
    o_ref[...] = x_ref[...] + jnp.dot(mix.astype(BF16), wout_ref[...], preferred_element_type=F32)


def _outproj(x2, y_s5, y_ssd, y_gdn, w_glu, s5_norm, w_out):
    n, d = x2.shape
    tm = min(ROW_BLOCK, n)
    ws = [y_s5.shape[1], y_ssd.shape[1], y_gdn.shape[1]]
    row = lambda i: (i, 0)
    const = lambda i: (0, 0)
    return pl.pallas_call(
        _outproj_kernel,
        grid=(n // tm,),
        in_specs=[pl.BlockSpec((tm, d), row)] + [pl.BlockSpec((tm, w), row) for w in ws]
                 + [pl.BlockSpec((ws[0], ws[0]), const), pl.BlockSpec((1, ws[0]), const),
                    pl.BlockSpec((sum(ws), d), const)],
        out_specs=pl.BlockSpec((tm, d), row),
        out_shape=jax.ShapeDtypeStruct((n, d), F32),
        compiler_params=_params("parallel"),
        name="outproj",
    )(x2, y_s5, y_ssd, y_gdn, w_glu.astype(BF16), s5_norm.reshape(1, -1), w_out.astype(BF16))


def _ffn_kernel(final, x_ref, g_ref, wg_ref, wu_ref, wd_ref, gf_ref, o_ref, h_ref, acc_ref):
    f = pl.program_id(1)

    @pl.when(f == 0)
    def _():
        h_ref[...] = _rms(x_ref[...], g_ref[...]).astype(BF16)
        acc_ref[...] = jnp.zeros(acc_ref.shape, F32)

    h = h_ref[...]
    a = jnp.dot(h, wg_ref[...], preferred_element_type=F32)
    u = jnp.dot(h, wu_ref[...], preferred_element_type=F32)
    acc_ref[...] += jnp.dot((_silu(a) * u).astype(BF16), wd_ref[...], preferred_element_type=F32)

    @pl.when(f == pl.num_programs(1) - 1)
    def _():
        out = x_ref[...] + acc_ref[...]
        o_ref[...] = _rms(out, gf_ref[...]) if final else out


def _ffn(x2, gain, w_gate, w_up, w_down, final_gain):
    n, d = x2.shape
    ff = w_gate.shape[1]
    tm = min(FFN_ROW_BLOCK, n)
    tf = min(FFN_COL_BLOCK, ff)
    final = final_gain is not None
    gf = (final_gain if final else gain).reshape(1, d)
    return pl.pallas_call(
        functools.partial(_ffn_kernel, final),
        grid=(n // tm, ff // tf),
        in_specs=[pl.BlockSpec((tm, d), lambda i, f: (i, 0)),
                  pl.BlockSpec((1, d), lambda i, f: (0, 0)),
                  pl.BlockSpec((d, tf), lambda i, f: (0, f)),
                  pl.BlockSpec((d, tf), lambda i, f: (0, f)),
                  pl.BlockSpec((tf, d), lambda i, f: (f, 0)),
                  pl.BlockSpec((1, d), lambda i, f: (0, 0))],
        out_specs=pl.BlockSpec((tm, d), lambda i, f: (i, 0)),
        out_shape=jax.ShapeDtypeStruct((n, d), F32),
        scratch_shapes=[pltpu.VMEM((tm, d), BF16), pltpu.VMEM((tm, d), F32)],
        compiler_params=_params("parallel", "arbitrary"),
        name="ffn",
    )(x2, gain.reshape(1, d), w_gate.astype(BF16), w_up.astype(BF16), w_down.astype(BF16), gf)


def _moe_kernel(final, n_exp, x_ref, g_ref, wr_ref, wg_ref, wu_ref, wd_ref, gf_ref, o_ref,
                h_ref, acc_ref, cmb_ref):
    e = pl.program_id(1)
    f = pl.program_id(2)

    @pl.when((e == 0) & (f == 0))
    def _():
        hn = _rms(x_ref[...], g_ref[...])
        h_ref[...] = hn.astype(BF16)
        acc_ref[...] = jnp.zeros(acc_ref.shape, F32)
        logits = jnp.dot(hn, wr_ref[...], precision=lax.Precision.HIGHEST, preferred_element_type=F32)
        lane = _iota2(logits.shape, 1)
        neg = jnp.float32(-jnp.inf)
        logits = jnp.where(lane < n_exp, logits, neg)
        m1 = jnp.max(logits, axis=-1, keepdims=True)
        i1 = jnp.min(jnp.where(logits == m1, lane, LANES), axis=-1, keepdims=True)
        rest = jnp.where(lane == i1, neg, logits)
        m2 = jnp.max(rest, axis=-1, keepdims=True)
        i2 = jnp.min(jnp.where(rest == m2, lane, LANES), axis=-1, keepdims=True)
        w2 = 1.0 / (1.0 + jnp.exp(m1 - m2))
        w1 = 1.0 / (1.0 + jnp.exp(m2 - m1))
        cmb_ref[...] = jnp.where(lane == i1, w1, 0.0) + jnp.where(lane == i2, w2, 0.0)

    cmb = cmb_ref[...]
    lane = _iota2(cmb.shape, 1)
    ce = jnp.sum(jnp.where(lane == e, cmb, 0.0), axis=-1, keepdims=True)
    h = h_ref[...]
    a = jnp.dot(h, wg_ref[0], preferred_element_type=F32)
    u = jnp.dot(h, wu_ref[0], preferred_element_type=F32)
    acc_ref[...] += ce * jnp.dot((_silu(a) * u).astype(BF16), wd_ref[0], preferred_element_type=F32)

    @pl.when((e == pl.num_programs(1) - 1) & (f == pl.num_programs(2) - 1))
    def _():
        out = x_ref[...] + acc_ref[...]
        o_ref[...] = _rms(out, gf_ref[...]) if final else out


def _moe(x2, gain, w_router, w_gate, w_up, w_down, final_gain):
    n, d = x2.shape
    n_exp, _, ff = w_gate.shape
    tm = min(FFN_ROW_BLOCK, n)
    tf = min(FFN_COL_BLOCK, ff)
    final = final_gain is not None
    gf = (final_gain if final else gain).reshape(1, d)
    wr = jnp.zeros((d, LANES), F32).at[:, :n_exp].set(w_router)
    return pl.pallas_call(
        functools.partial(_moe_kernel, final, n_exp),
        grid=(n // tm, n_exp, ff // tf),
        in_specs=[pl.BlockSpec((tm, d), lambda i, e, f: (i, 0)),
                  pl.BlockSpec((1, d), lambda i, e, f: (0, 0)),
                  pl.BlockSpec((d, LANES), lambda i, e, f: (0, 0)),
                  pl.BlockSpec((1, d, tf), lambda i, e, f: (e, 0, f)),
                  pl.BlockSpec((1, d, tf), lambda i, e, f: (e, 0, f)),
                  pl.BlockSpec((1, tf, d), lambda i, e, f: (e, f, 0)),
                  pl.BlockSpec((1, d), lambda i, e, f: (0, 0))],
        out_specs=pl.BlockSpec((tm, d), lambda i, e, f: (i, 0)),
        out_shape=jax.ShapeDtypeStruct((n, d), F32),
        scratch_shapes=[pltpu.VMEM((tm, d), BF16), pltpu.VMEM((tm, d), F32), pltpu.VMEM((tm, LANES), F32)],
        compiler_params=_params("parallel", "arbitrary", "arbitrary"),
        name="moe",
    )(x2, gain.reshape(1, d), wr, w_gate.astype(BF16), w_up.astype(BF16), w_down.astype(BF16), gf)


def _arrange_in_proj(w_in, s5_w, ssd_w, ssd_cdim, ssd_h, gdn_cdim, gdn_w, gdn_h):
    sizes = (s5_w, ssd_w, ssd_cdim, ssd_h, gdn_cdim, gdn_w, gdn_h, gdn_h)
    offs = np.cumsum((0,) + sizes)
    seg = lambda i: w_in[:, offs[i]:offs[i + 1]]
    small = jnp.concatenate([seg(3), seg(6), seg(7)], axis=1)
    small = jnp.pad(small, ((0, 0), (0, LANES - small.shape[1])))
    w = jnp.concatenate([seg(0), seg(1), seg(2), seg(4), seg(5), small], axis=1)
    return w.astype(BF16), (s5_w, ssd_w, ssd_cdim, gdn_cdim, gdn_w, LANES)


def kernel(x, norm_mix, w_in, w_out, s5_a_re, s5_a_im, s5_b_re, s5_b_im, s5_c_re, s5_c_im, s5_d, s5_log_step, s5_w_glu, s5_norm, ssd_conv_w, ssd_conv_b, ssd_dt_bias, ssd_a_log, ssd_d, ssd_norm, gdn_conv_w, gdn_a_log, gdn_dt_bias, gdn_norm, norm_ffn, ff_w_gate, ff_w_up, ff_w_down, moe_router, moe_w_gate, moe_w_up, moe_w_down, norm_final):
    bsz, seq, d = x.shape
    depth = norm_mix.shape[0]
    s5_w = s5_w_glu.shape[-1]
    ssd_w, ssd_h, ssd_cdim = ssd_norm.shape[-1], ssd_a_log.shape[-1], ssd_conv_w.shape[-1]
    gdn_h, gdn_cdim = gdn_a_log.shape[-1], gdn_conv_w.shape[-1]
    gdn_w = gdn_h * GDN_HEADDIM
    q = min(S5_CHUNK, seq)
    xr = x.reshape(bsz * seq, d)
    for layer in range(depth):
        w_arr, sizes = _arrange_in_proj(w_in[layer], s5_w, ssd_w, ssd_cdim, ssd_h, gdn_cdim, gdn_w, gdn_h)
        u_s5, z_ssd, xbc, qkv, z_gdn, small = _inproj(xr, norm_mix[layer], w_arr, sizes)
        tables = _s5_tables(s5_a_re[layer], s5_a_im[layer], s5_b_re[layer], s5_b_im[layer], s5_c_re[layer],
                            s5_c_im[layer], s5_d[layer], s5_log_step[layer], q, seq // q)
        y_s5 = _s5_scan(u_s5, tables, bsz, seq)
        y_ssd = _ssd(z_ssd, xbc, small, ssd_conv_w[layer], ssd_conv_b[layer], ssd_dt_bias[layer],
                     ssd_a_log[layer], ssd_d[layer], ssd_norm[layer], bsz, seq)
        y_gdn = _gdn(qkv, z_gdn, small, gdn_conv_w[layer], gdn_a_log[layer], gdn_dt_bias[layer],
                     gdn_norm[layer], bsz, seq)
        xr = _outproj(xr, y_s5, y_ssd, y_gdn, s5_w_glu[layer], s5_norm[layer], w_out[layer])
        final_gain = norm_final if layer == depth - 1 else None
        i = layer // 2
        if layer % 2 == 0:
            xr = _ffn(xr, norm_ffn[layer], ff_w_gate[i], ff_w_up[i], ff_w_down[i], final_gain)
        else:
            xr = _moe(xr, norm_ffn[layer], moe_router[i], moe_w_gate[i], moe_w_up[i], moe_w_down[i], final_gain)
    return xr.reshape(bsz, seq, d)
```

```python
import functools
import math

import jax
import jax.numpy as jnp
import numpy as np
from jax import lax
from jax.experimental import pallas as pl
from jax.experimental.pallas import tpu as pltpu

F32 = jnp.float32
BF16 = jnp.bfloat16
EPS = 1e-6

S5_GROUP = 16
S5_STATE = 64
SSD_HEADDIM = 64
SSD_NGROUPS = 2
SSD_STATE = 128
GDN_HEADDIM = 64
CONV_K = 4
TOP_K = 2

LANES = 128
SUBLANES = 8
VMEM_LIMIT_BYTES = 56 * 1024 * 1024

S5_CHUNK = 64
SSD_CHUNK = 128
GDN_CHUNK = 64
GDN_INV_BLOCK = 16
GDN_GROUP = 256
SEQ_BLOCK = 512
ROW_BLOCK = 512
FFN_ROW_BLOCK = 1024
FFN_COL_BLOCK = 512


def _params(*semantics):
    return pltpu.CompilerParams(dimension_semantics=semantics, vmem_limit_bytes=VMEM_LIMIT_BYTES)


def _dot(a, b):
    return jnp.dot(a.astype(BF16), b.astype(BF16), preferred_element_type=F32)


def _dot_nt(a, b):
    return lax.dot_general(a.astype(BF16), b.astype(BF16), (((1,), (1,)), ((), ())),
                           preferred_element_type=F32)


def _dot_tn(a, b):
    return lax.dot_general(a.astype(BF16), b.astype(BF16), (((0,), (0,)), ((), ())),
                           preferred_element_type=F32)


def _split3(a):
    hi = a.astype(BF16)
    r1 = a - hi.astype(F32)
    mid = r1.astype(BF16)
    lo = (r1 - mid.astype(F32)).astype(BF16)
    return hi, mid, lo


def _dot_sel(a, sel):
    hi, mid, lo = _split3(a)
    out = jnp.dot(hi, sel, preferred_element_type=F32)
    out += jnp.dot(mid, sel, preferred_element_type=F32)
    out += jnp.dot(lo, sel, preferred_element_type=F32)
    return out


def _sel_dot(sel, a):
    hi, mid, lo = _split3(a)
    out = jnp.dot(sel, hi, preferred_element_type=F32)
    out += jnp.dot(sel, mid, preferred_element_type=F32)
    out += jnp.dot(sel, lo, preferred_element_type=F32)
    return out


def _sigmoid(x):
    return 1.0 / (1.0 + jnp.exp(-x))


def _silu(x):
    return x * _sigmoid(x)


def _softplus(x):
    return jnp.maximum(x, 0.0) + jnp.log(1.0 + jnp.exp(-jnp.abs(x)))


def _gelu_tanh(x):
    c = math.sqrt(2.0 / math.pi)
    return 0.5 * x * (1.0 + jnp.tanh(c * (x + 0.044715 * (x * x * x))))


def _rms(x, w):
    return x * lax.rsqrt(jnp.mean(x * x, axis=-1, keepdims=True) + EPS) * w


def _iota2(shape, axis):
    return lax.broadcasted_iota(jnp.int32, shape, axis)


def _inproj_kernel(sizes, x_ref, g_ref, w_ref, *out_refs):
    h = _rms(x_ref[...], g_ref[...])
    p = jnp.dot(h.astype(BF16), w_ref[...], preferred_element_type=F32)
    off = 0
    for o_ref, sz in zip(out_refs, sizes):
        o_ref[...] = p[:, off:off + sz]
        off += sz


def _inproj(x2, gain, w_bf16, sizes):
    n, d = x2.shape
    tm = min(ROW_BLOCK, n)
    total = sum(sizes)
    return pl.pallas_call(
        functools.partial(_inproj_kernel, sizes),
        grid=(n // tm,),
        in_specs=[pl.BlockSpec((tm, d), lambda i: (i, 0)),
                  pl.BlockSpec((1, d), lambda i: (0, 0)),
                  pl.BlockSpec((d, total), lambda i: (0, 0))],
        out_specs=[pl.BlockSpec((tm, sz), lambda i: (i, 0)) for sz in sizes],
        out_shape=[jax.ShapeDtypeStruct((n, sz), F32) for sz in sizes],
        compiler_params=_params("parallel"),
        name="inproj",
    )(x2, gain.reshape(1, d), w_bf16)


def _s5_tables(a_re, a_im, b_re, b_im, c_re, c_im, d_skip, log_step, q, n_chunks):
    hp = lax.Precision.HIGHEST
    g, p = a_re.shape
    h = b_re.shape[-1]
    step = jnp.exp(log_step)[:, None]
    mag = jnp.exp(a_re * step)
    ang = a_im * step
    lb_re, lb_im = mag * jnp.cos(ang), mag * jnp.sin(ang)
    den = a_re * a_re + a_im * a_im
    f_re = ((lb_re - 1.0) * a_re + lb_im * a_im) / den
    f_im = (lb_im * a_re - (lb_re - 1.0) * a_im) / den
    bb_re = f_re[..., None] * b_re - f_im[..., None] * b_im
    bb_im = f_re[..., None] * b_im + f_im[..., None] * b_re

    def power(e):
        e = e.astype(F32)[:, None, None]
        m = jnp.exp(a_re * step * e)
        return m * jnp.cos(ang * e), m * jnp.sin(ang * e)

    pr, pi = power(jnp.arange(q + 1))
    wr = pr[..., None] * bb_re - pi[..., None] * bb_im
    wi = pr[..., None] * bb_im + pi[..., None] * bb_re
    kern = (jnp.einsum('ghp,tgpk->tghk', c_re, wr[:q], precision=hp)
            - jnp.einsum('ghp,tgpk->tghk', c_im, wi[:q], precision=hp))
    kern = kern.at[0].add(d_skip[:, :, None] * jnp.eye(h, dtype=F32))
    t_idx = jnp.arange(q)[:, None] - jnp.arange(q)[None, :]
    toep = jnp.where((t_idx >= 0)[:, :, None, None, None], kern[jnp.clip(t_idx, 0, q - 1)], 0.0)
    tt = toep.transpose(2, 1, 4, 0, 3).reshape(g, q * h, q * h)
    rev = q - 1 - jnp.arange(q)
    mt = jnp.concatenate([wr[rev], wi[rev]], axis=2)
    mt = mt.transpose(1, 0, 3, 2).reshape(g, q * h, 2 * p)
    p1r, p1i = pr[1:], pi[1:]
    n_re = c_re[None] * p1r[:, :, None, :] - c_im[None] * p1i[:, :, None, :]
    n_im = -(c_re[None] * p1i[:, :, None, :] + c_im[None] * p1r[:, :, None, :])
    nt = jnp.concatenate([n_re, n_im], axis=3).transpose(1, 3, 0, 2).reshape(g, 2 * p, q * h)
    n_steps = max(1, int(math.ceil(math.log2(max(n_chunks, 2)))))
    sr, si = power(q * (2 ** jnp.arange(n_steps)))
    ak = jnp.concatenate([sr, sr], axis=2).transpose(1, 0, 2)
    bk = jnp.concatenate([-si, si], axis=2).transpose(1, 0, 2)
    return tt, mt, nt, ak, bk


def _s5_kernel(n_chunks, n_steps, u_ref, tt_ref, mt_ref, nt_ref, ak_ref, bk_ref, y_ref):
    hp = lax.Precision.HIGHEST
    u = u_ref[0]
    y = jnp.dot(u, tt_ref[0], precision=hp, preferred_element_type=F32)
    s = jnp.dot(u, mt_ref[0], precision=hp, preferred_element_type=F32)
    rows, width = s.shape
    chunk = _iota2((rows, width), 0) % n_chunks
    ak = ak_ref[0]
    bk = bk_ref[0]
    for k in range(n_steps):
        sh = 2 ** k
        prev = jnp.where(chunk >= sh, pltpu.roll(s, sh, axis=0), 0.0)
        s = s + ak[k:k + 1] * prev + bk[k:k + 1] * pltpu.roll(prev, width // 2, axis=1)
    s_in = jnp.where(chunk >= 1, pltpu.roll(s, 1, axis=0), 0.0)
    y_ref[0] = y + jnp.dot(s_in, nt_ref[0], precision=hp, preferred_element_type=F32)


def _s5_scan(u, tables, bsz, seq):
    tt, mt, nt, ak, bk = tables
    g, qh, _ = tt.shape
    h = S5_GROUP
    q = qh // h
    n_chunks = seq // q
    rows = bsz * n_chunks
    n_steps = ak.shape[1]
    ug = u.reshape(bsz, n_chunks, q, g, h).transpose(3, 0, 1, 2, 4).reshape(g, rows, qh)
    yg = pl.pallas_call(
        functools.partial(_s5_kernel, n_chunks, n_steps),
        grid=(g,),
        in_specs=[pl.BlockSpec((1, rows, qh), lambda i: (i, 0, 0)),
                  pl.BlockSpec((1, qh, qh), lambda i: (i, 0, 0)),
                  pl.BlockSpec((1, qh, mt.shape[2]), lambda i: (i, 0, 0)),
                  pl.BlockSpec((1, nt.shape[1], qh), lambda i: (i, 0, 0)),
                  pl.BlockSpec((1,) + ak.shape[1:], lambda i: (i, 0, 0)),
                  pl.BlockSpec((1,) + bk.shape[1:], lambda i: (i, 0, 0))],
        out_specs=pl.BlockSpec((1, rows, qh), lambda i: (i, 0, 0)),
        out_shape=jax.ShapeDtypeStruct((g, rows, qh), F32),
        compiler_params=_params("parallel"),
        name="s5_scan",
    )(ug, tt, mt, nt, ak, bk)
    return yg.reshape(g, bsz, n_chunks, q, h).transpose(1, 2, 3, 0, 4).reshape(bsz * seq, g * h)


def _causal_conv_block(x_ref, w_ref, pad_ref, first):
    t = x_ref.shape[0]
    halo = SUBLANES

    @pl.when(first)
    def _():
        pad_ref[0:halo, :] = jnp.zeros((halo, pad_ref.shape[1]), F32)

    @pl.when(jnp.logical_not(first))
    def _():
        pad_ref[0:halo, :] = pad_ref[t:t + halo, :]

    pad_ref[halo:halo + t, :] = x_ref[...]
    acc = w_ref[CONV_K - 1:CONV_K, :] * pad_ref[halo:halo + t, :]
    for j in range(1, CONV_K):
        acc += w_ref[CONV_K - 1 - j:CONV_K - j, :] * pad_ref[halo - j:halo - j + t, :]
    return acc


def _ssd_kernel(heads, z_ref, xbc_ref, sm_ref, cw_ref, cb_ref, dtb_ref, a_ref, dexp_ref, nw_ref, ex_ref,
                o_ref, pad_ref, xc_ref, dt_ref, st_ref):
    t = xbc_ref.shape[0]
    q = min(SSD_CHUNK, t)
    width = heads * SSD_HEADDIM
    gw = SSD_NGROUPS * SSD_STATE
    first = pl.program_id(1) == 0

    @pl.when(first)
    def _():
        st_ref[...] = jnp.zeros(st_ref.shape, F32)

    xc_ref[...] = _silu(_causal_conv_block(xbc_ref, cw_ref, pad_ref, first) + cb_ref[...])
    dt_ref[...] = _softplus(sm_ref[...] + dtb_ref[...])

    row = _iota2((q, q), 0)
    col = _iota2((q, q), 1)
    causal = row >= col
    tril = jnp.where(causal, 1.0, 0.0).astype(BF16)
    lane = _iota2((1, width), 1)
    rep = heads // SSD_NGROUPS

    def chunk_body(ci, carry):
        r0 = pl.multiple_of(ci * q, q)
        xs = xc_ref[pl.ds(r0, q), 0:width]
        bm = xc_ref[pl.ds(r0, q), width:width + gw]
        cm = xc_ref[pl.ds(r0, q), width + gw:width + 2 * gw]
        dt = dt_ref[pl.ds(r0, q), :]
        acs = _sel_dot(tril, dt * a_ref[...])
        acs_t = acs.T
        xdt = xs * _dot_sel(dt, ex_ref[...])
        pieces = []
        for gi in range(SSD_NGROUPS):
            bg = bm[:, gi * SSD_STATE:(gi + 1) * SSD_STATE]
            cg = cm[:, gi * SSD_STATE:(gi + 1) * SSD_STATE]
            cb = _dot_nt(cg, bg)
            for hh in range(gi * rep, (gi + 1) * rep):
                acol = acs[:, hh:hh + 1]
                arow = acs_t[hh:hh + 1, :]
                alast = acs[q - 1:q, hh:hh + 1]
                diff = jnp.where(causal, acol - arow, 0.0)
                decay = jnp.where(causal, jnp.exp(diff), 0.0)
                xh = xdt[:, hh * SSD_HEADDIM:(hh + 1) * SSD_HEADDIM]
                y_diag = _dot(cb * decay, xh)
                state = st_ref[hh]
                y_off = _dot(cg, state) * jnp.exp(acol)
                st_ref[hh] = state * jnp.exp(alast) + _dot_tn(bg * jnp.exp(alast - acol), xh)
                pieces.append(y_diag + y_off)
        y = jnp.concatenate(pieces, axis=1) + dexp_ref[...] * xs
        y = y * _silu(z_ref[pl.ds(r0, q), :])
        gsz = width // SSD_NGROUPS
        y2 = y * y
        out = jnp.zeros_like(y)
        for gi in range(SSD_NGROUPS):
            m = (lane >= gi * gsz) & (lane < (gi + 1) * gsz)
            ms = jnp.sum(jnp.where(m, y2, 0.0), axis=-1, keepdims=True) / gsz
            out = jnp.where(m, y * lax.rsqrt(ms + EPS), out)
        o_ref[pl.ds(r0, q), :] = out * nw_ref[...]
        return carry

    lax.fori_loop(0, t // q, chunk_body, 0)


def _ssd(z, xbc, small, conv_w, conv_b, dt_bias, a_log, d_skip, norm_w, bsz, seq):
    n, width = z.shape
    heads = a_log.shape[0]
    cdim = xbc.shape[1]
    t = min(SEQ_BLOCK, seq)
    nblk = seq // t
    pad = lambda v: jnp.zeros((1, LANES), F32).at[0, :heads].set(v)
    expand = (jnp.arange(LANES)[:, None] == (jnp.arange(width)[None, :] // SSD_HEADDIM)).astype(BF16)
    tok = lambda b, c: (b * nblk + c, 0)
    const = lambda b, c: (0, 0)
    return pl.pallas_call(
        functools.partial(_ssd_kernel, heads),
        grid=(bsz, nblk),
        in_specs=[pl.BlockSpec((t, width), tok),
                  pl.BlockSpec((t, cdim), tok),
                  pl.BlockSpec((t, LANES), tok),
                  pl.BlockSpec((CONV_K, cdim), const),
                  pl.BlockSpec((1, cdim), const),
                  pl.BlockSpec((1, LANES), const),
                  pl.BlockSpec((1, LANES), const),
                  pl.BlockSpec((1, width), const),
                  pl.BlockSpec((1, width), const),
                  pl.BlockSpec((LANES, width), const)],
        out_specs=pl.BlockSpec((t, width), tok),
        out_shape=jax.ShapeDtypeStruct((n, width), F32),
        scratch_shapes=[pltpu.VMEM((SUBLANES + t, cdim), F32),
                        pltpu.VMEM((t, cdim), F32),
                        pltpu.VMEM((t, LANES), F32),
                        pltpu.VMEM((heads, SSD_STATE, SSD_HEADDIM), F32)],
        compiler_params=_params("parallel", "arbitrary"),
        name="ssd",
    )(z, xbc, small, conv_w, conv_b.reshape(1, cdim), pad(dt_bias), pad(-jnp.exp(a_log)),
      jnp.repeat(d_skip, SSD_HEADDIM).reshape(1, width), norm_w.reshape(1, width), expand)


def _mm(a, b):
    return jnp.dot(a, b, preferred_element_type=F32)


def _unit_lower_inverses(lms, eye, blk, chunk):
    lds = [jnp.where(blk, lm, 0.0) for lm in lms]
    offs = [(lm - ld).astype(BF16) for lm, ld in zip(lms, lds)]
    dinv = [eye - ld for ld in lds]
    pw = [ld.astype(BF16) for ld in lds]
    span = 2
    while span < GDN_INV_BLOCK:
        sq = [_mm(p, p) for p in pw]
        dinv = [_mm(d.astype(BF16), (eye + s).astype(BF16)) for d, s in zip(dinv, sq)]
        pw = [s.astype(BF16) for s in sq]
        span *= 2
    dinv = [d.astype(BF16) for d in dinv]
    ms = [_mm(d, o) for d, o in zip(dinv, offs)]
    out = [eye - m for m in ms]
    pw = [m.astype(BF16) for m in ms]
    span = 2
    while span < chunk // GDN_INV_BLOCK:
        sq = [_mm(p, p) for p in pw]
        out = [_mm(o.astype(BF16), (eye + s).astype(BF16)) for o, s in zip(out, sq)]
        pw = [s.astype(BF16) for s in sq]
        span *= 2
    return [_mm(o.astype(BF16), d) for o, d in zip(out, dinv)]


def _gdn_kernel(heads, qkv_ref, z_ref, sm_ref, cw_ref, dtb_ref, a_ref, nw_ref, exb_ref, exg_ref, ones_ref,
                o_ref, pad_ref, st_ref):
    t = qkv_ref.shape[0]
    c = min(GDN_CHUNK, t)
    hd = GDN_HEADDIM
    width = heads * hd
    hs = range(heads)
    sl = [slice(hh * hd, (hh + 1) * hd) for hh in hs]
    first = pl.program_id(1) == 0

    @pl.when(first)
    def _():
        st_ref[...] = jnp.zeros(st_ref.shape, F32)

    def seg_sum(v):
        hi = v.astype(BF16)
        lo = (v - hi.astype(F32)).astype(BF16)
        return _mm(hi, ones_ref[...]) + _mm(lo, ones_ref[...])

    xc = _silu(_causal_conv_block(qkv_ref, cw_ref, pad_ref, first))
    qf = xc[:, 0:width]
    kf = xc[:, width:2 * width]
    vf = xc[:, 2 * width:3 * width]
    qf = qf * lax.rsqrt(seg_sum(qf * qf) + EPS) * (hd ** -0.5)
    kf = kf * lax.rsqrt(seg_sum(kf * kf) + EPS)
    sm = sm_ref[...]
    beta = _dot_sel(_sigmoid(sm), exb_ref[...])
    g = a_ref[...] * _softplus(sm + dtb_ref[...])

    row = _iota2((t, t), 0)
    col = _iota2((t, t), 1)
    same = (row // c) == (col // c)
    causal = same & (row >= col)
    strict = same & (row > col)
    eye = jnp.where(row == col, 1.0, 0.0)
    blk = (row // GDN_INV_BLOCK) == (col // GDN_INV_BLOCK)
    gcs = _sel_dot(jnp.where(causal, 1.0, 0.0).astype(BF16), g)
    gtot = _sel_dot(jnp.where(same, 1.0, 0.0).astype(BF16), g)
    gcs_t = gcs.T
    gw = _dot_sel(gcs, exg_ref[...])
    gtw = _dot_sel(gtot, exg_ref[...])
    gexp = jnp.exp(gw)
    kb = kf * beta
    vb = vf * beta
    kcd_in = kb * gexp
    qd = (qf * gexp).astype(BF16)
    kend = (kf * jnp.exp(gtw - gw)).astype(BF16)
    cdec = jnp.exp(gtw)
    kfb = kf.astype(BF16)
    kbb = kb.astype(BF16)
    qfb = qf.astype(BF16)
    nt = (((1,), (1,)), ((), ()))
    decay = []
    for hh in hs:
        lg = 2 * heads + hh
        diff = jnp.where(causal, gcs[:, lg:lg + 1] - gcs_t[lg:lg + 1, :], 0.0)
        decay.append(jnp.where(causal, jnp.exp(diff), 0.0))
    kk = [lax.dot_general(kbb[:, sl[hh]], kfb[:, sl[hh]], nt, preferred_element_type=F32) for hh in hs]
    qk = [lax.dot_general(qfb[:, sl[hh]], kfb[:, sl[hh]], nt, preferred_element_type=F32) for hh in hs]
    lms = [jnp.where(strict, kk[hh] * decay[hh], 0.0) for hh in hs]
    qk = [(qk[hh] * decay[hh]).astype(BF16) for hh in hs]
    tinv = _unit_lower_inverses(lms, eye, blk, c)
    rhs = [jnp.concatenate([vb[:, sl[hh]], kcd_in[:, sl[hh]]], axis=1).astype(BF16) for hh in hs]
    vk = [_mm(tinv[hh].astype(BF16), rhs[hh]) for hh in hs]

    state = [st_ref[hh] for hh in hs]
    outs = []
    for ci in range(t // c):
        rs = slice(ci * c, (ci + 1) * c)
        lhs = [jnp.concatenate([vk[hh][rs, hd:2 * hd].astype(BF16), qd[rs, sl[hh]]], axis=0) for hh in hs]
        sb = [s.astype(BF16) for s in state]
        both = [_mm(lhs[hh], sb[hh]) for hh in hs]
        v_new = [(vk[hh][rs, 0:hd] - both[hh][0:c]).astype(BF16) for hh in hs]
        o = [both[hh][c:2 * c] + _mm(qk[hh][rs, ci * c:(ci + 1) * c], v_new[hh]) for hh in hs]
        upd = [lax.dot_general(kend[rs, sl[hh]], v_new[hh], (((0,), (0,)), ((), ())),
                               preferred_element_type=F32) for hh in hs]
        state = [state[hh] * cdec[ci * c:ci * c + 1, sl[hh]] + upd[hh] for hh in hs]
        outs.append(jnp.concatenate(o, axis=1))
    for hh in hs:
        st_ref[hh] = state[hh]
    o = jnp.concatenate(outs, axis=0)
    o = o * lax.rsqrt(seg_sum(o * o) / hd + EPS) * nw_ref[...]
    o_ref[...] = o * _silu(z_ref[...])


def _gdn(qkv, z, small, conv_w, a_log, dt_bias, norm_w, bsz, seq):
    n, width = z.shape
    heads = a_log.shape[0]
    cdim = qkv.shape[1]
    t = min(GDN_GROUP, seq)
    nblk = seq // t
    pad = lambda v, off: jnp.zeros((1, LANES), F32).at[0, off:off + heads].set(v)
    head_of = jnp.arange(width)[None, :] // GDN_HEADDIM
    lanes = jnp.arange(LANES)[:, None]
    exb = (lanes == heads + head_of).astype(BF16)
    exg = (lanes == 2 * heads + head_of).astype(BF16)
    ones = (head_of.T == head_of).astype(BF16)
    tok = lambda b, c: (b * nblk + c, 0)
    const = lambda b, c: (0, 0)
    return pl.pallas_call(
        functools.partial(_gdn_kernel, heads),
        grid=(bsz, nblk),
        in_specs=[pl.BlockSpec((t, cdim), tok),
                  pl.BlockSpec((t, width), tok),
                  pl.BlockSpec((t, LANES), tok),
                  pl.BlockSpec((CONV_K, cdim), const),
                  pl.BlockSpec((1, LANES), const),
                  pl.BlockSpec((1, LANES), const),
                  pl.BlockSpec((1, width), const),
                  pl.BlockSpec((LANES, width), const),
                  pl.BlockSpec((LANES, width), const),
                  pl.BlockSpec((width, width), const)],
        out_specs=pl.BlockSpec((t, width), tok),
        out_shape=jax.ShapeDtypeStruct((n, width), F32),
        scratch_shapes=[pltpu.VMEM((SUBLANES + t, cdim), F32),
                        pltpu.VMEM((heads, GDN_HEADDIM, GDN_HEADDIM), F32)],
        compiler_params=_params("parallel", "arbitrary"),
        name="gdn",
    )(qkv, z, small, conv_w, pad(dt_bias, 2 * heads), pad(-jnp.exp(a_log), 2 * heads),
      jnp.tile(norm_w, heads).reshape(1, width), exb, exg, ones)


def _outproj_kernel(x_ref, s5_ref, ssd_ref, gdn_ref, wglu_ref, s5n_ref, wout_ref, o_ref):
    y = _gelu_tanh(s5_ref[...])
    y = y * _sigmoid(jnp.dot(y.astype(BF16), wglu_ref[...], preferred_element_type=F32))
    y = _rms(y, s5n_ref[...])
    mix = jnp.concatenate([y, ssd_ref[...], gdn_ref[...]], axis=1)
    o_ref[...] = x_ref[...] + jnp.dot(mix.astype(BF16), wout_ref[...], preferred_element_type=F32)


def _outproj(x2, y_s5, y_ssd, y_gdn, w_glu, s5_norm, w_out):
    n, d = x2.shape
    tm = min(ROW_BLOCK, n)
    ws = [y_s5.shape[1], y_ssd.shape[1], y_gdn.shape[1]]
    row = lambda i: (i, 0)
    const = lambda i: (0, 0)
    return pl.pallas_call(
        _outproj_kernel,
        grid=(n // tm,),
        in_specs=[pl.BlockSpec((tm, d), row)] + [pl.BlockSpec((tm, w), row) for w in ws]
                 + [pl.BlockSpec((ws[0], ws[0]), const), pl.BlockSpec((1, ws[0]), const),
                    pl.BlockSpec((sum(ws), d), const)],
        out_specs=pl.BlockSpec((tm, d), row),
        out_shape=jax.ShapeDtypeStruct((n, d), F32),
        compiler_params=_params("parallel"),
        name="outproj",
    )(x2, y_s5, y_ssd, y_gdn, w_glu.astype(BF16), s5_norm.reshape(1, -1), w_out.astype(BF16))


def _ffn_kernel(final, x_ref, g_ref, wg_ref, wu_ref, wd_ref, gf_ref, o_ref, h_ref, acc_ref):
    f = pl.program_id(1)

    @pl.when(f == 0)
    def _():
        h_ref[...] = _rms(x_ref[...], g_ref[...]).astype(BF16)
        acc_ref[...] = jnp.zeros(acc_ref.shape, F32)

    h = h_ref[...]
    a = jnp.dot(h, wg_ref[...], preferred_element_type=F32)
    u = jnp.dot(h, wu_ref[...], preferred_element_type=F32)
    acc_ref[...] += jnp.dot((_silu(a) * u).astype(BF16), wd_ref[...], preferred_element_type=F32)

    @pl.when(f == pl.num_programs(1) - 1)
    def _():
        out = x_ref[...] + acc_ref[...]
        o_ref[...] = _rms(out, gf_ref[...]) if final else out


def _ffn(x2, gain, w_gate, w_up, w_down, final_gain):
    n, d = x2.shape
    ff = w_gate.shape[1]
    tm = min(FFN_ROW_BLOCK, n)
    tf = min(FFN_COL_BLOCK, ff)
    final = final_gain is not None
    gf = (final_gain if final else gain).reshape(1, d)
    return pl.pallas_call(
        functools.partial(_ffn_kernel, final),
        grid=(n // tm, ff // tf),
        in_specs=[pl.BlockSpec((tm, d), lambda i, f: (i, 0)),
                  pl.BlockSpec((1, d), lambda i, f: (0, 0)),
                  pl.BlockSpec((d, tf), lambda i, f: (0, f)),
                  pl.BlockSpec((d, tf), lambda i, f: (0, f)),
                  pl.BlockSpec((tf, d), lambda i, f: (f, 0)),
                  pl.BlockSpec((1, d), lambda i, f: (0, 0))],
        out_specs=pl.BlockSpec((tm, d), lambda i, f: (i, 0)),
        out_shape=jax.ShapeDtypeStruct((n, d), F32),
        scratch_shapes=[pltpu.VMEM((tm, d), BF16), pltpu.VMEM((tm, d), F32)],
        compiler_params=_params("parallel", "arbitrary"),
        name="ffn",
    )(x2, gain.reshape(1, d), w_gate.astype(BF16), w_up.astype(BF16), w_down.astype(BF16), gf)


def _moe_kernel(final, n_exp, x_ref, g_ref, wr_ref, wg_ref, wu_ref, wd_ref, gf_ref, o_ref,
                h_ref, acc_ref, cmb_ref):
    e = pl.program_id(1)
    f = pl.program_id(2)

    @pl.when((e == 0) & (f == 0))
    def _():
        hn = _rms(x_ref[...], g_ref[...])
        h_ref[...] = hn.astype(BF16)
        acc_ref[...] = jnp.zeros(acc_ref.shape, F32)
        logits = jnp.dot(hn, wr_ref[...], precision=lax.Precision.HIGHEST, preferred_element_type=F32)
        lane = _iota2(logits.shape, 1)
        neg = jnp.float32(-jnp.inf)
        logits = jnp.where(lane < n_exp, logits, neg)
        m1 = jnp.max(logits, axis=-1, keepdims=True)
        i1 = jnp.min(jnp.where(logits == m1, lane, LANES), axis=-1, keepdims=True)
        rest = jnp.where(lane == i1, neg, logits)
        m2 = jnp.max(rest, axis=-1, keepdims=True)
        i2 = jnp.min(jnp.where(rest == m2, lane, LANES), axis=-1, keepdims=True)
        w2 = 1.0 / (1.0 + jnp.exp(m1 - m2))
        w1 = 1.0 / (1.0 + jnp.exp(m2 - m1))
        cmb_ref[...] = jnp.where(lane == i1, w1, 0.0) + jnp.where(lane == i2, w2, 0.0)

    cmb = cmb_ref[...]
    lane = _iota2(cmb.shape, 1)
    ce = jnp.sum(jnp.where(lane == e, cmb, 0.0), axis=-1, keepdims=True)
    h = h_ref[...]
    a = jnp.dot(h, wg_ref[0], preferred_element_type=F32)
    u = jnp.dot(h, wu_ref[0], preferred_element_type=F32)
    acc_ref[...] += ce * jnp.dot((_silu(a) * u).astype(BF16), wd_ref[0], preferred_element_type=F32)

    @pl.when((e == pl.num_programs(1) - 1) & (f == pl.num_programs(2) - 1))
    def _():
        out = x_ref[...] + acc_ref[...]
        o_ref[...] = _rms(out, gf_ref[...]) if final else out


def _moe(x2, gain, w_router, w_gate, w_up, w_down, final_gain):
    n, d = x2.shape
    n_exp, _, ff = w_gate.shape
    tm = min(FFN_ROW_BLOCK, n)
    tf = min(FFN_COL_BLOCK, ff)
    final = final_gain is not None
    gf = (final_gain if final else gain).reshape(1, d)
    wr = jnp.zeros((d, LANES), F32).at[:, :n_exp].set(w_router)
    return pl.pallas_call(
        functools.partial(_moe_kernel, final, n_exp),
        grid=(n // tm, n_exp, ff // tf),
        in_specs=[pl.BlockSpec((tm, d), lambda i, e, f: (i, 0)),
                  pl.BlockSpec((1, d), lambda i, e, f: (0, 0)),
                  pl.BlockSpec((d, LANES), lambda i, e, f: (0, 0)),
                  pl.BlockSpec((1, d, tf), lambda i, e, f: (e, 0, f)),
                  pl.BlockSpec((1, d, tf), lambda i, e, f: (e, 0, f)),
                  pl.BlockSpec((1, tf, d), lambda i, e, f: (e, f, 0)),
                  pl.BlockSpec((1, d), lambda i, e, f: (0, 0))],
        out_specs=pl.BlockSpec((tm, d), lambda i, e, f: (i, 0)),
        out_shape=jax.ShapeDtypeStruct((n, d), F32),
        scratch_shapes=[pltpu.VMEM((tm, d), BF16), pltpu.VMEM((tm, d), F32), pltpu.VMEM((tm, LANES), F32)],
        compiler_params=_params("parallel", "arbitrary", "arbitrary"),
        name="moe",
    )(x2, gain.reshape(1, d), wr, w_gate.astype(BF16), w_up.astype(BF16), w_down.astype(BF16), gf)


def _arrange_in_proj(w_in, s5_w, ssd_w, ssd_cdim, ssd_h, gdn_cdim, gdn_w, gdn_h):
    sizes = (s5_w, ssd_w, ssd_cdim, ssd_h, gdn_cdim, gdn_w, gdn_h, gdn_h)
    offs = np.cumsum((0,) + sizes)
    seg = lambda i: w_in[:, offs[i]:offs[i + 1]]
    small = jnp.concatenate([seg(3), seg(6), seg(7)], axis=1)
    small = jnp.pad(small, ((0, 0), (0, LANES - small.shape[1])))
    w = jnp.concatenate([seg(0), seg(1), seg(2), seg(4), seg(5), small], axis=1)
    return w.astype(BF16), (s5_w, ssd_w, ssd_cdim, gdn_cdim, gdn_w, LANES)


def kernel(x, norm_mix, w_in, w_out, s5_a_re, s5_a_im, s5_b_re, s5_b_im, s5_c_re, s5_c_im, s5_d, s5_log_step, s5_w_glu, s5_norm, ssd_conv_w, ssd_conv_b, ssd_dt_bias, ssd_a_log, ssd_d, ssd_norm, gdn_conv_w, gdn_a_log, gdn_dt_bias, gdn_norm, norm_ffn, ff_w_gate, ff_w_up, ff_w_down, moe_router, moe_w_gate, moe_w_up, moe_w_down, norm_final):
    bsz, seq, d = x.shape
    depth = norm_mix.shape[0]
    s5_w = s5_w_glu.shape[-1]
    ssd_w, ssd_h, ssd_cdim = ssd_norm.shape[-1], ssd_a_log.shape[-1], ssd_conv_w.shape[-1]
    gdn_h, gdn_cdim = gdn_a_log.shape[-1], gdn_conv_w.shape[-1]
    gdn_w = gdn_h * GDN_HEADDIM
    q = min(S5_CHUNK, seq)
    xr = x.reshape(bsz * seq, d)
    for layer in range(depth):
        w_arr, sizes = _arrange_in_proj(w_in[layer], s5_w, ssd_w, ssd_cdim, ssd_h, gdn_cdim, gdn_w, gdn_h)
        u_s5, z_ssd, xbc, qkv, z_gdn, small = _inproj(xr, norm_mix[layer], w_arr, sizes)
        tables = _s5_tables(s5_a_re[layer], s5_a_im[layer], s5_b_re[layer], s5_b_im[layer], s5_c_re[layer],
                            s5_c_im[layer], s5_d[layer], s5_log_step[layer], q, seq // q)
        y_s5 = _s5_scan(u_s5, tables, bsz, seq)
        y_ssd = _ssd(z_ssd, xbc, small, ssd_conv_w[layer], ssd_conv_b[layer], ssd_dt_bias[layer],
                     ssd_a_log[layer], ssd_d[layer], ssd_norm[layer], bsz, seq)
        y_gdn = _gdn(qkv, z_gdn, small, gdn_conv_w[layer], gdn_a_log[layer], gdn_dt_bias[layer],
                     gdn_norm[layer], bsz, seq)
        xr = _outproj(xr, y_s5, y_ssd, y_gdn, s5_w_glu[layer], s5_norm[layer], w_out[layer])
        final_gain = norm_final if layer == depth - 1 else None
        i = layer // 2
        if layer % 2 == 0:
            xr = _ffn(xr, norm_ffn[layer], ff_w_gate[i], ff_w_up[i], ff_w_down[i], final_gain)
        else:
            xr = _moe(xr, norm_ffn[layer], moe_router[i], moe_w_gate[i], moe_w_up[i], moe_w_down[i], final_gain)
    return xr.reshape(bsz, seq, d)
```

```python
import functools
import math

import jax
import jax.numpy as jnp
import numpy as np
from jax import lax
from jax.experimental import pallas as pl
from jax.experimental.pallas import tpu as pltpu

F32 = jnp.float32
BF16 = jnp.bfloat16
EPS = 1e-6

S5_GROUP = 16
S5_STATE = 64
SSD_HEADDIM = 64
SSD_NGROUPS = 2
SSD_STATE = 128
GDN_HEADDIM = 64
CONV_K = 4
TOP_K = 2

LANES = 128
SUBLANES = 8
VMEM_LIMIT_BYTES = 56 * 1024 * 1024

S5_CHUNK = 16
SSD_CHUNK = 128
GDN_CHUNK = 64
GDN_INV_BLOCK = 16
GDN_GROUP = 256
SEQ_BLOCK = 512
ROW_BLOCK = 512
FFN_ROW_BLOCK = 1024
FFN_COL_BLOCK = 512
ROUTER_BLOCK = 1024
DISPATCH_BLOCK = 1024
MOE_ROW_BLOCK = 1024
COMBINE_BLOCK = 512


def _params(*semantics):
    return pltpu.CompilerParams(dimension_semantics=semantics, vmem_limit_bytes=VMEM_LIMIT_BYTES)


def _dot(a, b):
    return jnp.dot(a.astype(BF16), b.astype(BF16), preferred_element_type=F32)


def _dot_nt(a, b):
    return lax.dot_general(a.astype(BF16), b.astype(BF16), (((1,), (1,)), ((), ())),
                           preferred_element_type=F32)


def _dot_tn(a, b):
    return lax.dot_general(a.astype(BF16), b.astype(BF16), (((0,), (0,)), ((), ())),
                           preferred_element_type=F32)


def _split3(a):
    hi = a.astype(BF16)
    r1 = a - hi.astype(F32)
    mid = r1.astype(BF16)
    lo = (r1 - mid.astype(F32)).astype(BF16)
    return hi, mid, lo


def _dot_sel(a, sel):
    hi, mid, lo = _split3(a)
    out = jnp.dot(hi, sel, preferred_element_type=F32)
    out += jnp.dot(mid, sel, preferred_element_type=F32)
    out += jnp.dot(lo, sel, preferred_element_type=F32)
    return out


def _sel_dot(sel, a):
    hi, mid, lo = _split3(a)
    out = jnp.dot(sel, hi, preferred_element_type=F32)
    out += jnp.dot(sel, mid, preferred_element_type=F32)
    out += jnp.dot(sel, lo, preferred_element_type=F32)
    return out


def _sigmoid(x):
    return 1.0 / (1.0 + jnp.exp(-x))


def _silu(x):
    return x * _sigmoid(x)


def _softplus(x):
    return jnp.maximum(x, 0.0) + jnp.log(1.0 + jnp.exp(-jnp.abs(x)))


def _gelu_tanh(x):
    c = math.sqrt(2.0 / math.pi)
    return 0.5 * x * (1.0 + jnp.tanh(c * (x + 0.044715 * (x * x * x))))


def _rms(x, w):
    return x * lax.rsqrt(jnp.mean(x * x, axis=-1, keepdims=True) + EPS) * w


def _iota2(shape, axis):
    return lax.broadcasted_iota(jnp.int32, shape, axis)


def _inproj_kernel(sizes, x_ref, g_ref, w_ref, *out_refs):
    h = _rms(x_ref[...], g_ref[...])
    p = jnp.dot(h.astype(BF16), w_ref[...], preferred_element_type=F32)
    off = 0
    for o_ref, sz in zip(out_refs, sizes):
        o_ref[...] = p[:, off:off + sz]
        off += sz


def _inproj(x2, gain, w_bf16, sizes):
    n, d = x2.shape
    tm = min(ROW_BLOCK, n)
    total = sum(sizes)
    return pl.pallas_call(
        functools.partial(_inproj_kernel, sizes),
        grid=(n // tm,),
        in_specs=[pl.BlockSpec((tm, d), lambda i: (i, 0)),
                  pl.BlockSpec((1, d), lambda i: (0, 0)),
                  pl.BlockSpec((d, total), lambda i: (0, 0))],
        out_specs=[pl.BlockSpec((tm, sz), lambda i: (i, 0)) for sz in sizes],
        out_shape=[jax.ShapeDtypeStruct((n, sz), F32) for sz in sizes],
        compiler_params=_params("parallel"),
        name="inproj",
    )(x2, gain.reshape(1, d), w_bf16)


def _s5_tables(a_re, a_im, b_re, b_im, c_re, c_im, d_skip, log_step, q, n_chunks):
    hp = lax.Precision.HIGHEST
    g, p = a_re.shape
    h = b_re.shape[-1]
    step = jnp.exp(log_step)[:, None]
    mag = jnp.exp(a_re * step)
    ang = a_im * step
    lb_re, lb_im = mag * jnp.cos(ang), mag * jnp.sin(ang)
    den = a_re * a_re + a_im * a_im
    f_re = ((lb_re - 1.0) * a_re + lb_im * a_im) / den
    f_im = (lb_im * a_re - (lb_re - 1.0) * a_im) / den
    bb_re = f_re[..., None] * b_re - f_im[..., None] * b_im
    bb_im = f_re[..., None] * b_im + f_im[..., None] * b_re

    def power(e):
        e = e.astype(F32)[:, None, None]
        m = jnp.exp(a_re * step * e)
        return m * jnp.cos(ang * e), m * jnp.sin(ang * e)

    pr, pi = power(jnp.arange(q + 1))
    wr = pr[..., None] * bb_re - pi[..., None] * bb_im
    wi = pr[..., None] * bb_im + pi[..., None] * bb_re
    kern = (jnp.einsum('ghp,tgpk->tghk', c_re, wr[:q], precision=hp)
            - jnp.einsum('ghp,tgpk->tghk', c_im, wi[:q], precision=hp))
    kern = kern.at[0].add(d_skip[:, :, None] * jnp.eye(h, dtype=F32))
    t_idx = jnp.arange(q)[:, None] - jnp.arange(q)[None, :]
    toep = jnp.where((t_idx >= 0)[:, :, None, None, None], kern[jnp.clip(t_idx, 0, q - 1)], 0.0)
    tt = toep.transpose(2, 1, 4, 0, 3).reshape(g, q * h, q * h)
    rev = q - 1 - jnp.arange(q)
    mt = jnp.concatenate([wr[rev], wi[rev]], axis=2)
    mt = mt.transpose(1, 0, 3, 2).reshape(g, q * h, 2 * p)
    p1r, p1i = pr[1:], pi[1:]
    n_re = c_re[None] * p1r[:, :, None, :] - c_im[None] * p1i[:, :, None, :]
    n_im = -(c_re[None] * p1i[:, :, None, :] + c_im[None] * p1r[:, :, None, :])
    nt = jnp.concatenate([n_re, n_im], axis=3).transpose(1, 3, 0, 2).reshape(g, 2 * p, q * h)
    n_steps = max(1, int(math.ceil(math.log2(max(n_chunks, 2)))))
    sr, si = power(q * (2 ** jnp.arange(n_steps)))
    ak = jnp.concatenate([sr, sr], axis=2).transpose(1, 0, 2)
    bk = jnp.concatenate([-si, si], axis=2).transpose(1, 0, 2)
    return tt, mt, nt, ak, bk


def _s5_kernel(n_chunks, n_steps, u_ref, tt_ref, mt_ref, nt_ref, ak_ref, bk_ref, y_ref):
    hp = lax.Precision.HIGHEST
    u = u_ref[0]
    y = jnp.dot(u, tt_ref[0], precision=hp, preferred_element_type=F32)
    s = jnp.dot(u, mt_ref[0], precision=hp, preferred_element_type=F32)
    rows, width = s.shape
    chunk = _iota2((rows, width), 0) % n_chunks
    ak = ak_ref[0]
    bk = bk_ref[0]
    for k in range(n_steps):
        sh = 2 ** k
        prev = jnp.where(chunk >= sh, pltpu.roll(s, sh, axis=0), 0.0)
        s = s + ak[k:k + 1] * prev + bk[k:k + 1] * pltpu.roll(prev, width // 2, axis=1)
    s_in = jnp.where(chunk >= 1, pltpu.roll(s, 1, axis=0), 0.0)
    y_ref[0] = y + jnp.dot(s_in, nt_ref[0], precision=hp, preferred_element_type=F32)


def _s5_scan(u, tables, bsz, seq):
    tt, mt, nt, ak, bk = tables
    g, qh, _ = tt.shape
    h = S5_GROUP
    q = qh // h
    n_chunks = seq // q
    rows = bsz * n_chunks
    n_steps = ak.shape[1]
    ug = u.reshape(bsz, n_chunks, q, g, h).transpose(3, 0, 1, 2, 4).reshape(g, rows, qh)
    yg = pl.pallas_call(
        functools.partial(_s5_kernel, n_chunks, n_steps),
        grid=(g,),
        in_specs=[pl.BlockSpec((1, rows, qh), lambda i: (i, 0, 0)),
                  pl.BlockSpec((1, qh, qh), lambda i: (i, 0, 0)),
                  pl.BlockSpec((1, qh, mt.shape[2]), lambda i: (i, 0, 0)),
                  pl.BlockSpec((1, nt.shape[1], qh), lambda i: (i, 0, 0)),
                  pl.BlockSpec((1,) + ak.shape[1:], lambda i: (i, 0, 0)),
                  pl.BlockSpec((1,) + bk.shape[1:], lambda i: (i, 0, 0))],
        out_specs=pl.BlockSpec((1, rows, qh), lambda i: (i, 0, 0)),
        out_shape=jax.ShapeDtypeStruct((g, rows, qh), F32),
        compiler_params=_params("parallel"),
        name="s5_scan",
    )(ug, tt, mt, nt, ak, bk)
    return yg.reshape(g, bsz, n_chunks, q, h).transpose(1, 2, 3, 0, 4).reshape(bsz * seq, g * h)


def _causal_conv_block(x_ref, w_ref, pad_ref, first):
    t = x_ref.shape[0]
    halo = SUBLANES

    @pl.when(first)
    def _():
        pad_ref[0:halo, :] = jnp.zeros((halo, pad_ref.shape[1]), F32)

    @pl.when(jnp.logical_not(first))
    def _():
        pad_ref[0:halo, :] = pad_ref[t:t + halo, :]

    pad_ref[halo:halo + t, :] = x_ref[...]
    acc = w_ref[CONV_K - 1:CONV_K, :] * pad_ref[halo:halo + t, :]
    for j in range(1, CONV_K):
        acc += w_ref[CONV_K - 1 - j:CONV_K - j, :] * pad_ref[halo - j:halo - j + t, :]
    return acc


def _ssd_kernel(heads, z_ref, xbc_ref, sm_ref, cw_ref, cb_ref, dtb_ref, a_ref, dexp_ref, nw_ref, ex_ref,
                o_ref, pad_ref, xc_ref, dt_ref, st_ref):
    t = xbc_ref.shape[0]
    q = min(SSD_CHUNK, t)
    width = heads * SSD_HEADDIM
    gw = SSD_NGROUPS * SSD_STATE
    first = pl.program_id(1) == 0

    @pl.when(first)
    def _():
        st_ref[...] = jnp.zeros(st_ref.shape, F32)

    xc_ref[...] = _silu(_causal_conv_block(xbc_ref, cw_ref, pad_ref, first) + cb_ref[...])
    dt_ref[...] = _softplus(sm_ref[...] + dtb_ref[...])

    row = _iota2((q, q), 0)
    col = _iota2((q, q), 1)
    causal = row >= col
    tril = jnp.where(causal, 1.0, 0.0).astype(BF16)
    lane = _iota2((1, width), 1)
    rep = heads // SSD_NGROUPS

    def chunk_body(ci, carry):
        r0 = pl.multiple_of(ci * q, q)
        xs = xc_ref[pl.ds(r0, q), 0:width]
        bm = xc_ref[pl.ds(r0, q), width:width + gw]
        cm = xc_ref[pl.ds(r0, q), width + gw:width + 2 * gw]
        dt = dt_ref[pl.ds(r0, q), :]
        acs = _sel_dot(tril, dt * a_ref[...])
        acs_t = acs.T
        xdt = xs * _dot_sel(dt, ex_ref[...])
        hs = range(heads)
        bgs = [bm[:, gi * SSD_STATE:(gi + 1) * SSD_STATE] for gi in range(SSD_NGROUPS)]
        cgs = [cm[:, gi * SSD_STATE:(gi + 1) * SSD_STATE].astype(BF16) for gi in range(SSD_NGROUPS)]
        cbs = [_dot_nt(cgs[gi], bgs[gi]) for gi in range(SSD_NGROUPS)]
        xdb = xdt.astype(BF16)
        xh = [xdb[:, hh * SSD_HEADDIM:(hh + 1) * SSD_HEADDIM] for hh in hs]
        acol = [acs[:, hh:hh + 1] for hh in hs]
        alast = [acs[q - 1:q, hh:hh + 1] for hh in hs]
        decay = [jnp.where(causal, jnp.exp(jnp.where(causal, acol[hh] - acs_t[hh:hh + 1, :], 0.0)), 0.0)
                 for hh in hs]
        state = [st_ref[hh] for hh in hs]
        y_diag = [_mm((cbs[hh // rep] * decay[hh]).astype(BF16), xh[hh]) for hh in hs]
        y_off = [_mm(cgs[hh // rep], state[hh].astype(BF16)) * jnp.exp(acol[hh]) for hh in hs]
        upd = [_dot_tn(bgs[hh // rep] * jnp.exp(alast[hh] - acol[hh]), xh[hh]) for hh in hs]
        for hh in hs:
            st_ref[hh] = state[hh] * jnp.exp(alast[hh]) + upd[hh]
        y = jnp.concatenate([y_diag[hh] + y_off[hh] for hh in hs], axis=1) + dexp_ref[...] * xs
        y = y * _silu(z_ref[pl.ds(r0, q), :])
        gsz = width // SSD_NGROUPS
        y2 = y * y
        out = jnp.zeros_like(y)
        for gi in range(SSD_NGROUPS):
            m = (lane >= gi * gsz) & (lane < (gi + 1) * gsz)
            ms = jnp.sum(jnp.where(m, y2, 0.0), axis=-1, keepdims=True) / gsz
            out = jnp.where(m, y * lax.rsqrt(ms + EPS), out)
        o_ref[pl.ds(r0, q), :] = out * nw_ref[...]
        return carry

    lax.fori_loop(0, t // q, chunk_body, 0)


def _ssd(z, xbc, small, conv_w, conv_b, dt_bias, a_log, d_skip, norm_w, bsz, seq):
    n, width = z.shape
    heads = a_log.shape[0]
    cdim = xbc.shape[1]
    t = min(SEQ_BLOCK, seq)
    nblk = seq // t
    pad = lambda v: jnp.zeros((1, LANES), F32).at[0, :heads].set(v)
    expand = (jnp.arange(LANES)[:, None] == (jnp.arange(width)[None, :] // SSD_HEADDIM)).astype(BF16)
    tok = lambda b, c: (b * nblk + c, 0)
    const = lambda b, c: (0, 0)
    return pl.pallas_call(
        functools.partial(_ssd_kernel, heads),
        grid=(bsz, nblk),
        in_specs=[pl.BlockSpec((t, width), tok),
                  pl.BlockSpec((t, cdim), tok),
                  pl.BlockSpec((t, LANES), tok),
                  pl.BlockSpec((CONV_K, cdim), const),
                  pl.BlockSpec((1, cdim), const),
                  pl.BlockSpec((1, LANES), const),
                  pl.BlockSpec((1, LANES), const),
                  pl.BlockSpec((1, width), const),
                  pl.BlockSpec((1, width), const),
                  pl.BlockSpec((LANES, width), const)],
        out_specs=pl.BlockSpec((t, width), tok),
        out_shape=jax.ShapeDtypeStruct((n, width), F32),
        scratch_shapes=[pltpu.VMEM((SUBLANES + t, cdim), F32),
                        pltpu.VMEM((t, cdim), F32),
                        pltpu.VMEM((t, LANES), F32),
                        pltpu.VMEM((heads, SSD_STATE, SSD_HEADDIM), F32)],
        compiler_params=_params("parallel", "arbitrary"),
        name="ssd",
    )(z, xbc, small, conv_w, conv_b.reshape(1, cdim), pad(dt_bias), pad(-jnp.exp(a_log)),
      jnp.repeat(d_skip, SSD_HEADDIM).reshape(1, width), norm_w.reshape(1, width), expand)


def _mm(a, b):
    return jnp.dot(a, b, preferred_element_type=F32)


def _unit_lower_inverses(lms, eye, blk, chunk):
    lds = [jnp.where(blk, lm, 0.0) for lm in lms]
    offs = [(lm - ld).astype(BF16) for lm, ld in zip(lms, lds)]
    dinv = [eye - ld for ld in lds]
    pw = [ld.astype(BF16) for ld in lds]
    span = 2
    while span < GDN_INV_BLOCK:
        sq = [_mm(p, p) for p in pw]
        dinv = [_mm(d.astype(BF16), (eye + s).astype(BF16)) for d, s in zip(dinv, sq)]
        pw = [s.astype(BF16) for s in sq]
        span *= 2
    dinv = [d.astype(BF16) for d in dinv]
    ms = [_mm(d, o) for d, o in zip(dinv, offs)]
    out = [eye - m for m in ms]
    pw = [m.astype(BF16) for m in ms]
    span = 2
    while span < chunk // GDN_INV_BLOCK:
        sq = [_mm(p, p) for p in pw]
        out = [_mm(o.astype(BF16), (eye + s).astype(BF16)) for o, s in zip(out, sq)]
        pw = [s.astype(BF16) for s in sq]
        span *= 2
    return [_mm(o.astype(BF16), d) for o, d in zip(out, dinv)]


def _gdn_kernel(heads, qkv_ref, z_ref, sm_ref, cw_ref, dtb_ref, a_ref, nw_ref, exb_ref, exg_ref, ones_ref,
                o_ref, pad_ref, st_ref):
    t = qkv_ref.shape[0]
    c = min(GDN_CHUNK, t)
    hd = GDN_HEADDIM
    width = heads * hd
    hs = range(heads)
    sl = [slice(hh * hd, (hh + 1) * hd) for hh in hs]
    first = pl.program_id(1) == 0

    @pl.when(first)
    def _():
        st_ref[...] = jnp.zeros(st_ref.shape, F32)

    def seg_sum(v):
        hi = v.astype(BF16)
        lo = (v - hi.astype(F32)).astype(BF16)
        return _mm(hi, ones_ref[...]) + _mm(lo, ones_ref[...])

    xc = _silu(_causal_conv_block(qkv_ref, cw_ref, pad_ref, first))
    qf = xc[:, 0:width]
    kf = xc[:, width:2 * width]
    vf = xc[:, 2 * width:3 * width]
    qf = qf * lax.rsqrt(seg_sum(qf * qf) + EPS) * (hd ** -0.5)
    kf = kf * lax.rsqrt(seg_sum(kf * kf) + EPS)
    sm = sm_ref[...]
    beta = _dot_sel(_sigmoid(sm), exb_ref[...])
    g = a_ref[...] * _softplus(sm + dtb_ref[...])

    row = _iota2((t, t), 0)
    col = _iota2((t, t), 1)
    same = (row // c) == (col // c)
    causal = same & (row >= col)
    strict = same & (row > col)
    eye = jnp.where(row == col, 1.0, 0.0)
    blk = (row // GDN_INV_BLOCK) == (col // GDN_INV_BLOCK)
    gcs = _sel_dot(jnp.where(causal, 1.0, 0.0).astype(BF16), g)
    gtot = _sel_dot(jnp.where(same, 1.0, 0.0).astype(BF16), g)
    gcs_t = gcs.T
    gw = _dot_sel(gcs, exg_ref[...])
    gtw = _dot_sel(gtot, exg_ref[...])
    gexp = jnp.exp(gw)
    kb = kf * beta
    vb = vf * beta
    kcd_in = kb * gexp
    qd = (qf * gexp).astype(BF16)
    kend = (kf * jnp.exp(gtw - gw)).astype(BF16)
    cdec = jnp.exp(gtw)
    kfb = kf.astype(BF16)
    kbb = kb.astype(BF16)
    qfb = qf.astype(BF16)
    nt = (((1,), (1,)), ((), ()))
    decay = []
    for hh in hs:
        lg = 2 * heads + hh
        diff = jnp.where(causal, gcs[:, lg:lg + 1] - gcs_t[lg:lg + 1, :], 0.0)
        decay.append(jnp.where(causal, jnp.exp(diff), 0.0))
    kk = [lax.dot_general(kbb[:, sl[hh]], kfb[:, sl[hh]], nt, preferred_element_type=F32) for hh in hs]
    qk = [lax.dot_general(qfb[:, sl[hh]], kfb[:, sl[hh]], nt, preferred_element_type=F32) for hh in hs]
    lms = [jnp.where(strict, kk[hh] * decay[hh], 0.0) for hh in hs]
    qk = [(qk[hh] * decay[hh]).astype(BF16) for hh in hs]
    tinv = _unit_lower_inverses(lms, eye, blk, c)
    rhs = [jnp.concatenate([vb[:, sl[hh]], kcd_in[:, sl[hh]]], axis=1).astype(BF16) for hh in hs]
    vk = [_mm(tinv[hh].astype(BF16), rhs[hh]) for hh in hs]

    state = [st_ref[hh] for hh in hs]
    outs = []
    for ci in range(t // c):
        rs = slice(ci * c, (ci + 1) * c)
        lhs = [jnp.concatenate([vk[hh][rs, hd:2 * hd].astype(BF16), qd[rs, sl[hh]]], axis=0) for hh in hs]
        sb = [s.astype(BF16) for s in state]
        both = [_mm(lhs[hh], sb[hh]) for hh in hs]
        v_new = [(vk[hh][rs, 0:hd] - both[hh][0:c]).astype(BF16) for hh in hs]
        o = [both[hh][c:2 * c] + _mm(qk[hh][rs, ci * c:(ci + 1) * c], v_new[hh]) for hh in hs]
        upd = [lax.dot_general(kend[rs, sl[hh]], v_new[hh], (((0,), (0,)), ((), ())),
                               preferred_element_type=F32) for hh in hs]
        state = [state[hh] * cdec[ci * c:ci * c + 1, sl[hh]] + upd[hh] for hh in hs]
        outs.append(jnp.concatenate(o, axis=1))
    for hh in hs:
        st_ref[hh] = state[hh]
    o = jnp.concatenate(outs, axis=0)
    o = o * lax.rsqrt(seg_sum(o * o) / hd + EPS) * nw_ref[...]
    o_ref[...] = o * _silu(z_ref[...])


def _gdn(qkv, z, small, conv_w, a_log, dt_bias, norm_w, bsz, seq):
    n, width = z.shape
    heads = a_log.shape[0]
    cdim = qkv.shape[1]
    t = min(GDN_GROUP, seq)
    nblk = seq // t
    pad = lambda v, off: jnp.zeros((1, LANES), F32).at[0, off:off + heads].set(v)
    head_of = jnp.arange(width)[None, :] // GDN_HEADDIM
    lanes = jnp.arange(LANES)[:, None]
    exb = (lanes == heads + head_of).astype(BF16)
    exg = (lanes == 2 * heads + head_of).astype(BF16)
    ones = (head_of.T == head_of).astype(BF16)
    tok = lambda b, c: (b * nblk + c, 0)
    const = lambda b, c: (0, 0)
    return pl.pallas_call(
        functools.partial(_gdn_kernel, heads),
        grid=(bsz, nblk),
        in_specs=[pl.BlockSpec((t, cdim), tok),
                  pl.BlockSpec((t, width), tok),
                  pl.BlockSpec((t, LANES), tok),
                  pl.BlockSpec((CONV_K, cdim), const),
                  pl.BlockSpec((1, LANES), const),
                  pl.BlockSpec((1, LANES), const),
                  pl.BlockSpec((1, width), const),
                  pl.BlockSpec((LANES, width), const),
                  pl.BlockSpec((LANES, width), const),
                  pl.BlockSpec((width, width), const)],
        out_specs=pl.BlockSpec((t, width), tok),
        out_shape=jax.ShapeDtypeStruct((n, width), F32),
        scratch_shapes=[pltpu.VMEM((SUBLANES + t, cdim), F32),
                        pltpu.VMEM((heads, GDN_HEADDIM, GDN_HEADDIM), F32)],
        compiler_params=_params("parallel", "arbitrary"),
        name="gdn",
    )(qkv, z, small, conv_w, pad(dt_bias, 2 * heads), pad(-jnp.exp(a_log), 2 * heads),
      jnp.tile(norm_w, heads).reshape(1, width), exb, exg, ones)


def _outproj_kernel(x_ref, s5_ref, ssd_ref, gdn_ref, wglu_ref, s5n_ref, wout_ref, o_ref):
    y = _gelu_tanh(s5_ref[...])
    y = y * _sigmoid(jnp.dot(y.astype(BF16), wglu_ref[...], preferred_element_type=F32))
    y = _rms(y, s5n_ref[...])
    mix = jnp.concatenate([y, ssd_ref[...], gdn_ref[...]], axis=1)
    o_ref[...] = x_ref[...] + jnp.dot(mix.astype(BF16), wout_ref[...], preferred_element_type=F32)


def _outproj(x2, y_s5, y_ssd, y_gdn, w_glu, s5_norm, w_out):
    n, d = x2.shape
    tm = min(ROW_BLOCK, n)
    ws = [y_s5.shape[1], y_ssd.shape[1], y_gdn.shape[1]]
    row = lambda i: (i, 0)
    const = lambda i: (0, 0)
    return pl.pallas_call(
        _outproj_kernel,
        grid=(n // tm,),
        in_specs=[pl.BlockSpec((tm, d), row)] + [pl.BlockSpec((tm, w), row) for w in ws]
                 + [pl.BlockSpec((ws[0], ws[0]), const), pl.BlockSpec((1, ws[0]), const),
                    pl.BlockSpec((sum(ws), d), const)],
        out_specs=pl.BlockSpec((tm, d), row),
        out_shape=jax.ShapeDtypeStruct((n, d), F32),
        compiler_params=_params("parallel"),
        name="outproj",
    )(x2, y_s5, y_ssd, y_gdn, w_glu.astype(BF16), s5_norm.reshape(1, -1), w_out.astype(BF16))


def _ffn_kernel(final, x_ref, g_ref, wg_ref, wu_ref, wd_ref, gf_ref, o_ref, h_ref, acc_ref):
    f = pl.program_id(1)

    @pl.when(f == 0)
    def _():
        h_ref[...] = _rms(x_ref[...], g_ref[...]).astype(BF16)
        acc_ref[...] = jnp.zeros(acc_ref.shape, F32)

    h = h_ref[...]
    a = jnp.dot(h, wg_ref[...], preferred_element_type=F32)
    u = jnp.dot(h, wu_ref[...], preferred_element_type=F32)
    acc_ref[...] += jnp.dot((_silu(a) * u).astype(BF16), wd_ref[...], preferred_element_type=F32)

    @pl.when(f == pl.num_programs(1) - 1)
    def _():
        out = x_ref[...] + acc_ref[...]
        o_ref[...] = _rms(out, gf_ref[...]) if final else out


def _ffn(x2, gain, w_gate, w_up, w_down, final_gain):
    n, d = x2.shape
    ff = w_gate.shape[1]
    tm = min(FFN_ROW_BLOCK, n)
    tf = min(FFN_COL_BLOCK, ff)
    final = final_gain is not None
    gf = (final_gain if final else gain).reshape(1, d)
    return pl.pallas_call(
        functools.partial(_ffn_kernel, final),
        grid=(n // tm, ff // tf),
        in_specs=[pl.BlockSpec((tm, d), lambda i, f: (i, 0)),
                  pl.BlockSpec((1, d), lambda i, f: (0, 0)),
                  pl.BlockSpec((d, tf), lambda i, f: (0, f)),
                  pl.BlockSpec((d, tf), lambda i, f: (0, f)),
                  pl.BlockSpec((tf, d), lambda i, f: (f, 0)),
                  pl.BlockSpec((1, d), lambda i, f: (0, 0))],
        out_specs=pl.BlockSpec((tm, d), lambda i, f: (i, 0)),
        out_shape=jax.ShapeDtypeStruct((n, d), F32),
        scratch_shapes=[pltpu.VMEM((tm, d), BF16), pltpu.VMEM((tm, d), F32)],
        compiler_params=_params("parallel", "arbitrary"),
        name="ffn",
    )(x2, gain.reshape(1, d), w_gate.astype(BF16), w_up.astype(BF16), w_down.astype(BF16), gf)


_R_E1, _R_E2, _R_W1, _R_W2, _R_RANK1, _R_RANK2 = range(6)


def _router_kernel(n_exp, x_ref, g_ref, wr_ref, tri_ref, hn_ref, rt_ref, cnt_ref, base_ref):
    @pl.when(pl.program_id(0) == 0)
    def _():
        base_ref[...] = jnp.zeros(base_ref.shape, F32)

    hn = _rms(x_ref[...], g_ref[...])
    hn_ref[...] = hn
    logits = jnp.dot(hn, wr_ref[...], precision=lax.Precision.HIGHEST, preferred_element_type=F32)
    lane = _iota2(logits.shape, 1)
    neg = jnp.float32(-jnp.inf)
    logits = jnp.where(lane < n_exp, logits, neg)
    m1 = jnp.max(logits, axis=-1, keepdims=True)
    i1 = jnp.min(jnp.where(logits == m1, lane, LANES), axis=-1, keepdims=True)
    rest = jnp.where(lane == i1, neg, logits)
    m2 = jnp.max(rest, axis=-1, keepdims=True)
    i2 = jnp.min(jnp.where(rest == m2, lane, LANES), axis=-1, keepdims=True)
    w1 = 1.0 / (1.0 + jnp.exp(m2 - m1))
    w2 = 1.0 / (1.0 + jnp.exp(m1 - m2))
    hit = (lane == i1) | (lane == i2)
    onehot = jnp.where(hit, 1.0, 0.0)
    before = jnp.dot(tri_ref[...], onehot.astype(BF16), preferred_element_type=F32) + base_ref[...]
    r1 = jnp.sum(jnp.where(lane == i1, before, 0.0), axis=-1, keepdims=True)
    r2 = jnp.sum(jnp.where(lane == i2, before, 0.0), axis=-1, keepdims=True)
    cols = {_R_E1: i1.astype(F32), _R_E2: i2.astype(F32), _R_W1: w1, _R_W2: w2, _R_RANK1: r1, _R_RANK2: r2}
    rt = jnp.zeros(logits.shape, F32)
    for k, v in cols.items():
        rt = jnp.where(lane == k, v, rt)
    rt_ref[...] = rt
    base_ref[...] += jnp.sum(onehot, axis=0, keepdims=True)
    cnt_ref[...] = base_ref[...]


def _row_copy(src_ref, src_row, dst_ref, dst_row, sem):
    return pltpu.make_async_copy(src_ref.at[pl.ds(src_row, 1)], dst_ref.at[pl.ds(dst_row, 1)], sem)


def _dispatch_kernel(n_exp, rb, s1_ref, s2_ref, fill_ref, hn_ref, xs_ref, zero_ref, sem, zsem):
    tm = hn_ref.shape[0]
    base = pl.program_id(0) * tm

    @pl.when(pl.program_id(0) == 0)
    def _():
        zero_ref[...] = jnp.zeros(zero_ref.shape, F32)
        for e in range(n_exp):
            lo, hi = fill_ref[e], fill_ref[n_exp + e]

            def start_row(r, carry):
                _row_copy(zero_ref, 0, xs_ref, r, zsem).start()
                return carry

            def wait_row(r, carry):
                _row_copy(zero_ref, 0, xs_ref, r, zsem).wait()
                return carry

            lax.fori_loop(lo, hi, start_row, 0)
            lax.fori_loop(lo, hi, wait_row, 0)

        def block_copy(b):
            return pltpu.make_async_copy(zero_ref, xs_ref.at[pl.ds(pl.multiple_of(b * rb, rb), rb)], zsem)

        def start_block(b, carry):
            block_copy(b).start()
            return carry

        def wait_block(b, carry):
            block_copy(b).wait()
            return carry

        n_blocks = xs_ref.shape[0] // rb
        lax.fori_loop(fill_ref[2 * n_exp], n_blocks, start_block, 0)
        lax.fori_loop(fill_ref[2 * n_exp], n_blocks, wait_block, 0)

    def issue(r, carry):
        _row_copy(hn_ref, r, xs_ref, s1_ref[base + r], sem).start()
        _row_copy(hn_ref, r, xs_ref, s2_ref[base + r], sem).start()
        return carry

    lax.fori_loop(0, tm, issue, 0)
    for _ in range(TOP_K):
        pltpu.make_async_copy(hn_ref, xs_ref.at[pl.ds(0, tm)], sem).wait()


def _expert_ffn_kernel(be_ref, bv_ref, xs_ref, wg_ref, wu_ref, wd_ref, ys_ref, h_ref, acc_ref):
    i = pl.program_id(0)
    f = pl.program_id(1)
    last = pl.num_programs(1) - 1
    valid = bv_ref[i]

    @pl.when(valid > 0)
    def _():
        @pl.when(f == 0)
        def _():
            row = _iota2((xs_ref.shape[0], 1), 0)
            h_ref[...] = jnp.where(row < valid, xs_ref[...], 0.0).astype(BF16)
            acc_ref[...] = jnp.zeros(acc_ref.shape, F32)

        h = h_ref[...]
        a = jnp.dot(h, wg_ref[0], preferred_element_type=F32)
        u = jnp.dot(h, wu_ref[0], preferred_element_type=F32)
        acc_ref[...] += jnp.dot((_silu(a) * u).astype(BF16), wd_ref[0], preferred_element_type=F32)

        @pl.when(f == last)
        def _():
            ys_ref[...] = acc_ref[...]

    @pl.when((valid == 0) & (f == last))
    def _():
        ys_ref[...] = jnp.zeros(ys_ref.shape, F32)


def _combine_kernel(final, s1_ref, s2_ref, x_ref, rt_ref, gf_ref, ys_ref, o_ref, b1_ref, b2_ref, sem):
    tm = x_ref.shape[0]
    base = pl.program_id(0) * tm

    def issue(r, carry):
        _row_copy(ys_ref, s1_ref[base + r], b1_ref, r, sem).start()
        _row_copy(ys_ref, s2_ref[base + r], b2_ref, r, sem).start()
        return carry

    lax.fori_loop(0, tm, issue, 0)
    for buf in (b1_ref, b2_ref):
        pltpu.make_async_copy(ys_ref.at[pl.ds(0, tm)], buf, sem).wait()
    rt = rt_ref[...]
    out = (x_ref[...] + rt[:, _R_W1:_R_W1 + 1] * b1_ref[...] + rt[:, _R_W2:_R_W2 + 1] * b2_ref[...])
    o_ref[...] = _rms(out, gf_ref[...]) if final else out


def _moe(x2, gain, w_router, w_gate, w_up, w_down, final_gain):
    n, d = x2.shape
    n_exp, _, ff = w_gate.shape
    final = final_gain is not None
    gf = (final_gain if final else gain).reshape(1, d)
    wr = jnp.zeros((d, LANES), F32).at[:, :n_exp].set(w_router)

    tr = min(ROUTER_BLOCK, n)
    tri = (jnp.arange(tr)[:, None] > jnp.arange(tr)[None, :]).astype(BF16)
    hn, route, counts = pl.pallas_call(
        functools.partial(_router_kernel, n_exp),
        grid=(n // tr,),
        in_specs=[pl.BlockSpec((tr, d), lambda i: (i, 0)),
                  pl.BlockSpec((1, d), lambda i: (0, 0)),
                  pl.BlockSpec((d, LANES), lambda i: (0, 0)),
                  pl.BlockSpec((tr, tr), lambda i: (0, 0))],
        out_specs=[pl.BlockSpec((tr, d), lambda i: (i, 0)),
                   pl.BlockSpec((tr, LANES), lambda i: (i, 0)),
                   pl.BlockSpec((1, LANES), lambda i: (0, 0))],
        out_shape=[jax.ShapeDtypeStruct((n, d), F32),
                   jax.ShapeDtypeStruct((n, LANES), F32),
                   jax.ShapeDtypeStruct((1, LANES), F32)],
        scratch_shapes=[pltpu.VMEM((1, LANES), F32)],
        compiler_params=_params("arbitrary"),
        name="moe_router",
    )(x2, gain.reshape(1, d), wr, tri)

    rb = min(MOE_ROW_BLOCK, TOP_K * n)
    n_slots = TOP_K * n + n_exp * rb
    n_blocks = n_slots // rb
    cnt = counts[0, :n_exp].astype(jnp.int32)
    padded = ((cnt + rb - 1) // rb) * rb
    ends = jnp.cumsum(padded)
    starts = ends - padded
    as_int = lambda k: route[:, k].astype(jnp.int32)
    slot1 = starts[as_int(_R_E1)] + as_int(_R_RANK1)
    slot2 = starts[as_int(_R_E2)] + as_int(_R_RANK2)
    block_row = jnp.arange(n_blocks, dtype=jnp.int32) * rb
    block_exp = jnp.minimum(jnp.sum(block_row[:, None] >= ends[None, :], axis=1), n_exp - 1).astype(jnp.int32)
    block_valid = jnp.clip(cnt[block_exp] - (block_row - starts[block_exp]), 0, rb).astype(jnp.int32)

    tdp = min(DISPATCH_BLOCK, n)
    fill = jnp.concatenate([starts + cnt, ends, ends[-1:] // rb]).astype(jnp.int32)
    xs = pl.pallas_call(
        functools.partial(_dispatch_kernel, n_exp, rb),
        grid_spec=pltpu.PrefetchScalarGridSpec(
            num_scalar_prefetch=3,
            grid=(n // tdp,),
            in_specs=[pl.BlockSpec((tdp, d), lambda i, s1, s2, fl: (i, 0))],
            out_specs=pl.BlockSpec(memory_space=pl.ANY),
            scratch_shapes=[pltpu.VMEM((rb, d), F32), pltpu.SemaphoreType.DMA, pltpu.SemaphoreType.DMA]),
        out_shape=jax.ShapeDtypeStruct((n_slots, d), F32),
        compiler_params=_params("arbitrary"),
        name="moe_dispatch",
    )(slot1, slot2, fill, hn)

    tf = min(FFN_COL_BLOCK, ff)
    ys = pl.pallas_call(
        _expert_ffn_kernel,
        grid_spec=pltpu.PrefetchScalarGridSpec(
            num_scalar_prefetch=2,
            grid=(n_blocks, ff // tf),
            in_specs=[pl.BlockSpec((rb, d), lambda i, f, be, bv: (i, 0)),
                      pl.BlockSpec((1, d, tf), lambda i, f, be, bv: (be[i], 0, f)),
                      pl.BlockSpec((1, d, tf), lambda i, f, be, bv: (be[i], 0, f)),
                      pl.BlockSpec((1, tf, d), lambda i, f, be, bv: (be[i], f, 0))],
            out_specs=pl.BlockSpec((rb, d), lambda i, f, be, bv: (i, 0)),
            scratch_shapes=[pltpu.VMEM((rb, d), BF16), pltpu.VMEM((rb, d), F32)]),
        out_shape=jax.ShapeDtypeStruct((n_slots, d), F32),
        compiler_params=_params("parallel", "arbitrary"),
        name="moe_experts",
    )(block_exp, block_valid, xs, w_gate.astype(BF16), w_up.astype(BF16), w_down.astype(BF16))

    tc = min(COMBINE_BLOCK, n)
    return pl.pallas_call(
        functools.partial(_combine_kernel, final),
        grid_spec=pltpu.PrefetchScalarGridSpec(
            num_scalar_prefetch=2,
            grid=(n // tc,),
            in_specs=[pl.BlockSpec((tc, d), lambda i, s1, s2: (i, 0)),
                      pl.BlockSpec((tc, LANES), lambda i, s1, s2: (i, 0)),
                      pl.BlockSpec((1, d), lambda i, s1, s2: (0, 0)),
                      pl.BlockSpec(memory_space=pl.ANY)],
            out_specs=pl.BlockSpec((tc, d), lambda i, s1, s2: (i, 0)),
            scratch_shapes=[pltpu.VMEM((tc, d), F32), pltpu.VMEM((tc, d), F32), pltpu.SemaphoreType.DMA]),
        out_shape=jax.ShapeDtypeStruct((n, d), F32),
        compiler_params=_params("arbitrary"),
        name="moe_combine",
    )(slot1, slot2, x2, route, gf, ys)


def _arrange_in_proj(w_in, s5_w, ssd_w, ssd_cdim, ssd_h, gdn_cdim, gdn_w, gdn_h):
    sizes = (s5_w, ssd_w, ssd_cdim, ssd_h, gdn_cdim, gdn_w, gdn_h, gdn_h)
    offs = np.cumsum((0,) + sizes)
    seg = lambda i: w_in[:, offs[i]:offs[i + 1]]
    small = jnp.concatenate([seg(3), seg(6), seg(7)], axis=1)
    small = jnp.pad(small, ((0, 0), (0, LANES - small.shape[1])))
    w = jnp.concatenate([seg(0), seg(1), seg(2), seg(4), seg(5), small], axis=1)
    return w.astype(BF16), (s5_w, ssd_w, ssd_cdim, gdn_cdim, gdn_w, LANES)


def kernel(x, norm_mix, w_in, w_out, s5_a_re, s5_a_im, s5_b_re, s5_b_im, s5_c_re, s5_c_im, s5_d, s5_log_step, s5_w_glu, s5_norm, ssd_conv_w, ssd_conv_b, ssd_dt_bias, ssd_a_log, ssd_d, ssd_norm, gdn_conv_w, gdn_a_log, gdn_dt_bias, gdn_norm, norm_ffn, ff_w_gate, ff_w_up, ff_w_down, moe_router, moe_w_gate, moe_w_up, moe_w_down, norm_final):
    bsz, seq, d = x.shape
    depth = norm_mix.shape[0]
    s5_w = s5_w_glu.shape[-1]
    ssd_w, ssd_h, ssd_cdim = ssd_norm.shape[-1], ssd_a_log.shape[-1], ssd_conv_w.shape[-1]
    gdn_h, gdn_cdim = gdn_a_log.shape[-1], gdn_conv_w.shape[-1]
    gdn_w = gdn_h * GDN_HEADDIM
    q = min(S5_CHUNK, seq)
    xr = x.reshape(bsz * seq, d)
    for layer in range(depth):
        w_arr, sizes = _arrange_in_proj(w_in[layer], s5_w, ssd_w, ssd_cdim, ssd_h, gdn_cdim, gdn_w, gdn_h)
        u_s5, z_ssd, xbc, qkv, z_gdn, small = _inproj(xr, norm_mix[layer], w_arr, sizes)
        tables = _s5_tables(s5_a_re[layer], s5_a_im[layer], s5_b_re[layer], s5_b_im[layer], s5_c_re[layer],
                            s5_c_im[layer], s5_d[layer], s5_log_step[layer], q, seq // q)
        y_s5 = _s5_scan(u_s5, tables, bsz, seq)
        y_ssd = _ssd(z_ssd, xbc, small, ssd_conv_w[layer], ssd_conv_b[layer], ssd_dt_bias[layer],
                     ssd_a_log[layer], ssd_d[layer], ssd_norm[layer], bsz, seq)
        y_gdn = _gdn(qkv, z_gdn, small, gdn_conv_w[layer], gdn_a_log[layer], gdn_dt_bias[layer],
                     gdn_norm[layer], bsz, seq)
        xr = _outproj(xr, y_s5, y_ssd, y_gdn, s5_w_glu[layer], s5_norm[layer], w_out[layer])
        final_gain = norm_final if layer == depth - 1 else None
        i = layer // 2
        if layer % 2 == 0:
            xr = _ffn(xr, norm_ffn[layer], ff_w_gate[i], ff_w_up[i], ff_w_down[i], final_gain)
        else:
            xr = _moe(xr, norm_ffn[layer], moe_router[i], moe_w_gate[i], moe_w_up[i], moe_w_down[i], final_gain)
    return xr.reshape(bsz, seq, d)
```

```python
import functools
import math

import jax
import jax.numpy as jnp
import numpy as np
from jax import lax
from jax.experimental import pallas as pl
from jax.experimental.pallas import tpu as pltpu

F32 = jnp.float32
BF16 = jnp.bfloat16
EPS = 1e-6

S5_GROUP = 16
S5_STATE = 64
SSD_HEADDIM = 64
SSD_NGROUPS = 2
SSD_STATE = 128
GDN_HEADDIM = 64
CONV_K = 4
TOP_K = 2

LANES = 128
SUBLANES = 8
VMEM_LIMIT_BYTES = 56 * 1024 * 1024

SSD_CHUNK = 128
GDN_CHUNK = 64
GDN_INV_BLOCK = 16
GDN_GROUP = 256
SEQ_BLOCK = 512
ROW_BLOCK = 512
FFN_ROW_BLOCK = 1024
FFN_COL_BLOCK = 512
ROUTER_BLOCK = 1024
DISPATCH_BLOCK = 1024
MOE_ROW_BLOCK = 1024
COMBINE_BLOCK = 512
DMA_ISSUE_UNROLL = 8


def _params(*semantics):
    return pltpu.CompilerParams(dimension_semantics=semantics, vmem_limit_bytes=VMEM_LIMIT_BYTES)


def _dot(a, b):
    return jnp.dot(a.astype(BF16), b.astype(BF16), preferred_element_type=F32)


def _dot_nt(a, b):
    return lax.dot_general(a.astype(BF16), b.astype(BF16), (((1,), (1,)), ((), ())),
                           preferred_element_type=F32)


def _dot_tn(a, b):
    return lax.dot_general(a.astype(BF16), b.astype(BF16), (((0,), (0,)), ((), ())),
                           preferred_element_type=F32)


def _split3(a):
    hi = a.astype(BF16)
    r1 = a - hi.astype(F32)
    mid = r1.astype(BF16)
    lo = (r1 - mid.astype(F32)).astype(BF16)
    return hi, mid, lo


def _dot_sel(a, sel):
    hi, mid, lo = _split3(a)
    out = jnp.dot(hi, sel, preferred_element_type=F32)
    out += jnp.dot(mid, sel, preferred_element_type=F32)
    out += jnp.dot(lo, sel, preferred_element_type=F32)
    return out


def _sel_dot(sel, a):
    hi, mid, lo = _split3(a)
    out = jnp.dot(sel, hi, preferred_element_type=F32)
    out += jnp.dot(sel, mid, preferred_element_type=F32)
    out += jnp.dot(sel, lo, preferred_element_type=F32)
    return out


def _sigmoid(x):
    return 1.0 / (1.0 + jnp.exp(-x))


def _silu(x):
    return x * _sigmoid(x)


def _softplus(x):
    return jnp.maximum(x, 0.0) + jnp.log(1.0 + jnp.exp(-jnp.abs(x)))


def _gelu_tanh(x):
    c = math.sqrt(2.0 / math.pi)
    return 0.5 * x * (1.0 + jnp.tanh(c * (x + 0.044715 * (x * x * x))))


def _rms(x, w):
    return x * lax.rsqrt(jnp.mean(x * x, axis=-1, keepdims=True) + EPS) * w


def _iota2(shape, axis):
    return lax.broadcasted_iota(jnp.int32, shape, axis)


def _inproj_kernel(sizes, x_ref, g_ref, w_ref, *out_refs):
    h = _rms(x_ref[...], g_ref[...])
    p = jnp.dot(h.astype(BF16), w_ref[...], preferred_element_type=F32)
    off = 0
    for o_ref, sz in zip(out_refs, sizes):
        o_ref[...] = p[:, off:off + sz]
        off += sz


def _inproj(x2, gain, w_bf16, sizes):
    n, d = x2.shape
    tm = min(ROW_BLOCK, n)
    total = sum(sizes)
    return pl.pallas_call(
        functools.partial(_inproj_kernel, sizes),
        grid=(n // tm,),
        in_specs=[pl.BlockSpec((tm, d), lambda i: (i, 0)),
                  pl.BlockSpec((1, d), lambda i: (0, 0)),
                  pl.BlockSpec((d, total), lambda i: (0, 0))],
        out_specs=[pl.BlockSpec((tm, sz), lambda i: (i, 0)) for sz in sizes],
        out_shape=[jax.ShapeDtypeStruct((n, sz), F32) for sz in sizes],
        compiler_params=_params("parallel"),
        name="inproj",
    )(x2, gain.reshape(1, d), w_bf16)


def _s5_operands(a_re, a_im, b_re, b_im, c_re, c_im, d_skip, log_step):
    g, p = a_re.shape
    h = b_re.shape[-1]
    step = jnp.exp(log_step)[:, None]
    mag = jnp.exp(a_re * step)
    ang = a_im * step
    lb_re, lb_im = mag * jnp.cos(ang), mag * jnp.sin(ang)
    den = a_re * a_re + a_im * a_im
    f_re = ((lb_re - 1.0) * a_re + lb_im * a_im) / den
    f_im = (lb_im * a_re - (lb_re - 1.0) * a_im) / den
    bb_re = f_re[..., None] * b_re - f_im[..., None] * b_im
    bb_im = f_re[..., None] * b_im + f_im[..., None] * b_re

    eye = jnp.eye(g, dtype=F32)
    to_state = lambda m: (m.transpose(0, 2, 1)[:, :, None, :] * eye[:, None, :, None]).reshape(g * h, g * p)
    to_out = lambda m: (m.transpose(0, 2, 1)[:, :, None, :] * eye[:, None, :, None]).reshape(g * p, g * h)
    bbd = jnp.concatenate([to_state(bb_re), to_state(bb_im)], axis=1)
    cbd = jnp.concatenate([to_out(c_re), -to_out(c_im)], axis=0)
    e = jnp.arange(1, SUBLANES + 1, dtype=F32)[:, None, None]
    m = jnp.exp(a_re * step * e)
    lam = jnp.concatenate([(m * jnp.cos(ang * e)).reshape(SUBLANES, g * p),
                           (m * jnp.sin(ang * e)).reshape(SUBLANES, g * p)], axis=1)
    return bbd, cbd, lam, d_skip.reshape(1, g * h)


def _s5_kernel(u_ref, bbd_ref, cbd_ref, lam_ref, d_ref, y_ref, s_ref, carry_ref):
    t, s2 = s_ref.shape
    half = s2 // 2
    tile = SUBLANES

    @pl.when(pl.program_id(1) == 0)
    def _():
        carry_ref[...] = jnp.zeros(carry_ref.shape, F32)

    u = u_ref[...]
    bu = jnp.dot(u.astype(BF16), bbd_ref[...], preferred_element_type=F32)
    re = bu[:, :half].reshape(t // tile, tile, half)
    im = bu[:, half:].reshape(t // tile, tile, half)
    row = _iota2((1, tile, half), 1)
    sh = 1
    while sh < tile:
        ar = lam_ref[sh - 1:sh, :half]
        ai = lam_ref[sh - 1:sh, half:]
        keep = row >= sh
        pr = jnp.where(keep, pltpu.roll(re, sh, axis=1), 0.0)
        pi = jnp.where(keep, pltpu.roll(im, sh, axis=1), 0.0)
        re, im = re + ar * pr - ai * pi, im + ar * pi + ai * pr
        sh *= 2
    s_ref[:, :half] = re.reshape(t, half)
    s_ref[:, half:] = im.reshape(t, half)

    lr = lam_ref[:, :half]
    li = lam_ref[:, half:]

    def tile_body(k, carry):
        cr, ci = carry
        r0 = pl.multiple_of(k * tile, tile)
        tr = s_ref[pl.ds(r0, tile), :half] + lr * cr - li * ci
        ti = s_ref[pl.ds(r0, tile), half:] + lr * ci + li * cr
        s_ref[pl.ds(r0, tile), :half] = tr
        s_ref[pl.ds(r0, tile), half:] = ti
        return (jnp.broadcast_to(tr[tile - 1:tile], tr.shape), jnp.broadcast_to(ti[tile - 1:tile], ti.shape))

    cr, ci = lax.fori_loop(0, t // tile, tile_body, (carry_ref[:, :half], carry_ref[:, half:]))
    carry_ref[:, :half] = cr
    carry_ref[:, half:] = ci
    y_ref[...] = (jnp.dot(s_ref[...].astype(BF16), cbd_ref[...], preferred_element_type=F32)
                  + d_ref[...] * u)


def _s5_scan(u, operands, bsz, seq):
    bbd, cbd, lam, d = operands
    n, width = u.shape
    s2 = bbd.shape[1]
    t = min(SEQ_BLOCK, seq)
    nblk = seq // t
    tok = lambda b, c: (b * nblk + c, 0)
    const = lambda b, c: (0, 0)
    return pl.pallas_call(
        _s5_kernel,
        grid=(bsz, nblk),
        in_specs=[pl.BlockSpec((t, width), tok),
                  pl.BlockSpec((width, s2), const),
                  pl.BlockSpec((s2, width), const),
                  pl.BlockSpec((SUBLANES, s2), const),
                  pl.BlockSpec((1, width), const)],
        out_specs=pl.BlockSpec((t, width), tok),
        out_shape=jax.ShapeDtypeStruct((n, width), F32),
        scratch_shapes=[pltpu.VMEM((t, s2), F32), pltpu.VMEM((SUBLANES, s2), F32)],
        compiler_params=_params("parallel", "arbitrary"),
        name="s5_scan",
    )(u, bbd.astype(BF16), cbd.astype(BF16), lam, d)


def _causal_conv_block(x_ref, w_ref, pad_ref, first):
    t = x_ref.shape[0]
    halo = SUBLANES

    @pl.when(first)
    def _():
        pad_ref[0:halo, :] = jnp.zeros((halo, pad_ref.shape[1]), F32)

    @pl.when(jnp.logical_not(first))
    def _():
        pad_ref[0:halo, :] = pad_ref[t:t + halo, :]

    pad_ref[halo:halo + t, :] = x_ref[...]
    acc = w_ref[CONV_K - 1:CONV_K, :] * pad_ref[halo:halo + t, :]
    for j in range(1, CONV_K):
        acc += w_ref[CONV_K - 1 - j:CONV_K - j, :] * pad_ref[halo - j:halo - j + t, :]
    return acc


def _ssd_kernel(heads, z_ref, xbc_ref, sm_ref, cw_ref, cb_ref, dtb_ref, a_ref, dexp_ref, nw_ref, ex_ref,
                o_ref, pad_ref, xc_ref, dt_ref, st_ref):
    t = xbc_ref.shape[0]
    q = min(SSD_CHUNK, t)
    width = heads * SSD_HEADDIM
    gw = SSD_NGROUPS * SSD_STATE
    first = pl.program_id(1) == 0

    @pl.when(first)
    def _():
        st_ref[...] = jnp.zeros(st_ref.shape, F32)

    xc_ref[...] = _silu(_causal_conv_block(xbc_ref, cw_ref, pad_ref, first) + cb_ref[...])
    dt_ref[...] = _softplus(sm_ref[...] + dtb_ref[...])

    row = _iota2((q, q), 0)
    col = _iota2((q, q), 1)
    causal = row >= col
    tril = jnp.where(causal, 1.0, 0.0).astype(BF16)
    lane = _iota2((1, width), 1)
    rep = heads // SSD_NGROUPS

    def chunk_body(ci, carry):
        r0 = pl.multiple_of(ci * q, q)
        xs = xc_ref[pl.ds(r0, q), 0:width]
        bm = xc_ref[pl.ds(r0, q), width:width + gw]
        cm = xc_ref[pl.ds(r0, q), width + gw:width + 2 * gw]
        dt = dt_ref[pl.ds(r0, q), :]
        acs = _sel_dot(tril, dt * a_ref[...])
        acs_t = acs.T
        xdt = xs * _dot_sel(dt, ex_ref[...])
        hs = range(heads)
        bgs = [bm[:, gi * SSD_STATE:(gi + 1) * SSD_STATE] for gi in range(SSD_NGROUPS)]
        cgs = [cm[:, gi * SSD_STATE:(gi + 1) * SSD_STATE].astype(BF16) for gi in range(SSD_NGROUPS)]
        cbs = [_dot_nt(cgs[gi], bgs[gi]) for gi in range(SSD_NGROUPS)]
        xdb = xdt.astype(BF16)
        xh = [xdb[:, hh * SSD_HEADDIM:(hh + 1) * SSD_HEADDIM] for hh in hs]
        acol = [acs[:, hh:hh + 1] for hh in hs]
        alast = [acs[q - 1:q, hh:hh + 1] for hh in hs]
        decay = [jnp.where(causal, jnp.exp(jnp.where(causal, acol[hh] - acs_t[hh:hh + 1, :], 0.0)), 0.0)
                 for hh in hs]
        state = [st_ref[hh] for hh in hs]
        y_diag = [_mm((cbs[hh // rep] * decay[hh]).astype(BF16), xh[hh]) for hh in hs]
        y_off = [_mm(cgs[hh // rep], state[hh].astype(BF16)) * jnp.exp(acol[hh]) for hh in hs]
        upd = [_dot_tn(bgs[hh // rep] * jnp.exp(alast[hh] - acol[hh]), xh[hh]) for hh in hs]
        for hh in hs:
            st_ref[hh] = state[hh] * jnp.exp(alast[hh]) + upd[hh]
        y = jnp.concatenate([y_diag[hh] + y_off[hh] for hh in hs], axis=1) + dexp_ref[...] * xs
        y = y * _silu(z_ref[pl.ds(r0, q), :])
        gsz = width // SSD_NGROUPS
        y2 = y * y
        out = jnp.zeros_like(y)
        for gi in range(SSD_NGROUPS):
            m = (lane >= gi * gsz) & (lane < (gi + 1) * gsz)
            ms = jnp.sum(jnp.where(m, y2, 0.0), axis=-1, keepdims=True) / gsz
            out = jnp.where(m, y * lax.rsqrt(ms + EPS), out)
        o_ref[pl.ds(r0, q), :] = out * nw_ref[...]
        return carry

    lax.fori_loop(0, t // q, chunk_body, 0)


def _ssd(z, xbc, small, conv_w, conv_b, dt_bias, a_log, d_skip, norm_w, bsz, seq):
    n, width = z.shape
    heads = a_log.shape[0]
    cdim = xbc.shape[1]
    t = min(SEQ_BLOCK, seq)
    nblk = seq // t
    pad = lambda v: jnp.zeros((1, LANES), F32).at[0, :heads].set(v)
    expand = (jnp.arange(LANES)[:, None] == (jnp.arange(width)[None, :] // SSD_HEADDIM)).astype(BF16)
    tok = lambda b, c: (b * nblk + c, 0)
    const = lambda b, c: (0, 0)
    return pl.pallas_call(
        functools.partial(_ssd_kernel, heads),
        grid=(bsz, nblk),
        in_specs=[pl.BlockSpec((t, width), tok),
                  pl.BlockSpec((t, cdim), tok),
                  pl.BlockSpec((t, LANES), tok),
                  pl.BlockSpec((CONV_K, cdim), const),
                  pl.BlockSpec((1, cdim), const),
                  pl.BlockSpec((1, LANES), const),
                  pl.BlockSpec((1, LANES), const),
                  pl.BlockSpec((1, width), const),
                  pl.BlockSpec((1, width), const),
                  pl.BlockSpec((LANES, width), const)],
        out_specs=pl.BlockSpec((t, width), tok),
        out_shape=jax.ShapeDtypeStruct((n, width), F32),
        scratch_shapes=[pltpu.VMEM((SUBLANES + t, cdim), F32),
                        pltpu.VMEM((t, cdim), F32),
                        pltpu.VMEM((t, LANES), F32),
                        pltpu.VMEM((heads, SSD_STATE, SSD_HEADDIM), F32)],
        compiler_params=_params("parallel", "arbitrary"),
        name="ssd",
    )(z, xbc, small, conv_w, conv_b.reshape(1, cdim), pad(dt_bias), pad(-jnp.exp(a_log)),
      jnp.repeat(d_skip, SSD_HEADDIM).reshape(1, width), norm_w.reshape(1, width), expand)


def _mm(a, b):
    return jnp.dot(a, b, preferred_element_type=F32)


def _unit_lower_inverses(lms, eye, blk, chunk):
    lds = [jnp.where(blk, lm, 0.0) for lm in lms]
    offs = [(lm - ld).astype(BF16) for lm, ld in zip(lms, lds)]
    dinv = [eye - ld for ld in lds]
    pw = [ld.astype(BF16) for ld in lds]
    span = 2
    while span < GDN_INV_BLOCK:
        sq = [_mm(p, p) for p in pw]
        dinv = [_mm(d.astype(BF16), (eye + s).astype(BF16)) for d, s in zip(dinv, sq)]
        pw = [s.astype(BF16) for s in sq]
        span *= 2
    dinv = [d.astype(BF16) for d in dinv]
    ms = [_mm(d, o) for d, o in zip(dinv, offs)]
    out = [eye - m for m in ms]
    pw = [m.astype(BF16) for m in ms]
    span = 2
    while span < chunk // GDN_INV_BLOCK:
        sq = [_mm(p, p) for p in pw]
        out = [_mm(o.astype(BF16), (eye + s).astype(BF16)) for o, s in zip(out, sq)]
        pw = [s.astype(BF16) for s in sq]
        span *= 2
    return [_mm(o.astype(BF16), d) for o, d in zip(out, dinv)]


def _gdn_kernel(heads, qkv_ref, z_ref, sm_ref, cw_ref, dtb_ref, a_ref, nw_ref, exb_ref, exg_ref, ones_ref,
                o_ref, pad_ref, st_ref):
    t = qkv_ref.shape[0]
    c = min(GDN_CHUNK, t)
    hd = GDN_HEADDIM
    width = heads * hd
    hs = range(heads)
    sl = [slice(hh * hd, (hh + 1) * hd) for hh in hs]
    first = pl.program_id(1) == 0

    @pl.when(first)
    def _():
        st_ref[...] = jnp.zeros(st_ref.shape, F32)

    def seg_sum(v):
        hi = v.astype(BF16)
        lo = (v - hi.astype(F32)).astype(BF16)
        return _mm(hi, ones_ref[...]) + _mm(lo, ones_ref[...])

    xc = _silu(_causal_conv_block(qkv_ref, cw_ref, pad_ref, first))
    qf = xc[:, 0:width]
    kf = xc[:, width:2 * width]
    vf = xc[:, 2 * width:3 * width]
    qf = qf * lax.rsqrt(seg_sum(qf * qf) + EPS) * (hd ** -0.5)
    kf = kf * lax.rsqrt(seg_sum(kf * kf) + EPS)
    sm = sm_ref[...]
    beta = _dot_sel(_sigmoid(sm), exb_ref[...])
    g = a_ref[...] * _softplus(sm + dtb_ref[...])

    row = _iota2((t, t), 0)
    col = _iota2((t, t), 1)
    same = (row // c) == (col // c)
    causal = same & (row >= col)
    strict = same & (row > col)
    eye = jnp.where(row == col, 1.0, 0.0)
    blk = (row // GDN_INV_BLOCK) == (col // GDN_INV_BLOCK)
    gcs = _sel_dot(jnp.where(causal, 1.0, 0.0).astype(BF16), g)
    gtot = _sel_dot(jnp.where(same, 1.0, 0.0).astype(BF16), g)
    gcs_t = gcs.T
    gw = _dot_sel(gcs, exg_ref[...])
    gtw = _dot_sel(gtot, exg_ref[...])
    gexp = jnp.exp(gw)
    kb = kf * beta
    vb = vf * beta
    kcd_in = kb * gexp
    qd = (qf * gexp).astype(BF16)
    kend = (kf * jnp.exp(gtw - gw)).astype(BF16)
    cdec = jnp.exp(gtw)
    kfb = kf.astype(BF16)
    kbb = kb.astype(BF16)
    qfb = qf.astype(BF16)
    nt = (((1,), (1,)), ((), ()))
    decay = []
    for hh in hs:
        lg = 2 * heads + hh
        diff = jnp.where(causal, gcs[:, lg:lg + 1] - gcs_t[lg:lg + 1, :], 0.0)
        decay.append(jnp.where(causal, jnp.exp(diff), 0.0))
    kk = [lax.dot_general(kbb[:, sl[hh]], kfb[:, sl[hh]], nt, preferred_element_type=F32) for hh in hs]
    qk = [lax.dot_general(qfb[:, sl[hh]], kfb[:, sl[hh]], nt, preferred_element_type=F32) for hh in hs]
    lms = [jnp.where(strict, kk[hh] * decay[hh], 0.0) for hh in hs]
    qk = [(qk[hh] * decay[hh]).astype(BF16) for hh in hs]
    tinv = _unit_lower_inverses(lms, eye, blk, c)
    rhs = [jnp.concatenate([vb[:, sl[hh]], kcd_in[:, sl[hh]]], axis=1).astype(BF16) for hh in hs]
    vk = [_mm(tinv[hh].astype(BF16), rhs[hh]) for hh in hs]

    state = [st_ref[hh] for hh in hs]
    outs = []
    for ci in range(t // c):
        rs = slice(ci * c, (ci + 1) * c)
        lhs = [jnp.concatenate([vk[hh][rs, hd:2 * hd].astype(BF16), qd[rs, sl[hh]]], axis=0) for hh in hs]
        sb = [s.astype(BF16) for s in state]
        both = [_mm(lhs[hh], sb[hh]) for hh in hs]
        v_new = [(vk[hh][rs, 0:hd] - both[hh][0:c]).astype(BF16) for hh in hs]
        o = [both[hh][c:2 * c] + _mm(qk[hh][rs, ci * c:(ci + 1) * c], v_new[hh]) for hh in hs]
        upd = [lax.dot_general(kend[rs, sl[hh]], v_new[hh], (((0,), (0,)), ((), ())),
                               preferred_element_type=F32) for hh in hs]
        state = [state[hh] * cdec[ci * c:ci * c + 1, sl[hh]] + upd[hh] for hh in hs]
        outs.append(jnp.concatenate(o, axis=1))
    for hh in hs:
        st_ref[hh] = state[hh]
    o = jnp.concatenate(outs, axis=0)
    o = o * lax.rsqrt(seg_sum(o * o) / hd + EPS) * nw_ref[...]
    o_ref[...] = o * _silu(z_ref[...])


def _gdn(qkv, z, small, conv_w, a_log, dt_bias, norm_w, bsz, seq):
    n, width = z.shape
    heads = a_log.shape[0]
    cdim = qkv.shape[1]
    t = min(GDN_GROUP, seq)
    nblk = seq // t
    pad = lambda v, off: jnp.zeros((1, LANES), F32).at[0, off:off + heads].set(v)
    head_of = jnp.arange(width)[None, :] // GDN_HEADDIM
    lanes = jnp.arange(LANES)[:, None]
    exb = (lanes == heads + head_of).astype(BF16)
    exg = (lanes == 2 * heads + head_of).astype(BF16)
    ones = (head_of.T == head_of).astype(BF16)
    tok = lambda b, c: (b * nblk + c, 0)
    const = lambda b, c: (0, 0)
    return pl.pallas_call(
        functools.partial(_gdn_kernel, heads),
        grid=(bsz, nblk),
        in_specs=[pl.BlockSpec((t, cdim), tok),
                  pl.BlockSpec((t, width), tok),
                  pl.BlockSpec((t, LANES), tok),
                  pl.BlockSpec((CONV_K, cdim), const),
                  pl.BlockSpec((1, LANES), const),
                  pl.BlockSpec((1, LANES), const),
                  pl.BlockSpec((1, width), const),
                  pl.BlockSpec((LANES, width), const),
                  pl.BlockSpec((LANES, width), const),
                  pl.BlockSpec((width, width), const)],
        out_specs=pl.BlockSpec((t, width), tok),
        out_shape=jax.ShapeDtypeStruct((n, width), F32),
        scratch_shapes=[pltpu.VMEM((SUBLANES + t, cdim), F32),
                        pltpu.VMEM((heads, GDN_HEADDIM, GDN_HEADDIM), F32)],
        compiler_params=_params("parallel", "arbitrary"),
        name="gdn",
    )(qkv, z, small, conv_w, pad(dt_bias, 2 * heads), pad(-jnp.exp(a_log), 2 * heads),
      jnp.tile(norm_w, heads).reshape(1, width), exb, exg, ones)


def _outproj_kernel(x_ref, s5_ref, ssd_ref, gdn_ref, wglu_ref, s5n_ref, wout_ref, o_ref):
    y = _gelu_tanh(s5_ref[...])
    y = y * _sigmoid(jnp.dot(y.astype(BF16), wglu_ref[...], preferred_element_type=F32))
    y = _rms(y, s5n_ref[...])
    mix = jnp.concatenate([y, ssd_ref[...], gdn_ref[...]], axis=1)
    o_ref[...] = x_ref[...] + jnp.dot(mix.astype(BF16), wout_ref[...], preferred_element_type=F32)


def _outproj(x2, y_s5, y_ssd, y_gdn, w_glu, s5_norm, w_out):
    n, d = x2.shape
    tm = min(ROW_BLOCK, n)
    ws = [y_s5.shape[1], y_ssd.shape[1], y_gdn.shape[1]]
    row = lambda i: (i, 0)
    const = lambda i: (0, 0)
    return pl.pallas_call(
        _outproj_kernel,
        grid=(n // tm,),
        in_specs=[pl.BlockSpec((tm, d), row)] + [pl.BlockSpec((tm, w), row) for w in ws]
                 + [pl.BlockSpec((ws[0], ws[0]), const), pl.BlockSpec((1, ws[0]), const),
                    pl.BlockSpec((sum(ws), d), const)],
        out_specs=pl.BlockSpec((tm, d), row),
        out_shape=jax.ShapeDtypeStruct((n, d), F32),
        compiler_params=_params("parallel"),
        name="outproj",
    )(x2, y_s5, y_ssd, y_gdn, w_glu.astype(BF16), s5_norm.reshape(1, -1), w_out.astype(BF16))


def _ffn_kernel(final, x_ref, g_ref, wg_ref, wu_ref, wd_ref, gf_ref, o_ref, h_ref, acc_ref):
    f = pl.program_id(1)

    @pl.when(f == 0)
    def _():
        h_ref[...] = _rms(x_ref[...], g_ref[...]).astype(BF16)
        acc_ref[...] = jnp.zeros(acc_ref.shape, F32)

    h = h_ref[...]
    a = jnp.dot(h, wg_ref[...].astype(BF16), preferred_element_type=F32)
    u = jnp.dot(h, wu_ref[...].astype(BF16), preferred_element_type=F32)
    acc_ref[...] += jnp.dot((_silu(a) * u).astype(BF16), wd_ref[...].astype(BF16), preferred_element_type=F32)

    @pl.when(f == pl.num_programs(1) - 1)
    def _():
        out = x_ref[...] + acc_ref[...]
        o_ref[...] = _rms(out, gf_ref[...]) if final else out


def _ffn(x2, gain, w_gate, w_up, w_down, final_gain):
    n, d = x2.shape
    ff = w_gate.shape[1]
    tm = min(FFN_ROW_BLOCK, n)
    tf = min(FFN_COL_BLOCK, ff)
    final = final_gain is not None
    gf = (final_gain if final else gain).reshape(1, d)
    return pl.pallas_call(
        functools.partial(_ffn_kernel, final),
        grid=(n // tm, ff // tf),
        in_specs=[pl.BlockSpec((tm, d), lambda i, f: (i, 0)),
                  pl.BlockSpec((1, d), lambda i, f: (0, 0)),
                  pl.BlockSpec((d, tf), lambda i, f: (0, f)),
                  pl.BlockSpec((d, tf), lambda i, f: (0, f)),
                  pl.BlockSpec((tf, d), lambda i, f: (f, 0)),
                  pl.BlockSpec((1, d), lambda i, f: (0, 0))],
        out_specs=pl.BlockSpec((tm, d), lambda i, f: (i, 0)),
        out_shape=jax.ShapeDtypeStruct((n, d), F32),
        scratch_shapes=[pltpu.VMEM((tm, d), BF16), pltpu.VMEM((tm, d), F32)],
        compiler_params=_params("parallel", "arbitrary"),
        name="ffn",
    )(x2, gain.reshape(1, d), w_gate, w_up, w_down, gf)


_R_E1, _R_E2, _R_W1, _R_W2, _R_RANK1, _R_RANK2 = range(6)


def _router_kernel(n_exp, x_ref, g_ref, wr_ref, tri_ref, hn_ref, rt_ref, cnt_ref, base_ref):
    @pl.when(pl.program_id(0) == 0)
    def _():
        base_ref[...] = jnp.zeros(base_ref.shape, F32)

    hn = _rms(x_ref[...], g_ref[...])
    hn_ref[...] = hn
    logits = jnp.dot(hn, wr_ref[...], precision=lax.Precision.HIGHEST, preferred_element_type=F32)
    lane = _iota2(logits.shape, 1)
    neg = jnp.float32(-jnp.inf)
    logits = jnp.where(lane < n_exp, logits, neg)
    m1 = jnp.max(logits, axis=-1, keepdims=True)
    i1 = jnp.min(jnp.where(logits == m1, lane, LANES), axis=-1, keepdims=True)
    rest = jnp.where(lane == i1, neg, logits)
    m2 = jnp.max(rest, axis=-1, keepdims=True)
    i2 = jnp.min(jnp.where(rest == m2, lane, LANES), axis=-1, keepdims=True)
    w1 = 1.0 / (1.0 + jnp.exp(m2 - m1))
    w2 = 1.0 / (1.0 + jnp.exp(m1 - m2))
    hit = (lane == i1) | (lane == i2)
    onehot = jnp.where(hit, 1.0, 0.0)
    before = jnp.dot(tri_ref[...], onehot.astype(BF16), preferred_element_type=F32) + base_ref[...]
    r1 = jnp.sum(jnp.where(lane == i1, before, 0.0), axis=-1, keepdims=True)
    r2 = jnp.sum(jnp.where(lane == i2, before, 0.0), axis=-1, keepdims=True)
    cols = {_R_E1: i1.astype(F32), _R_E2: i2.astype(F32), _R_W1: w1, _R_W2: w2, _R_RANK1: r1, _R_RANK2: r2}
    rt = jnp.zeros(logits.shape, F32)
    for k, v in cols.items():
        rt = jnp.where(lane == k, v, rt)
    rt_ref[...] = rt
    base_ref[...] += jnp.sum(onehot, axis=0, keepdims=True)
    cnt_ref[...] = base_ref[...]


def _row_copy(src_ref, src_row, dst_ref, dst_row, sem):
    return pltpu.make_async_copy(src_ref.at[pl.ds(src_row, 1)], dst_ref.at[pl.ds(dst_row, 1)], sem)


def _dispatch_kernel(n_exp, rb, s1_ref, s2_ref, fill_ref, hn_ref, xs_ref, zero_ref, sem, zsem):
    tm = hn_ref.shape[0]
    base = pl.program_id(0) * tm

    @pl.when(pl.program_id(0) == 0)
    def _():
        zero_ref[...] = jnp.zeros(zero_ref.shape, F32)
        for e in range(n_exp):
            lo, hi = fill_ref[e], fill_ref[n_exp + e]

            def start_row(r, carry):
                _row_copy(zero_ref, 0, xs_ref, r, zsem).start()
                return carry

            def wait_row(r, carry):
                _row_copy(zero_ref, 0, xs_ref, r, zsem).wait()
                return carry

            lax.fori_loop(lo, hi, start_row, 0)
            lax.fori_loop(lo, hi, wait_row, 0)

        def block_copy(b):
            return pltpu.make_async_copy(zero_ref, xs_ref.at[pl.ds(pl.multiple_of(b * rb, rb), rb)], zsem)

        def start_block(b, carry):
            block_copy(b).start()
            return carry

        def wait_block(b, carry):
            block_copy(b).wait()
            return carry

        n_blocks = xs_ref.shape[0] // rb
        lax.fori_loop(fill_ref[2 * n_exp], n_blocks, start_block, 0)
        lax.fori_loop(fill_ref[2 * n_exp], n_blocks, wait_block, 0)

    def issue(r, carry):
        _row_copy(hn_ref, r, xs_ref, s1_ref[base + r], sem).start()
        _row_copy(hn_ref, r, xs_ref, s2_ref[base + r], sem).start()
        return carry

    lax.fori_loop(0, tm, issue, 0, unroll=DMA_ISSUE_UNROLL)
    for _ in range(TOP_K):
        pltpu.make_async_copy(hn_ref, xs_ref.at[pl.ds(0, tm)], sem).wait()


def _expert_ffn_kernel(be_ref, bv_ref, xs_ref, wg_ref, wu_ref, wd_ref, ys_ref, h_ref, acc_ref):
    i = pl.program_id(0)
    f = pl.program_id(1)
    last = pl.num_programs(1) - 1
    valid = bv_ref[i]

    @pl.when(valid > 0)
    def _():
        @pl.when(f == 0)
        def _():
            row = _iota2((xs_ref.shape[0], 1), 0)
            h_ref[...] = jnp.where(row < valid, xs_ref[...], 0.0).astype(BF16)
            acc_ref[...] = jnp.zeros(acc_ref.shape, F32)

        h = h_ref[...]
        a = jnp.dot(h, wg_ref[0].astype(BF16), preferred_element_type=F32)
        u = jnp.dot(h, wu_ref[0].astype(BF16), preferred_element_type=F32)
        acc_ref[...] += jnp.dot((_silu(a) * u).astype(BF16), wd_ref[0].astype(BF16), preferred_element_type=F32)

        @pl.when(f == last)
        def _():
            ys_ref[...] = acc_ref[...]

    @pl.when((valid == 0) & (f == last))
    def _():
        ys_ref[...] = jnp.zeros(ys_ref.shape, F32)


def _combine_kernel(final, s1_ref, s2_ref, x_ref, rt_ref, gf_ref, ys_ref, o_ref, b1_ref, b2_ref, sem):
    tm = x_ref.shape[0]
    base = pl.program_id(0) * tm

    def issue(r, carry):
        _row_copy(ys_ref, s1_ref[base + r], b1_ref, r, sem).start()
        _row_copy(ys_ref, s2_ref[base + r], b2_ref, r, sem).start()
        return carry

    lax.fori_loop(0, tm, issue, 0, unroll=DMA_ISSUE_UNROLL)
    for buf in (b1_ref, b2_ref):
        pltpu.make_async_copy(ys_ref.at[pl.ds(0, tm)], buf, sem).wait()
    rt = rt_ref[...]
    out = (x_ref[...] + rt[:, _R_W1:_R_W1 + 1] * b1_ref[...] + rt[:, _R_W2:_R_W2 + 1] * b2_ref[...])
    o_ref[...] = _rms(out, gf_ref[...]) if final else out


def _moe(x2, gain, w_router, w_gate, w_up, w_down, final_gain):
    n, d = x2.shape
    n_exp, _, ff = w_gate.shape
    final = final_gain is not None
    gf = (final_gain if final else gain).reshape(1, d)
    wr = jnp.zeros((d, LANES), F32).at[:, :n_exp].set(w_router)

    tr = min(ROUTER_BLOCK, n)
    tri = (jnp.arange(tr)[:, None] > jnp.arange(tr)[None, :]).astype(BF16)
    hn, route, counts = pl.pallas_call(
        functools.partial(_router_kernel, n_exp),
        grid=(n // tr,),
        in_specs=[pl.BlockSpec((tr, d), lambda i: (i, 0)),
                  pl.BlockSpec((1, d), lambda i: (0, 0)),
                  pl.BlockSpec((d, LANES), lambda i: (0, 0)),
                  pl.BlockSpec((tr, tr), lambda i: (0, 0))],
        out_specs=[pl.BlockSpec((tr, d), lambda i: (i, 0)),
                   pl.BlockSpec((tr, LANES), lambda i: (i, 0)),
                   pl.BlockSpec((1, LANES), lambda i: (0, 0))],
        out_shape=[jax.ShapeDtypeStruct((n, d), F32),
                   jax.ShapeDtypeStruct((n, LANES), F32),
                   jax.ShapeDtypeStruct((1, LANES), F32)],
        scratch_shapes=[pltpu.VMEM((1, LANES), F32)],
        compiler_params=_params("arbitrary"),
        name="moe_router",
    )(x2, gain.reshape(1, d), wr, tri)

    rb = min(MOE_ROW_BLOCK, TOP_K * n)
    n_slots = TOP_K * n + n_exp * rb
    n_blocks = n_slots // rb
    cnt = counts[0, :n_exp].astype(jnp.int32)
    padded = ((cnt + rb - 1) // rb) * rb
    ends = jnp.cumsum(padded)
    starts = ends - padded
    as_int = lambda k: route[:, k].astype(jnp.int32)
    slot1 = starts[as_int(_R_E1)] + as_int(_R_RANK1)
    slot2 = starts[as_int(_R_E2)] + as_int(_R_RANK2)
    block_row = jnp.arange(n_blocks, dtype=jnp.int32) * rb
    block_exp = jnp.minimum(jnp.sum(block_row[:, None] >= ends[None, :], axis=1), n_exp - 1).astype(jnp.int32)
    block_valid = jnp.clip(cnt[block_exp] - (block_row - starts[block_exp]), 0, rb).astype(jnp.int32)

    tdp = min(DISPATCH_BLOCK, n)
    fill = jnp.concatenate([starts + cnt, ends, ends[-1:] // rb]).astype(jnp.int32)
    xs = pl.pallas_call(
        functools.partial(_dispatch_kernel, n_exp, rb),
        grid_spec=pltpu.PrefetchScalarGridSpec(
            num_scalar_prefetch=3,
            grid=(n // tdp,),
            in_specs=[pl.BlockSpec((tdp, d), lambda i, s1, s2, fl: (i, 0))],
            out_specs=pl.BlockSpec(memory_space=pl.ANY),
            scratch_shapes=[pltpu.VMEM((rb, d), F32), pltpu.SemaphoreType.DMA, pltpu.SemaphoreType.DMA]),
        out_shape=jax.ShapeDtypeStruct((n_slots, d), F32),
        compiler_params=_params("arbitrary"),
        name="moe_dispatch",
    )(slot1, slot2, fill, hn)

    tf = min(FFN_COL_BLOCK, ff)
    ys = pl.pallas_call(
        _expert_ffn_kernel,
        grid_spec=pltpu.PrefetchScalarGridSpec(
            num_scalar_prefetch=2,
            grid=(n_blocks, ff // tf),
            in_specs=[pl.BlockSpec((rb, d), lambda i, f, be, bv: (i, 0)),
                      pl.BlockSpec((1, d, tf), lambda i, f, be, bv: (be[i], 0, f)),
                      pl.BlockSpec((1, d, tf), lambda i, f, be, bv: (be[i], 0, f)),
                      pl.BlockSpec((1, tf, d), lambda i, f, be, bv: (be[i], f, 0))],
            out_specs=pl.BlockSpec((rb, d), lambda i, f, be, bv: (i, 0)),
            scratch_shapes=[pltpu.VMEM((rb, d), BF16), pltpu.VMEM((rb, d), F32)]),
        out_shape=jax.ShapeDtypeStruct((n_slots, d), F32),
        compiler_params=_params("parallel", "arbitrary"),
        name="moe_experts",
    )(block_exp, block_valid, xs, w_gate, w_up, w_down)

    tc = min(COMBINE_BLOCK, n)
    return pl.pallas_call(
        functools.partial(_combine_kernel, final),
        grid_spec=pltpu.PrefetchScalarGridSpec(
            num_scalar_prefetch=2,
            grid=(n // tc,),
            in_specs=[pl.BlockSpec((tc, d), lambda i, s1, s2: (i, 0)),
                      pl.BlockSpec((tc, LANES), lambda i, s1, s2: (i, 0)),
                      pl.BlockSpec((1, d), lambda i, s1, s2: (0, 0)),
                      pl.BlockSpec(memory_space=pl.ANY)],
            out_specs=pl.BlockSpec((tc, d), lambda i, s1, s2: (i, 0)),
            scratch_shapes=[pltpu.VMEM((tc, d), F32), pltpu.VMEM((tc, d), F32), pltpu.SemaphoreType.DMA]),
        out_shape=jax.ShapeDtypeStruct((n, d), F32),
        compiler_params=_params("arbitrary"),
        name="moe_combine",
    )(slot1, slot2, x2, route, gf, ys)


def _arrange_in_proj(w_in, s5_w, ssd_w, ssd_cdim, ssd_h, gdn_cdim, gdn_w, gdn_h):
    sizes = (s5_w, ssd_w, ssd_cdim, ssd_h, gdn_cdim, gdn_w, gdn_h, gdn_h)
    offs = np.cumsum((0,) + sizes)
    seg = lambda i: w_in[:, offs[i]:offs[i + 1]]
    small = jnp.concatenate([seg(3), seg(6), seg(7)], axis=1)
    small = jnp.pad(small, ((0, 0), (0, LANES - small.shape[1])))
    w = jnp.concatenate([seg(0), seg(1), seg(2), seg(4), seg(5), small], axis=1)
    return w.astype(BF16), (s5_w, ssd_w, ssd_cdim, gdn_cdim, gdn_w, LANES)


def kernel(x, norm_mix, w_in, w_out, s5_a_re, s5_a_im, s5_b_re, s5_b_im, s5_c_re, s5_c_im, s5_d, s5_log_step, s5_w_glu, s5_norm, ssd_conv_w, ssd_conv_b, ssd_dt_bias, ssd_a_log, ssd_d, ssd_norm, gdn_conv_w, gdn_a_log, gdn_dt_bias, gdn_norm, norm_ffn, ff_w_gate, ff_w_up, ff_w_down, moe_router, moe_w_gate, moe_w_up, moe_w_down, norm_final):
    bsz, seq, d = x.shape
    depth = norm_mix.shape[0]
    s5_w = s5_w_glu.shape[-1]
    ssd_w, ssd_h, ssd_cdim = ssd_norm.shape[-1], ssd_a_log.shape[-1], ssd_conv_w.shape[-1]
    gdn_h, gdn_cdim = gdn_a_log.shape[-1], gdn_conv_w.shape[-1]
    gdn_w = gdn_h * GDN_HEADDIM
    xr = x.reshape(bsz * seq, d)
    for layer in range(depth):
        w_arr, sizes = _arrange_in_proj(w_in[layer], s5_w, ssd_w, ssd_cdim, ssd_h, gdn_cdim, gdn_w, gdn_h)
        u_s5, z_ssd, xbc, qkv, z_gdn, small = _inproj(xr, norm_mix[layer], w_arr, sizes)
        s5_ops = _s5_operands(s5_a_re[layer], s5_a_im[layer], s5_b_re[layer], s5_b_im[layer], s5_c_re[layer],
                              s5_c_im[layer], s5_d[layer], s5_log_step[layer])
        y_s5 = _s5_scan(u_s5, s5_ops, bsz, seq)
        y_ssd = _ssd(z_ssd, xbc, small, ssd_conv_w[layer], ssd_conv_b[layer], ssd_dt_bias[layer],
                     ssd_a_log[layer], ssd_d[layer], ssd_norm[layer], bsz, seq)
        y_gdn = _gdn(qkv, z_gdn, small, gdn_conv_w[layer], gdn_a_log[layer], gdn_dt_bias[layer],
                     gdn_norm[layer], bsz, seq)
        xr = _outproj(xr, y_s5, y_ssd, y_gdn, s5_w_glu[layer], s5_norm[layer], w_out[layer])
        final_gain = norm_final if layer == depth - 1 else None
        i = layer // 2
        if layer % 2 == 0:
            xr = _ffn(xr, norm_ffn[layer], ff_w_gate[i], ff_w_up[i], ff_w_down[i], final_gain)
        else:
            xr = _moe(xr, norm_ffn[layer], moe_router[i], moe_w_gate[i], moe_w_up[i], moe_w_down[i], final_gain)
    return xr.reshape(bsz, seq, d)
```

```python
import functools
import math

import jax
import jax.numpy as jnp
import numpy as np
from jax import lax
from jax.experimental import pallas as pl
from jax.experimental.pallas import tpu as pltpu

F32 = jnp.float32
BF16 = jnp.bfloat16
EPS = 1e-6

S5_GROUP = 16
S5_STATE = 64
SSD_HEADDIM = 64
SSD_NGROUPS = 2
SSD_STATE = 128
GDN_HEADDIM = 64
CONV_K = 4
TOP_K = 2

LANES = 128
SUBLANES = 8
VMEM_LIMIT_BYTES = 56 * 1024 * 1024

SSD_CHUNK = 128
GDN_CHUNK = 64
GDN_INV_BLOCK = 16
GDN_GROUP = 256
SEQ_BLOCK = 512
ROW_BLOCK = 512
FFN_ROW_BLOCK = 1024
FFN_COL_BLOCK = 512
ROUTER_BLOCK = 1024
DISPATCH_BLOCK = 1024
MOE_ROW_BLOCK = 1024
COMBINE_BLOCK = 512
DMA_ISSUE_UNROLL = 8


def _params(*semantics):
    return pltpu.CompilerParams(dimension_semantics=semantics, vmem_limit_bytes=VMEM_LIMIT_BYTES)


def _dot(a, b):
    return jnp.dot(a.astype(BF16), b.astype(BF16), preferred_element_type=F32)


def _dot_nt(a, b):
    return lax.dot_general(a.astype(BF16), b.astype(BF16), (((1,), (1,)), ((), ())),
                           preferred_element_type=F32)


def _dot_tn(a, b):
    return lax.dot_general(a.astype(BF16), b.astype(BF16), (((0,), (0,)), ((), ())),
                           preferred_element_type=F32)


def _split3(a):
    hi = a.astype(BF16)
    r1 = a - hi.astype(F32)
    mid = r1.astype(BF16)
    lo = (r1 - mid.astype(F32)).astype(BF16)
    return hi, mid, lo


def _dot_sel(a, sel):
    hi, mid, lo = _split3(a)
    out = jnp.dot(hi, sel, preferred_element_type=F32)
    out += jnp.dot(mid, sel, preferred_element_type=F32)
    out += jnp.dot(lo, sel, preferred_element_type=F32)
    return out


def _sel_dot(sel, a):
    hi, mid, lo = _split3(a)
    out = jnp.dot(sel, hi, preferred_element_type=F32)
    out += jnp.dot(sel, mid, preferred_element_type=F32)
    out += jnp.dot(sel, lo, preferred_element_type=F32)
    return out


def _sigmoid(x):
    return 1.0 / (1.0 + jnp.exp(-x))


def _silu(x):
    return x * _sigmoid(x)


def _softplus(x):
    return jnp.maximum(x, 0.0) + jnp.log(1.0 + jnp.exp(-jnp.abs(x)))


def _gelu_tanh(x):
    c = math.sqrt(2.0 / math.pi)
    return 0.5 * x * (1.0 + jnp.tanh(c * (x + 0.044715 * (x * x * x))))


def _rms(x, w):
    return x * lax.rsqrt(jnp.mean(x * x, axis=-1, keepdims=True) + EPS) * w


def _iota2(shape, axis):
    return lax.broadcasted_iota(jnp.int32, shape, axis)


def _inproj_kernel(layer, sizes, cuts, x_ref, g_ref, w_hbm, small_ref, *rest):
    out_refs = rest[:len(sizes)]
    raw_ref, w_ref = rest[len(sizes):]

    @pl.when(pl.program_id(0) == 0)
    def _():
        pltpu.sync_copy(w_hbm.at[layer], raw_ref)
        ncols = raw_ref.shape[1]
        off = 0
        for lo, hi in cuts:
            a = (lo // LANES) * LANES
            b = min(-(-hi // LANES) * LANES, ncols)
            w_ref[:, off:off + hi - lo] = raw_ref[:, a:b][:, lo - a:hi - a].astype(BF16)
            off += hi - lo
        w_ref[:, off:off + LANES] = small_ref[...].astype(BF16)

    h = _rms(x_ref[...], g_ref[...])
    p = jnp.dot(h.astype(BF16), w_ref[...], preferred_element_type=F32)
    off = 0
    for o_ref, sz in zip(out_refs, sizes):
        o_ref[...] = p[:, off:off + sz]
        off += sz


def _inproj(x2, gain, w_all, layer, small, sizes, cuts):
    n, d = x2.shape
    tm = min(ROW_BLOCK, n)
    total = sum(sizes)
    return pl.pallas_call(
        functools.partial(_inproj_kernel, layer, sizes, cuts),
        grid=(n // tm,),
        in_specs=[pl.BlockSpec((tm, d), lambda i: (i, 0)),
                  pl.BlockSpec((1, d), lambda i: (0, 0)),
                  pl.BlockSpec(memory_space=pl.ANY),
                  pl.BlockSpec((d, LANES), lambda i: (0, 0))],
        out_specs=[pl.BlockSpec((tm, sz), lambda i: (i, 0)) for sz in sizes],
        out_shape=[jax.ShapeDtypeStruct((n, sz), F32) for sz in sizes],
        scratch_shapes=[pltpu.VMEM(w_all.shape[1:], F32), pltpu.VMEM((d, total), BF16)],
        compiler_params=_params("arbitrary"),
        name="inproj",
    )(x2, gain.reshape(1, d), w_all, small)


def _s5_operands(a_re, a_im, b_re, b_im, c_re, c_im, d_skip, log_step):
    g, p = a_re.shape
    h = b_re.shape[-1]
    step = jnp.exp(log_step)[:, None]
    mag = jnp.exp(a_re * step)
    ang = a_im * step
    lb_re, lb_im = mag * jnp.cos(ang), mag * jnp.sin(ang)
    den = a_re * a_re + a_im * a_im
    f_re = ((lb_re - 1.0) * a_re + lb_im * a_im) / den
    f_im = (lb_im * a_re - (lb_re - 1.0) * a_im) / den
    bb_re = f_re[..., None] * b_re - f_im[..., None] * b_im
    bb_im = f_re[..., None] * b_im + f_im[..., None] * b_re

    eye = jnp.eye(g, dtype=F32)
    to_state = lambda m: (m.transpose(0, 2, 1)[:, :, None, :] * eye[:, None, :, None]).reshape(g * h, g * p)
    to_out = lambda m: (m.transpose(0, 2, 1)[:, :, None, :] * eye[:, None, :, None]).reshape(g * p, g * h)
    bbd = jnp.concatenate([to_state(bb_re), to_state(bb_im)], axis=1)
    cbd = jnp.concatenate([to_out(c_re), -to_out(c_im)], axis=0)
    e = jnp.arange(1, SUBLANES + 1, dtype=F32)[:, None, None]
    m = jnp.exp(a_re * step * e)
    lam = jnp.concatenate([(m * jnp.cos(ang * e)).reshape(SUBLANES, g * p),
                           (m * jnp.sin(ang * e)).reshape(SUBLANES, g * p)], axis=1)
    return bbd, cbd, lam, d_skip.reshape(1, g * h)


def _s5_kernel(u_ref, bbd_ref, cbd_ref, lam_ref, d_ref, y_ref, s_ref, carry_ref):
    t, s2 = s_ref.shape
    half = s2 // 2
    tile = SUBLANES

    @pl.when(pl.program_id(1) == 0)
    def _():
        carry_ref[...] = jnp.zeros(carry_ref.shape, F32)

    u = u_ref[...]
    bu = jnp.dot(u.astype(BF16), bbd_ref[...], preferred_element_type=F32)
    re = bu[:, :half].reshape(t // tile, tile, half)
    im = bu[:, half:].reshape(t // tile, tile, half)
    row = _iota2((1, tile, half), 1)
    sh = 1
    while sh < tile:
        ar = lam_ref[sh - 1:sh, :half]
        ai = lam_ref[sh - 1:sh, half:]
        keep = row >= sh
        pr = jnp.where(keep, pltpu.roll(re, sh, axis=1), 0.0)
        pi = jnp.where(keep, pltpu.roll(im, sh, axis=1), 0.0)
        re, im = re + ar * pr - ai * pi, im + ar * pi + ai * pr
        sh *= 2
    s_ref[:, :half] = re.reshape(t, half)
    s_ref[:, half:] = im.reshape(t, half)

    lr = lam_ref[:, :half]
    li = lam_ref[:, half:]

    def tile_body(k, carry):
        cr, ci = carry
        r0 = pl.multiple_of(k * tile, tile)
        tr = s_ref[pl.ds(r0, tile), :half] + lr * cr - li * ci
        ti = s_ref[pl.ds(r0, tile), half:] + lr * ci + li * cr
        s_ref[pl.ds(r0, tile), :half] = tr
        s_ref[pl.ds(r0, tile), half:] = ti
        return (jnp.broadcast_to(tr[tile - 1:tile], tr.shape), jnp.broadcast_to(ti[tile - 1:tile], ti.shape))

    cr, ci = lax.fori_loop(0, t // tile, tile_body, (carry_ref[:, :half], carry_ref[:, half:]))
    carry_ref[:, :half] = cr
    carry_ref[:, half:] = ci
    y_ref[...] = (jnp.dot(s_ref[...].astype(BF16), cbd_ref[...], preferred_element_type=F32)
                  + d_ref[...] * u)


def _s5_scan(u, operands, bsz, seq):
    bbd, cbd, lam, d = operands
    n, width = u.shape
    s2 = bbd.shape[1]
    t = min(SEQ_BLOCK, seq)
    nblk = seq // t
    tok = lambda b, c: (b * nblk + c, 0)
    const = lambda b, c: (0, 0)
    return pl.pallas_call(
        _s5_kernel,
        grid=(bsz, nblk),
        in_specs=[pl.BlockSpec((t, width), tok),
                  pl.BlockSpec((width, s2), const),
                  pl.BlockSpec((s2, width), const),
                  pl.BlockSpec((SUBLANES, s2), const),
                  pl.BlockSpec((1, width), const)],
        out_specs=pl.BlockSpec((t, width), tok),
        out_shape=jax.ShapeDtypeStruct((n, width), F32),
        scratch_shapes=[pltpu.VMEM((t, s2), F32), pltpu.VMEM((SUBLANES, s2), F32)],
        compiler_params=_params("parallel", "arbitrary"),
        name="s5_scan",
    )(u, bbd.astype(BF16), cbd.astype(BF16), lam, d)


def _causal_conv_block(x_ref, w_ref, pad_ref, first):
    t = x_ref.shape[0]
    halo = SUBLANES

    @pl.when(first)
    def _():
        pad_ref[0:halo, :] = jnp.zeros((halo, pad_ref.shape[1]), F32)

    @pl.when(jnp.logical_not(first))
    def _():
        pad_ref[0:halo, :] = pad_ref[t:t + halo, :]

    pad_ref[halo:halo + t, :] = x_ref[...]
    acc = w_ref[CONV_K - 1:CONV_K, :] * pad_ref[halo:halo + t, :]
    for j in range(1, CONV_K):
        acc += w_ref[CONV_K - 1 - j:CONV_K - j, :] * pad_ref[halo - j:halo - j + t, :]
    return acc


def _ssd_kernel(heads, z_ref, xbc_ref, sm_ref, cw_ref, cb_ref, dtb_ref, a_ref, dexp_ref, nw_ref, ex_ref,
                o_ref, pad_ref, xc_ref, dt_ref, st_ref):
    t = xbc_ref.shape[0]
    q = min(SSD_CHUNK, t)
    width = heads * SSD_HEADDIM
    gw = SSD_NGROUPS * SSD_STATE
    first = pl.program_id(1) == 0

    @pl.when(first)
    def _():
        st_ref[...] = jnp.zeros(st_ref.shape, F32)

    xc_ref[...] = _silu(_causal_conv_block(xbc_ref, cw_ref, pad_ref, first) + cb_ref[...])
    dt_ref[...] = _softplus(sm_ref[...] + dtb_ref[...])

    row = _iota2((q, q), 0)
    col = _iota2((q, q), 1)
    causal = row >= col
    tril = jnp.where(causal, 1.0, 0.0).astype(BF16)
    lane = _iota2((1, width), 1)
    rep = heads // SSD_NGROUPS

    def chunk_body(ci, carry):
        r0 = ci * q
        xs = xc_ref[pl.ds(r0, q), 0:width]
        bm = xc_ref[pl.ds(r0, q), width:width + gw]
        cm = xc_ref[pl.ds(r0, q), width + gw:width + 2 * gw]
        dt = dt_ref[pl.ds(r0, q), :]
        acs = _sel_dot(tril, dt * a_ref[...])
        acs_t = acs.T
        xdt = xs * _dot_sel(dt, ex_ref[...])
        hs = range(heads)
        bgs = [bm[:, gi * SSD_STATE:(gi + 1) * SSD_STATE] for gi in range(SSD_NGROUPS)]
        cgs = [cm[:, gi * SSD_STATE:(gi + 1) * SSD_STATE].astype(BF16) for gi in range(SSD_NGROUPS)]
        cbs = [_dot_nt(cgs[gi], bgs[gi]) for gi in range(SSD_NGROUPS)]
        xdb = xdt.astype(BF16)
        xh = [xdb[:, hh * SSD_HEADDIM:(hh + 1) * SSD_HEADDIM] for hh in hs]
        acol = [acs[:, hh:hh + 1] for hh in hs]
        alast = [acs[q - 1:q, hh:hh + 1] for hh in hs]
        decay = [jnp.where(causal, jnp.exp(jnp.where(causal, acol[hh] - acs_t[hh:hh + 1, :], 0.0)), 0.0)
                 for hh in hs]
        state = [st_ref[hh] for hh in hs]
        y_diag = [_mm((cbs[hh // rep] * decay[hh]).astype(BF16), xh[hh]) for hh in hs]
        y_off = [_mm(cgs[hh // rep], state[hh].astype(BF16)) * jnp.exp(acol[hh]) for hh in hs]
        upd = [_dot_tn(bgs[hh // rep] * jnp.exp(alast[hh] - acol[hh]), xh[hh]) for hh in hs]
        for hh in hs:
            st_ref[hh] = state[hh] * jnp.exp(alast[hh]) + upd[hh]
        y = jnp.concatenate([y_diag[hh] + y_off[hh] for hh in hs], axis=1) + dexp_ref[...] * xs
        y = y * _silu(z_ref[pl.ds(r0, q), :])
        gsz = width // SSD_NGROUPS
        y2 = y * y
        out = jnp.zeros_like(y)
        for gi in range(SSD_NGROUPS):
            m = (lane >= gi * gsz) & (lane < (gi + 1) * gsz)
            ms = jnp.sum(jnp.where(m, y2, 0.0), axis=-1, keepdims=True) / gsz
            out = jnp.where(m, y * lax.rsqrt(ms + EPS), out)
        o_ref[pl.ds(r0, q), :] = out * nw_ref[...]
        return carry

    for ci in range(t // q):
        chunk_body(ci, 0)


def _ssd(z, xbc, small, conv_w, conv_b, dt_bias, a_log, d_skip, norm_w, bsz, seq):
    n, width = z.shape
    heads = a_log.shape[0]
    cdim = xbc.shape[1]
    t = min(SEQ_BLOCK, seq)
    nblk = seq // t
    pad = lambda v: jnp.zeros((1, LANES), F32).at[0, :heads].set(v)
    expand = (jnp.arange(LANES)[:, None] == (jnp.arange(width)[None, :] // SSD_HEADDIM)).astype(BF16)
    tok = lambda b, c: (b * nblk + c, 0)
    const = lambda b, c: (0, 0)
    return pl.pallas_call(
        functools.partial(_ssd_kernel, heads),
        grid=(bsz, nblk),
        in_specs=[pl.BlockSpec((t, width), tok),
                  pl.BlockSpec((t, cdim), tok),
                  pl.BlockSpec((t, LANES), tok),
                  pl.BlockSpec((CONV_K, cdim), const),
                  pl.BlockSpec((1, cdim), const),
                  pl.BlockSpec((1, LANES), const),
                  pl.BlockSpec((1, LANES), const),
                  pl.BlockSpec((1, width), const),
                  pl.BlockSpec((1, width), const),
                  pl.BlockSpec((LANES, width), const)],
        out_specs=pl.BlockSpec((t, width), tok),
        out_shape=jax.ShapeDtypeStruct((n, width), F32),
        scratch_shapes=[pltpu.VMEM((SUBLANES + t, cdim), F32),
                        pltpu.VMEM((t, cdim), F32),
                        pltpu.VMEM((t, LANES), F32),
                        pltpu.VMEM((heads, SSD_STATE, SSD_HEADDIM), F32)],
        compiler_params=_params("parallel", "arbitrary"),
        name="ssd",
    )(z, xbc, small, conv_w, conv_b.reshape(1, cdim), pad(dt_bias), pad(-jnp.exp(a_log)),
      jnp.repeat(d_skip, SSD_HEADDIM).reshape(1, width), norm_w.reshape(1, width), expand)


def _mm(a, b):
    return jnp.dot(a, b, preferred_element_type=F32)


def _unit_lower_inverses(lms, eye, blk, chunk):
    lds = [jnp.where(blk, lm, 0.0) for lm in lms]
    offs = [(lm - ld).astype(BF16) for lm, ld in zip(lms, lds)]
    dinv = [eye - ld for ld in lds]
    pw = [ld.astype(BF16) for ld in lds]
    span = 2
    while span < GDN_INV_BLOCK:
        sq = [_mm(p, p) for p in pw]
        dinv = [_mm(d.astype(BF16), (eye + s).astype(BF16)) for d, s in zip(dinv, sq)]
        pw = [s.astype(BF16) for s in sq]
        span *= 2
    dinv = [d.astype(BF16) for d in dinv]
    ms = [_mm(d, o) for d, o in zip(dinv, offs)]
    out = [eye - m for m in ms]
    pw = [m.astype(BF16) for m in ms]
    span = 2
    while span < chunk // GDN_INV_BLOCK:
        sq = [_mm(p, p) for p in pw]
        out = [_mm(o.astype(BF16), (eye + s).astype(BF16)) for o, s in zip(out, sq)]
        pw = [s.astype(BF16) for s in sq]
        span *= 2
    return [_mm(o.astype(BF16), d) for o, d in zip(out, dinv)]


def _gdn_kernel(heads, qkv_ref, z_ref, sm_ref, cw_ref, dtb_ref, a_ref, nw_ref, exb_ref, exg_ref, ones_ref,
                o_ref, pad_ref, st_ref):
    t = qkv_ref.shape[0]
    c = min(GDN_CHUNK, t)
    hd = GDN_HEADDIM
    width = heads * hd
    hs = range(heads)
    sl = [slice(hh * hd, (hh + 1) * hd) for hh in hs]
    first = pl.program_id(1) == 0

    @pl.when(first)
    def _():
        st_ref[...] = jnp.zeros(st_ref.shape, F32)

    def seg_sum(v):
        hi = v.astype(BF16)
        lo = (v - hi.astype(F32)).astype(BF16)
        return _mm(hi, ones_ref[...]) + _mm(lo, ones_ref[...])

    xc = _silu(_causal_conv_block(qkv_ref, cw_ref, pad_ref, first))
    qf = xc[:, 0:width]
    kf = xc[:, width:2 * width]
    vf = xc[:, 2 * width:3 * width]
    qf = qf * lax.rsqrt(seg_sum(qf * qf) + EPS) * (hd ** -0.5)
    kf = kf * lax.rsqrt(seg_sum(kf * kf) + EPS)
    sm = sm_ref[...]
    beta = _dot_sel(_sigmoid(sm), exb_ref[...])
    g = a_ref[...] * _softplus(sm + dtb_ref[...])

    row = _iota2((t, t), 0)
    col = _iota2((t, t), 1)
    same = (row // c) == (col // c)
    causal = same & (row >= col)
    strict = same & (row > col)
    eye = jnp.where(row == col, 1.0, 0.0)
    blk = (row // GDN_INV_BLOCK) == (col // GDN_INV_BLOCK)
    gcs = _sel_dot(jnp.where(causal, 1.0, 0.0).astype(BF16), g)
    gtot = _sel_dot(jnp.where(same, 1.0, 0.0).astype(BF16), g)
    gcs_t = gcs.T
    gw = _dot_sel(gcs, exg_ref[...])
    gtw = _dot_sel(gtot, exg_ref[...])
    gexp = jnp.exp(gw)
    kb = kf * beta
    vb = vf * beta
    kcd_in = kb * gexp
    qd = (qf * gexp).astype(BF16)
    kend = (kf * jnp.exp(gtw - gw)).astype(BF16)
    cdec = jnp.exp(gtw)
    kfb = kf.astype(BF16)
    kbb = kb.astype(BF16)
    qfb = qf.astype(BF16)
    nt = (((1,), (1,)), ((), ()))
    decay = []
    for hh in hs:
        lg = 2 * heads + hh
        diff = jnp.where(causal, gcs[:, lg:lg + 1] - gcs_t[lg:lg + 1, :], 0.0)
        decay.append(jnp.where(causal, jnp.exp(diff), 0.0))
    kk = [lax.dot_general(kbb[:, sl[hh]], kfb[:, sl[hh]], nt, preferred_element_type=F32) for hh in hs]
    qk = [lax.dot_general(qfb[:, sl[hh]], kfb[:, sl[hh]], nt, preferred_element_type=F32) for hh in hs]
    lms = [jnp.where(strict, kk[hh] * decay[hh], 0.0) for hh in hs]
    qk = [(qk[hh] * decay[hh]).astype(BF16) for hh in hs]
    tinv = _unit_lower_inverses(lms, eye, blk, c)
    rhs = [jnp.concatenate([vb[:, sl[hh]], kcd_in[:, sl[hh]]], axis=1).astype(BF16) for hh in hs]
    vk = [_mm(tinv[hh].astype(BF16), rhs[hh]) for hh in hs]

    state = [st_ref[hh] for hh in hs]
    outs = []
    for ci in range(t // c):
        rs = slice(ci * c, (ci + 1) * c)
        lhs = [jnp.concatenate([vk[hh][rs, hd:2 * hd].astype(BF16), qd[rs, sl[hh]]], axis=0) for hh in hs]
        sb = [s.astype(BF16) for s in state]
        both = [_mm(lhs[hh], sb[hh]) for hh in hs]
        v_new = [(vk[hh][rs, 0:hd] - both[hh][0:c]).astype(BF16) for hh in hs]
        o = [both[hh][c:2 * c] + _mm(qk[hh][rs, ci * c:(ci + 1) * c], v_new[hh]) for hh in hs]
        upd = [lax.dot_general(kend[rs, sl[hh]], v_new[hh], (((0,), (0,)), ((), ())),
                               preferred_element_type=F32) for hh in hs]
        state = [state[hh] * cdec[ci * c:ci * c + 1, sl[hh]] + upd[hh] for hh in hs]
        outs.append(jnp.concatenate(o, axis=1))
    for hh in hs:
        st_ref[hh] = state[hh]
    o = jnp.concatenate(outs, axis=0)
    o = o * lax.rsqrt(seg_sum(o * o) / hd + EPS) * nw_ref[...]
    o_ref[...] = o * _silu(z_ref[...])


def _gdn(qkv, z, small, conv_w, a_log, dt_bias, norm_w, bsz, seq):
    n, width = z.shape
    heads = a_log.shape[0]
    cdim = qkv.shape[1]
    t = min(GDN_GROUP, seq)
    nblk = seq // t
    pad = lambda v, off: jnp.zeros((1, LANES), F32).at[0, off:off + heads].set(v)
    head_of = jnp.arange(width)[None, :] // GDN_HEADDIM
    lanes = jnp.arange(LANES)[:, None]
    exb = (lanes == heads + head_of).astype(BF16)
    exg = (lanes == 2 * heads + head_of).astype(BF16)
    ones = (head_of.T == head_of).astype(BF16)
    tok = lambda b, c: (b * nblk + c, 0)
    const = lambda b, c: (0, 0)
    return pl.pallas_call(
        functools.partial(_gdn_kernel, heads),
        grid=(bsz, nblk),
        in_specs=[pl.BlockSpec((t, cdim), tok),
                  pl.BlockSpec((t, width), tok),
                  pl.BlockSpec((t, LANES), tok),
                  pl.BlockSpec((CONV_K, cdim), const),
                  pl.BlockSpec((1, LANES), const),
                  pl.BlockSpec((1, LANES), const),
                  pl.BlockSpec((1, width), const),
                  pl.BlockSpec((LANES, width), const),
                  pl.BlockSpec((LANES, width), const),
                  pl.BlockSpec((width, width), const)],
        out_specs=pl.BlockSpec((t, width), tok),
        out_shape=jax.ShapeDtypeStruct((n, width), F32),
        scratch_shapes=[pltpu.VMEM((SUBLANES + t, cdim), F32),
                        pltpu.VMEM((heads, GDN_HEADDIM, GDN_HEADDIM), F32)],
        compiler_params=_params("parallel", "arbitrary"),
        name="gdn",
    )(qkv, z, small, conv_w, pad(dt_bias, 2 * heads), pad(-jnp.exp(a_log), 2 * heads),
      jnp.tile(norm_w, heads).reshape(1, width), exb, exg, ones)


def _outproj_kernel(x_ref, s5_ref, ssd_ref, gdn_ref, wglu_ref, s5n_ref, wout_ref, o_ref):
    y = _gelu_tanh(s5_ref[...])
    y = y * _sigmoid(jnp.dot(y.astype(BF16), wglu_ref[...], preferred_element_type=F32))
    y = _rms(y, s5n_ref[...])
    mix = jnp.concatenate([y, ssd_ref[...], gdn_ref[...]], axis=1)
    o_ref[...] = x_ref[...] + jnp.dot(mix.astype(BF16), wout_ref[...], preferred_element_type=F32)


def _outproj(x2, y_s5, y_ssd, y_gdn, w_glu, s5_norm, w_out):
    n, d = x2.shape
    tm = min(ROW_BLOCK, n)
    ws = [y_s5.shape[1], y_ssd.shape[1], y_gdn.shape[1]]
    row = lambda i: (i, 0)
    const = lambda i: (0, 0)
    return pl.pallas_call(
        _outproj_kernel,
        grid=(n // tm,),
        in_specs=[pl.BlockSpec((tm, d), row)] + [pl.BlockSpec((tm, w), row) for w in ws]
                 + [pl.BlockSpec((ws[0], ws[0]), const), pl.BlockSpec((1, ws[0]), const),
                    pl.BlockSpec((sum(ws), d), const)],
        out_specs=pl.BlockSpec((tm, d), row),
        out_shape=jax.ShapeDtypeStruct((n, d), F32),
        compiler_params=_params("parallel"),
        name="outproj",
    )(x2, y_s5, y_ssd, y_gdn, w_glu.astype(BF16), s5_norm.reshape(1, -1), w_out.astype(BF16))


def _ffn_kernel(final, x_ref, g_ref, wg_ref, wu_ref, wd_ref, gf_ref, o_ref, h_ref, acc_ref):
    f = pl.program_id(1)

    @pl.when(f == 0)
    def _():
        h_ref[...] = _rms(x_ref[...], g_ref[...]).astype(BF16)
        acc_ref[...] = jnp.zeros(acc_ref.shape, F32)

    h = h_ref[...]
    a = jnp.dot(h, wg_ref[...].astype(BF16), preferred_element_type=F32)
    u = jnp.dot(h, wu_ref[...].astype(BF16), preferred_element_type=F32)
    acc_ref[...] += jnp.dot((_silu(a) * u).astype(BF16), wd_ref[...].astype(BF16), preferred_element_type=F32)

    @pl.when(f == pl.num_programs(1) - 1)
    def _():
        out = x_ref[...] + acc_ref[...]
        o_ref[...] = _rms(out, gf_ref[...]) if final else out


def _ffn(x2, gain, w_gate, w_up, w_down, final_gain):
    n, d = x2.shape
    ff = w_gate.shape[1]
    tm = min(FFN_ROW_BLOCK, n)
    tf = min(FFN_COL_BLOCK, ff)
    final = final_gain is not None
    gf = (final_gain if final else gain).reshape(1, d)
    return pl.pallas_call(
        functools.partial(_ffn_kernel, final),
        grid=(n // tm, ff // tf),
        in_specs=[pl.BlockSpec((tm, d), lambda i, f: (i, 0)),
                  pl.BlockSpec((1, d), lambda i, f: (0, 0)),
                  pl.BlockSpec((d, tf), lambda i, f: (0, f)),
                  pl.BlockSpec((d, tf), lambda i, f: (0, f)),
                  pl.BlockSpec((tf, d), lambda i, f: (f, 0)),
                  pl.BlockSpec((1, d), lambda i, f: (0, 0))],
        out_specs=pl.BlockSpec((tm, d), lambda i, f: (i, 0)),
        out_shape=jax.ShapeDtypeStruct((n, d), F32),
        scratch_shapes=[pltpu.VMEM((tm, d), BF16), pltpu.VMEM((tm, d), F32)],
        compiler_params=_params("parallel", "arbitrary"),
        name="ffn",
    )(x2, gain.reshape(1, d), w_gate, w_up, w_down, gf)


_R_E1, _R_E2, _R_W1, _R_W2, _R_RANK1, _R_RANK2 = range(6)


def _router_kernel(n_exp, x_ref, g_ref, wr_ref, tri_ref, hn_ref, rt_ref, cnt_ref, base_ref):
    @pl.when(pl.program_id(0) == 0)
    def _():
        base_ref[...] = jnp.zeros(base_ref.shape, F32)

    hn = _rms(x_ref[...], g_ref[...])
    hn_ref[...] = hn
    logits = jnp.dot(hn, wr_ref[...], precision=lax.Precision.HIGHEST, preferred_element_type=F32)
    lane = _iota2(logits.shape, 1)
    neg = jnp.float32(-jnp.inf)
    logits = jnp.where(lane < n_exp, logits, neg)
    m1 = jnp.max(logits, axis=-1, keepdims=True)
    i1 = jnp.min(jnp.where(logits == m1, lane, LANES), axis=-1, keepdims=True)
    rest = jnp.where(lane == i1, neg, logits)
    m2 = jnp.max(rest, axis=-1, keepdims=True)
    i2 = jnp.min(jnp.where(rest == m2, lane, LANES), axis=-1, keepdims=True)
    w1 = 1.0 / (1.0 + jnp.exp(m2 - m1))
    w2 = 1.0 / (1.0 + jnp.exp(m1 - m2))
    hit = (lane == i1) | (lane == i2)
    onehot = jnp.where(hit, 1.0, 0.0)
    before = jnp.dot(tri_ref[...], onehot.astype(BF16), preferred_element_type=F32) + base_ref[...]
    r1 = jnp.sum(jnp.where(lane == i1, before, 0.0), axis=-1, keepdims=True)
    r2 = jnp.sum(jnp.where(lane == i2, before, 0.0), axis=-1, keepdims=True)
    cols = {_R_E1: i1.astype(F32), _R_E2: i2.astype(F32), _R_W1: w1, _R_W2: w2, _R_RANK1: r1, _R_RANK2: r2}
    rt = jnp.zeros(logits.shape, F32)
    for k, v in cols.items():
        rt = jnp.where(lane == k, v, rt)
    rt_ref[...] = rt
    base_ref[...] += jnp.sum(onehot, axis=0, keepdims=True)
    cnt_ref[...] = base_ref[...]


def _row_copy(src_ref, src_row, dst_ref, dst_row, sem):
    return pltpu.make_async_copy(src_ref.at[pl.ds(src_row, 1)], dst_ref.at[pl.ds(dst_row, 1)], sem)


def _dispatch_kernel(n_exp, rb, s1_ref, s2_ref, fill_ref, hn_ref, xs_ref, zero_ref, sem, zsem):
    tm = hn_ref.shape[0]
    base = pl.program_id(0) * tm

    @pl.when(pl.program_id(0) == 0)
    def _():
        zero_ref[...] = jnp.zeros(zero_ref.shape, F32)
        for e in range(n_exp):
            lo, hi = fill_ref[e], fill_ref[n_exp + e]

            def start_row(r, carry):
                _row_copy(zero_ref, 0, xs_ref, r, zsem).start()
                return carry

            def wait_row(r, carry):
                _row_copy(zero_ref, 0, xs_ref, r, zsem).wait()
                return carry

            lax.fori_loop(lo, hi, start_row, 0)
            lax.fori_loop(lo, hi, wait_row, 0)

        def block_copy(b):
            return pltpu.make_async_copy(zero_ref, xs_ref.at[pl.ds(pl.multiple_of(b * rb, rb), rb)], zsem)

        def start_block(b, carry):
            block_copy(b).start()
            return carry

        def wait_block(b, carry):
            block_copy(b).wait()
            return carry

        n_blocks = xs_ref.shape[0] // rb
        lax.fori_loop(fill_ref[2 * n_exp], n_blocks, start_block, 0)
        lax.fori_loop(fill_ref[2 * n_exp], n_blocks, wait_block, 0)

    def issue(r, carry):
        _row_copy(hn_ref, r, xs_ref, s1_ref[base + r], sem).start()
        _row_copy(hn_ref, r, xs_ref, s2_ref[base + r], sem).start()
        return carry

    lax.fori_loop(0, tm, issue, 0, unroll=DMA_ISSUE_UNROLL)
    for _ in range(TOP_K):
        pltpu.make_async_copy(hn_ref, xs_ref.at[pl.ds(0, tm)], sem).wait()


def _expert_ffn_kernel(be_ref, bv_ref, xs_ref, wg_ref, wu_ref, wd_ref, ys_ref, h_ref, acc_ref):
    i = pl.program_id(0)
    f = pl.program_id(1)
    last = pl.num_programs(1) - 1
    valid = bv_ref[i]

    @pl.when(valid > 0)
    def _():
        @pl.when(f == 0)
        def _():
            row = _iota2((xs_ref.shape[0], 1), 0)
            h_ref[...] = jnp.where(row < valid, xs_ref[...], 0.0).astype(BF16)
            acc_ref[...] = jnp.zeros(acc_ref.shape, F32)

        h = h_ref[...]
        a = jnp.dot(h, wg_ref[0].astype(BF16), preferred_element_type=F32)
        u = jnp.dot(h, wu_ref[0].astype(BF16), preferred_element_type=F32)
        acc_ref[...] += jnp.dot((_silu(a) * u).astype(BF16), wd_ref[0].astype(BF16), preferred_element_type=F32)

        @pl.when(f == last)
        def _():
            ys_ref[...] = acc_ref[...]

    @pl.when((valid == 0) & (f == last))
    def _():
        ys_ref[...] = jnp.zeros(ys_ref.shape, F32)


def _combine_kernel(final, s1_ref, s2_ref, x_ref, rt_ref, gf_ref, ys_ref, o_ref, b1_ref, b2_ref, sem):
    tm = x_ref.shape[0]
    base = pl.program_id(0) * tm

    def issue(r, carry):
        _row_copy(ys_ref, s1_ref[base + r], b1_ref, r, sem).start()
        _row_copy(ys_ref, s2_ref[base + r], b2_ref, r, sem).start()
        return carry

    lax.fori_loop(0, tm, issue, 0, unroll=DMA_ISSUE_UNROLL)
    for buf in (b1_ref, b2_ref):
        pltpu.make_async_copy(ys_ref.at[pl.ds(0, tm)], buf, sem).wait()
    rt = rt_ref[...]
    out = (x_ref[...] + rt[:, _R_W1:_R_W1 + 1] * b1_ref[...] + rt[:, _R_W2:_R_W2 + 1] * b2_ref[...])
    o_ref[...] = _rms(out, gf_ref[...]) if final else out


def _moe(x2, gain, w_router, w_gate, w_up, w_down, final_gain):
    n, d = x2.shape
    n_exp, _, ff = w_gate.shape
    final = final_gain is not None
    gf = (final_gain if final else gain).reshape(1, d)
    wr = jnp.zeros((d, LANES), F32).at[:, :n_exp].set(w_router)

    tr = min(ROUTER_BLOCK, n)
    tri = (jnp.arange(tr)[:, None] > jnp.arange(tr)[None, :]).astype(BF16)
    hn, route, counts = pl.pallas_call(
        functools.partial(_router_kernel, n_exp),
        grid=(n // tr,),
        in_specs=[pl.BlockSpec((tr, d), lambda i: (i, 0)),
                  pl.BlockSpec((1, d), lambda i: (0, 0)),
                  pl.BlockSpec((d, LANES), lambda i: (0, 0)),
                  pl.BlockSpec((tr, tr), lambda i: (0, 0))],
        out_specs=[pl.BlockSpec((tr, d), lambda i: (i, 0)),
                   pl.BlockSpec((tr, LANES), lambda i: (i, 0)),
                   pl.BlockSpec((1, LANES), lambda i: (0, 0))],
        out_shape=[jax.ShapeDtypeStruct((n, d), F32),
                   jax.ShapeDtypeStruct((n, LANES), F32),
                   jax.ShapeDtypeStruct((1, LANES), F32)],
        scratch_shapes=[pltpu.VMEM((1, LANES), F32)],
        compiler_params=_params("arbitrary"),
        name="moe_router",
    )(x2, gain.reshape(1, d), wr, tri)

    rb = min(MOE_ROW_BLOCK, TOP_K * n)
    n_slots = TOP_K * n + n_exp * rb
    n_blocks = n_slots // rb
    cnt = counts[0, :n_exp].astype(jnp.int32)
    padded = ((cnt + rb - 1) // rb) * rb
    ends = jnp.cumsum(padded)
    starts = ends - padded
    as_int = lambda k: route[:, k].astype(jnp.int32)
    slot1 = starts[as_int(_R_E1)] + as_int(_R_RANK1)
    slot2 = starts[as_int(_R_E2)] + as_int(_R_RANK2)
    block_row = jnp.arange(n_blocks, dtype=jnp.int32) * rb
    block_exp = jnp.minimum(jnp.sum(block_row[:, None] >= ends[None, :], axis=1), n_exp - 1).astype(jnp.int32)
    block_valid = jnp.clip(cnt[block_exp] - (block_row - starts[block_exp]), 0, rb).astype(jnp.int32)

    tdp = min(DISPATCH_BLOCK, n)
    fill = jnp.concatenate([starts + cnt, ends, ends[-1:] // rb]).astype(jnp.int32)
    xs = pl.pallas_call(
        functools.partial(_dispatch_kernel, n_exp, rb),
        grid_spec=pltpu.PrefetchScalarGridSpec(
            num_scalar_prefetch=3,
            grid=(n // tdp,),
            in_specs=[pl.BlockSpec((tdp, d), lambda i, s1, s2, fl: (i, 0))],
            out_specs=pl.BlockSpec(memory_space=pl.ANY),
            scratch_shapes=[pltpu.VMEM((rb, d), F32), pltpu.SemaphoreType.DMA, pltpu.SemaphoreType.DMA]),
        out_shape=jax.ShapeDtypeStruct((n_slots, d), F32),
        compiler_params=_params("arbitrary"),
        name="moe_dispatch",
    )(slot1, slot2, fill, hn)

    tf = min(FFN_COL_BLOCK, ff)
    ys = pl.pallas_call(
        _expert_ffn_kernel,
        grid_spec=pltpu.PrefetchScalarGridSpec(
            num_scalar_prefetch=2,
            grid=(n_blocks, ff // tf),
            in_specs=[pl.BlockSpec((rb, d), lambda i, f, be, bv: (i, 0)),
                      pl.BlockSpec((1, d, tf), lambda i, f, be, bv: (be[i], 0, f)),
                      pl.BlockSpec((1, d, tf), lambda i, f, be, bv: (be[i], 0, f)),
                      pl.BlockSpec((1, tf, d), lambda i, f, be, bv: (be[i], f, 0))],
            out_specs=pl.BlockSpec((rb, d), lambda i, f, be, bv: (i, 0)),
            scratch_shapes=[pltpu.VMEM((rb, d), BF16), pltpu.VMEM((rb, d), F32)]),
        out_shape=jax.ShapeDtypeStruct((n_slots, d), F32),
        compiler_params=_params("parallel", "arbitrary"),
        name="moe_experts",
    )(block_exp, block_valid, xs, w_gate, w_up, w_down)

    tc = min(COMBINE_BLOCK, n)
    return pl.pallas_call(
        functools.partial(_combine_kernel, final),
        grid_spec=pltpu.PrefetchScalarGridSpec(
            num_scalar_prefetch=2,
            grid=(n // tc,),
            in_specs=[pl.BlockSpec((tc, d), lambda i, s1, s2: (i, 0)),
                      pl.BlockSpec((tc, LANES), lambda i, s1, s2: (i, 0)),
                      pl.BlockSpec((1, d), lambda i, s1, s2: (0, 0)),
                      pl.BlockSpec(memory_space=pl.ANY)],
            out_specs=pl.BlockSpec((tc, d), lambda i, s1, s2: (i, 0)),
            scratch_shapes=[pltpu.VMEM((tc, d), F32), pltpu.VMEM((tc, d), F32), pltpu.SemaphoreType.DMA]),
        out_shape=jax.ShapeDtypeStruct((n, d), F32),
        compiler_params=_params("arbitrary"),
        name="moe_combine",
    )(slot1, slot2, x2, route, gf, ys)


def _arrange_in_proj(w_in, s5_w, ssd_w, ssd_cdim, ssd_h, gdn_cdim, gdn_w, gdn_h):
    sizes = (s5_w, ssd_w, ssd_cdim, ssd_h, gdn_cdim, gdn_w, gdn_h, gdn_h)
    offs = [int(o) for o in np.cumsum((0,) + sizes)]
    seg = lambda i: w_in[:, offs[i]:offs[i + 1]]
    small = jnp.concatenate([seg(3), seg(6), seg(7)], axis=1)
    small = jnp.pad(small, ((0, 0), (0, LANES - small.shape[1])))
    cuts = ((offs[0], offs[3]), (offs[4], offs[5]), (offs[5], offs[6]))
    return small, (s5_w, ssd_w, ssd_cdim, gdn_cdim, gdn_w, LANES), cuts


def kernel(x, norm_mix, w_in, w_out, s5_a_re, s5_a_im, s5_b_re, s5_b_im, s5_c_re, s5_c_im, s5_d, s5_log_step, s5_w_glu, s5_norm, ssd_conv_w, ssd_conv_b, ssd_dt_bias, ssd_a_log, ssd_d, ssd_norm, gdn_conv_w, gdn_a_log, gdn_dt_bias, gdn_norm, norm_ffn, ff_w_gate, ff_w_up, ff_w_down, moe_router, moe_w_gate, moe_w_up, moe_w_down, norm_final):
    bsz, seq, d = x.shape
    depth = norm_mix.shape[0]
    s5_w = s5_w_glu.shape[-1]
    ssd_w, ssd_h, ssd_cdim = ssd_norm.shape[-1], ssd_a_log.shape[-1], ssd_conv_w.shape[-1]
    gdn_h, gdn_cdim = gdn_a_log.shape[-1], gdn_conv_w.shape[-1]
    gdn_w = gdn_h * GDN_HEADDIM
    xr = x.reshape(bsz * seq, d)
    for layer in range(depth):
        w_small, sizes, cuts = _arrange_in_proj(w_in[layer], s5_w, ssd_w, ssd_cdim, ssd_h, gdn_cdim, gdn_w, gdn_h)
        u_s5, z_ssd, xbc, qkv, z_gdn, small = _inproj(xr, norm_mix[layer], w_in, layer, w_small, sizes, cuts)
        s5_ops = _s5_operands(s5_a_re[layer], s5_a_im[layer], s5_b_re[layer], s5_b_im[layer], s5_c_re[layer],
                              s5_c_im[layer], s5_d[layer], s5_log_step[layer])
        y_s5 = _s5_scan(u_s5, s5_ops, bsz, seq)
        y_ssd = _ssd(z_ssd, xbc, small, ssd_conv_w[layer], ssd_conv_b[layer], ssd_dt_bias[layer],
                     ssd_a_log[layer], ssd_d[layer], ssd_norm[layer], bsz, seq)
        y_gdn = _gdn(qkv, z_gdn, small, gdn_conv_w[layer], gdn_a_log[layer], gdn_dt_bias[layer],
                     gdn_norm[layer], bsz, seq)
        xr = _outproj(xr, y_s5, y_ssd, y_gdn, s5_w_glu[layer], s5_norm[layer], w_out[layer])
        final_gain = norm_final if layer == depth - 1 else None
        i = layer // 2
        if layer % 2 == 0:
            xr = _ffn(xr, norm_ffn[layer], ff_w_gate[i], ff_w_up[i], ff_w_down[i], final_gain)
        else:
            xr = _moe(xr, norm_ffn[layer], moe_router[i], moe_w_gate[i], moe_w_up[i], moe_w_down[i], final_gain)
    return xr.reshape(bsz, seq, d)
```

```python
import functools
import math

import jax
import jax.numpy as jnp
import numpy as np
from jax import lax
from jax.experimental import pallas as pl
from jax.experimental.pallas import tpu as pltpu

F32 = jnp.float32
BF16 = jnp.bfloat16
EPS = 1e-6

S5_GROUP = 16
S5_STATE = 64
SSD_HEADDIM = 64
SSD_NGROUPS = 2
SSD_STATE = 128
GDN_HEADDIM = 64
CONV_K = 4
TOP_K = 2

LANES = 128
SUBLANES = 8
VMEM_LIMIT_BYTES = 56 * 1024 * 1024

SSD_CHUNK = 128
GDN_CHUNK = 64
GDN_INV_BLOCK = 16
GDN_GROUP = 256
GDN_BLOCK = 512
SEQ_BLOCK = 512
SSD_BLOCK = 1024
ROW_BLOCK = 512
FFN_ROW_BLOCK = 1024
FFN_COL_BLOCK = 512
ROUTER_BLOCK = 1024
DISPATCH_BLOCK = 1024
MOE_ROW_BLOCK = 1024
COMBINE_BLOCK = 512
DMA_ISSUE_UNROLL = 8


def _params(*semantics):
    return pltpu.CompilerParams(dimension_semantics=semantics, vmem_limit_bytes=VMEM_LIMIT_BYTES)


def _dot(a, b):
    return jnp.dot(a.astype(BF16), b.astype(BF16), preferred_element_type=F32)


def _mm(a, b):
    return jnp.dot(a, b, preferred_element_type=F32)


def _dot_nt(a, b):
    return lax.dot_general(a.astype(BF16), b.astype(BF16), (((1,), (1,)), ((), ())),
                           preferred_element_type=F32)


def _dot_tn(a, b):
    return lax.dot_general(a.astype(BF16), b.astype(BF16), (((0,), (0,)), ((), ())),
                           preferred_element_type=F32)


def _split3(a):
    hi = a.astype(BF16)
    r1 = a - hi.astype(F32)
    mid = r1.astype(BF16)
    lo = (r1 - mid.astype(F32)).astype(BF16)
    return hi, mid, lo


def _dot_sel(a, sel):
    hi, mid, lo = _split3(a)
    out = jnp.dot(hi, sel, preferred_element_type=F32)
    out += jnp.dot(mid, sel, preferred_element_type=F32)
    out += jnp.dot(lo, sel, preferred_element_type=F32)
    return out


def _sel_dot(sel, a):
    hi, mid, lo = _split3(a)
    out = jnp.dot(sel, hi, preferred_element_type=F32)
    out += jnp.dot(sel, mid, preferred_element_type=F32)
    out += jnp.dot(sel, lo, preferred_element_type=F32)
    return out


def _sigmoid(x):
    return 1.0 / (1.0 + jnp.exp(-x))


def _silu(x):
    return x * _sigmoid(x)


def _softplus(x):
    return jnp.maximum(x, 0.0) + jnp.log(1.0 + jnp.exp(-jnp.abs(x)))


def _gelu_tanh(x):
    c = math.sqrt(2.0 / math.pi)
    return 0.5 * x * (1.0 + jnp.tanh(c * (x + 0.044715 * (x * x * x))))


def _rms(x, w):
    return x * lax.rsqrt(jnp.mean(x * x, axis=-1, keepdims=True) + EPS) * w


def _iota2(shape, axis):
    return lax.broadcasted_iota(jnp.int32, shape, axis)


def _inproj_kernel(layer, sizes, cuts, x_ref, g_ref, w_hbm, small_ref, *rest):
    out_refs = rest[:len(sizes)]
    raw_ref, w_ref = rest[len(sizes):]

    @pl.when(pl.program_id(0) == 0)
    def _():
        pltpu.sync_copy(w_hbm.at[layer], raw_ref)
        ncols = raw_ref.shape[1]
        off = 0
        for lo, hi in cuts:
            a = (lo // LANES) * LANES
            b = min(-(-hi // LANES) * LANES, ncols)
            w_ref[:, off:off + hi - lo] = raw_ref[:, a:b][:, lo - a:hi - a].astype(BF16)
            off += hi - lo
        w_ref[:, off:off + LANES] = small_ref[...].astype(BF16)

    h = _rms(x_ref[...], g_ref[...])
    p = jnp.dot(h.astype(BF16), w_ref[...], preferred_element_type=F32)
    off = 0
    for o_ref, sz in zip(out_refs, sizes):
        o_ref[...] = p[:, off:off + sz]
        off += sz


def _inproj(x2, gain, w_all, layer, small, sizes, cuts):
    n, d = x2.shape
    tm = min(ROW_BLOCK, n)
    total = sum(sizes)
    return pl.pallas_call(
        functools.partial(_inproj_kernel, layer, sizes, cuts),
        grid=(n // tm,),
        in_specs=[pl.BlockSpec((tm, d), lambda i: (i, 0)),
                  pl.BlockSpec((1, d), lambda i: (0, 0)),
                  pl.BlockSpec(memory_space=pl.ANY),
                  pl.BlockSpec((d, LANES), lambda i: (0, 0))],
        out_specs=[pl.BlockSpec((tm, sz), lambda i: (i, 0)) for sz in sizes],
        out_shape=[jax.ShapeDtypeStruct((n, sz), F32) for sz in sizes],
        scratch_shapes=[pltpu.VMEM(w_all.shape[1:], F32), pltpu.VMEM((d, total), BF16)],
        compiler_params=_params("arbitrary"),
        name="inproj",
    )(x2, gain.reshape(1, d), w_all, small)


def _s5_operands(a_re, a_im, b_re, b_im, c_re, c_im, d_skip, log_step):
    g, p = a_re.shape
    h = b_re.shape[-1]
    step = jnp.exp(log_step)[:, None]
    mag = jnp.exp(a_re * step)
    ang = a_im * step
    lb_re, lb_im = mag * jnp.cos(ang), mag * jnp.sin(ang)
    den = a_re * a_re + a_im * a_im
    f_re = ((lb_re - 1.0) * a_re + lb_im * a_im) / den
    f_im = (lb_im * a_re - (lb_re - 1.0) * a_im) / den
    bb_re = f_re[..., None] * b_re - f_im[..., None] * b_im
    bb_im = f_re[..., None] * b_im + f_im[..., None] * b_re

    eye = jnp.eye(g, dtype=F32)
    to_state = lambda m: (m.transpose(0, 2, 1)[:, :, None, :] * eye[:, None, :, None]).reshape(g * h, g * p)
    to_out = lambda m: (m.transpose(0, 2, 1)[:, :, None, :] * eye[:, None, :, None]).reshape(g * p, g * h)
    bbd = jnp.concatenate([to_state(bb_re), to_state(bb_im)], axis=1)
    cbd = jnp.concatenate([to_out(c_re), -to_out(c_im)], axis=0)
    e = jnp.arange(1, SUBLANES + 1, dtype=F32)[:, None, None]
    m = jnp.exp(a_re * step * e)
    lam = jnp.concatenate([(m * jnp.cos(ang * e)).reshape(SUBLANES, g * p),
                           (m * jnp.sin(ang * e)).reshape(SUBLANES, g * p)], axis=1)
    return bbd, cbd, lam, d_skip.reshape(1, g * h)


def _s5_kernel(u_ref, bbd_ref, cbd_ref, lam_ref, d_ref, y_ref, s_ref, carry_ref):
    n_slab, t, _ = s_ref.shape
    n_pair = n_slab // 2
    tile = SUBLANES
    pairs = range(n_pair)
    lanes = lambda j: slice(j * LANES, (j + 1) * LANES)

    @pl.when(pl.program_id(1) == 0)
    def _():
        carry_ref[...] = jnp.zeros(carry_ref.shape, F32)

    u = u_ref[...]
    bu = jnp.dot(u.astype(BF16), bbd_ref[...], preferred_element_type=F32)
    for j in range(n_slab):
        s_ref[j] = bu[:, lanes(j)]
    lam = lam_ref[...]
    lr = [lam[:, lanes(j)] for j in pairs]
    li = [lam[:, lanes(n_pair + j)] for j in pairs]

    def rows(j, r):
        return s_ref.at[j, pl.ds(r, t // tile, stride=tile), :]

    pr = [rows(j, 0)[...] for j in pairs]
    pi = [rows(n_pair + j, 0)[...] for j in pairs]
    for r in range(1, tile):
        nr = [rows(j, r)[...] + lr[j][0:1] * pr[j] - li[j][0:1] * pi[j] for j in pairs]
        ni = [rows(n_pair + j, r)[...] + lr[j][0:1] * pi[j] + li[j][0:1] * pr[j] for j in pairs]
        for j in pairs:
            rows(j, r)[...] = nr[j]
            rows(n_pair + j, r)[...] = ni[j]
        pr, pi = nr, ni

    def tile_body(k, carry):
        cr, ci = carry
        r0 = pl.multiple_of(k * tile, tile)
        tr = [s_ref[j, pl.ds(r0, tile), :] + lr[j] * cr[j] - li[j] * ci[j] for j in pairs]
        ti = [s_ref[n_pair + j, pl.ds(r0, tile), :] + lr[j] * ci[j] + li[j] * cr[j] for j in pairs]
        for j in pairs:
            s_ref[j, pl.ds(r0, tile), :] = tr[j]
            s_ref[n_pair + j, pl.ds(r0, tile), :] = ti[j]
        last = lambda v: jnp.broadcast_to(v[tile - 1:tile], v.shape)
        return [last(v) for v in tr], [last(v) for v in ti]

    cr, ci = lax.fori_loop(0, t // tile, tile_body,
                           ([carry_ref[j] for j in pairs], [carry_ref[n_pair + j] for j in pairs]))
    for j in pairs:
        carry_ref[j] = cr[j]
        carry_ref[n_pair + j] = ci[j]
    s_all = jnp.concatenate([s_ref[j] for j in range(n_slab)], axis=1)
    y_ref[...] = jnp.dot(s_all.astype(BF16), cbd_ref[...], preferred_element_type=F32) + d_ref[...] * u


def _s5_scan(u, operands, bsz, seq):
    bbd, cbd, lam, d = operands
    n, width = u.shape
    s2 = bbd.shape[1]
    t = min(SEQ_BLOCK, seq)
    nblk = seq // t
    tok = lambda b, c: (b * nblk + c, 0)
    const = lambda b, c: (0, 0)
    return pl.pallas_call(
        _s5_kernel,
        grid=(bsz, nblk),
        in_specs=[pl.BlockSpec((t, width), tok),
                  pl.BlockSpec((width, s2), const),
                  pl.BlockSpec((s2, width), const),
                  pl.BlockSpec((SUBLANES, s2), const),
                  pl.BlockSpec((1, width), const)],
        out_specs=pl.BlockSpec((t, width), tok),
        out_shape=jax.ShapeDtypeStruct((n, width), F32),
        scratch_shapes=[pltpu.VMEM((s2 // LANES, t, LANES), F32), pltpu.VMEM((s2 // LANES, SUBLANES, LANES), F32)],
        compiler_params=_params("parallel", "arbitrary"),
        name="s5_scan",
    )(u, bbd.astype(BF16), cbd.astype(BF16), lam, d)


def _causal_conv_block(x_ref, w_ref, pad_ref, first):
    t = x_ref.shape[0]
    halo = SUBLANES

    @pl.when(first)
    def _():
        pad_ref[0:halo, :] = jnp.zeros((halo, pad_ref.shape[1]), F32)

    @pl.when(jnp.logical_not(first))
    def _():
        pad_ref[0:halo, :] = pad_ref[t:t + halo, :]

    pad_ref[halo:halo + t, :] = x_ref[...]
    acc = w_ref[CONV_K - 1:CONV_K, :] * pad_ref[halo:halo + t, :]
    for j in range(1, CONV_K):
        acc += w_ref[CONV_K - 1 - j:CONV_K - j, :] * pad_ref[halo - j:halo - j + t, :]
    return acc


def _ssd_kernel(heads, z_ref, xbc_ref, sm_ref, cw_ref, cb_ref, dtb_ref, a_ref, dexp_ref, nw_ref, ex_ref,
                o_ref, pad_ref, xc_ref, dt_ref, st_ref):
    t = xbc_ref.shape[0]
    q = min(SSD_CHUNK, t)
    width = heads * SSD_HEADDIM
    gw = SSD_NGROUPS * SSD_STATE
    first = pl.program_id(1) == 0

    @pl.when(first)
    def _():
        st_ref[...] = jnp.zeros(st_ref.shape, F32)

    xc_ref[...] = _silu(_causal_conv_block(xbc_ref, cw_ref, pad_ref, first) + cb_ref[...])
    dt_ref[...] = _softplus(sm_ref[...] + dtb_ref[...])

    row = _iota2((q, q), 0)
    col = _iota2((q, q), 1)
    causal = row >= col
    tril = jnp.where(causal, 1.0, 0.0).astype(BF16)
    lane = _iota2((1, width), 1)
    rep = heads // SSD_NGROUPS

    def chunk_body(ci, carry):
        r0 = ci * q
        xs = xc_ref[pl.ds(r0, q), 0:width]
        bm = xc_ref[pl.ds(r0, q), width:width + gw]
        cm = xc_ref[pl.ds(r0, q), width + gw:width + 2 * gw]
        dt = dt_ref[pl.ds(r0, q), :]
        acs = _sel_dot(tril, dt * a_ref[...])
        acs_t = acs.T
        xdt = xs * _dot_sel(dt, ex_ref[...])
        hs = range(heads)
        bgs = [bm[:, gi * SSD_STATE:(gi + 1) * SSD_STATE] for gi in range(SSD_NGROUPS)]
        cgs = [cm[:, gi * SSD_STATE:(gi + 1) * SSD_STATE].astype(BF16) for gi in range(SSD_NGROUPS)]
        cbs = [_dot_nt(cgs[gi], bgs[gi]) for gi in range(SSD_NGROUPS)]
        xdb = xdt.astype(BF16)
        xh = [xdb[:, hh * SSD_HEADDIM:(hh + 1) * SSD_HEADDIM] for hh in hs]
        acol = [acs[:, hh:hh + 1] for hh in hs]
        alast = [acs[q - 1:q, hh:hh + 1] for hh in hs]
        decay = [jnp.where(causal, jnp.exp(jnp.where(causal, acol[hh] - acs_t[hh:hh + 1, :], 0.0)), 0.0)
                 for hh in hs]
        state = [st_ref[hh] for hh in hs]
        y_diag = [_mm((cbs[hh // rep] * decay[hh]).astype(BF16), xh[hh]) for hh in hs]
        y_off = [_mm(cgs[hh // rep], state[hh].astype(BF16)) * jnp.exp(acol[hh]) for hh in hs]
        upd = [_dot_tn(bgs[hh // rep] * jnp.exp(alast[hh] - acol[hh]), xh[hh]) for hh in hs]
        for hh in hs:
            st_ref[hh] = state[hh] * jnp.exp(alast[hh]) + upd[hh]
        y = jnp.concatenate([y_diag[hh] + y_off[hh] for hh in hs], axis=1) + dexp_ref[...] * xs
        y = y * _silu(z_ref[pl.ds(r0, q), :])
        gsz = width // SSD_NGROUPS
        y2 = y * y
        out = jnp.zeros_like(y)
        for gi in range(SSD_NGROUPS):
            m = (lane >= gi * gsz) & (lane < (gi + 1) * gsz)
            ms = jnp.sum(jnp.where(m, y2, 0.0), axis=-1, keepdims=True) / gsz
            out = jnp.where(m, y * lax.rsqrt(ms + EPS), out)
        o_ref[pl.ds(r0, q), :] = out * nw_ref[...]
        return carry

    for ci in range(t // q):
        chunk_body(ci, 0)


def _ssd(z, xbc, small, conv_w, conv_b, dt_bias, a_log, d_skip, norm_w, bsz, seq):
    n, width = z.shape
    heads = a_log.shape[0]
    cdim = xbc.shape[1]
    t = min(SSD_BLOCK, seq)
    nblk = seq // t
    pad = lambda v: jnp.zeros((1, LANES), F32).at[0, :heads].set(v)
    expand = (jnp.arange(LANES)[:, None] == (jnp.arange(width)[None, :] // SSD_HEADDIM)).astype(BF16)
    tok = lambda b, c: (b * nblk + c, 0)
    const = lambda b, c: (0, 0)
    return pl.pallas_call(
        functools.partial(_ssd_kernel, heads),
        grid=(bsz, nblk),
        in_specs=[pl.BlockSpec((t, width), tok),
                  pl.BlockSpec((t, cdim), tok),
                  pl.BlockSpec((t, LANES), tok),
                  pl.BlockSpec((CONV_K, cdim), const),
                  pl.BlockSpec((1, cdim), const),
                  pl.BlockSpec((1, LANES), const),
                  pl.BlockSpec((1, LANES), const),
                  pl.BlockSpec((1, width), const),
                  pl.BlockSpec((1, width), const),
                  pl.BlockSpec((LANES, width), const)],
        out_specs=pl.BlockSpec((t, width), tok),
        out_shape=jax.ShapeDtypeStruct((n, width), F32),
        scratch_shapes=[pltpu.VMEM((SUBLANES + t, cdim), F32),
                        pltpu.VMEM((t, cdim), F32),
                        pltpu.VMEM((t, LANES), F32),
                        pltpu.VMEM((heads, SSD_STATE, SSD_HEADDIM), F32)],
        compiler_params=_params("parallel", "arbitrary"),
        name="ssd",
    )(z, xbc, small, conv_w, conv_b.reshape(1, cdim), pad(dt_bias), pad(-jnp.exp(a_log)),
      jnp.repeat(d_skip, SSD_HEADDIM).reshape(1, width), norm_w.reshape(1, width), expand)


def _unit_lower_inverses(lms, eye, blk, chunk):
    lds = [jnp.where(blk, lm, 0.0) for lm in lms]
    offs = [(lm - ld).astype(BF16) for lm, ld in zip(lms, lds)]
    dinv = [eye - ld for ld in lds]
    pw = [ld.astype(BF16) for ld in lds]
    yield
    span = 2
    while span < GDN_INV_BLOCK:
        sq = [_mm(p, p) for p in pw]
        yield
        dinv = [_mm(d.astype(BF16), (eye + s).astype(BF16)) for d, s in zip(dinv, sq)]
        pw = [s.astype(BF16) for s in sq]
        yield
        span *= 2
    dinv = [d.astype(BF16) for d in dinv]
    ms = [_mm(d, o) for d, o in zip(dinv, offs)]
    yield
    out = [eye - m for m in ms]
    pw = [m.astype(BF16) for m in ms]
    span = 2
    while span < chunk // GDN_INV_BLOCK:
        sq = [_mm(p, p) for p in pw]
        yield
        out = [_mm(o.astype(BF16), (eye + s).astype(BF16)) for o, s in zip(out, sq)]
        pw = [s.astype(BF16) for s in sq]
        yield
        span *= 2
    return [_mm(o.astype(BF16), d) for o, d in zip(out, dinv)]


def _interleave(gens):
    results = [None] * len(gens)
    live = list(range(len(gens)))
    while live:
        for i in list(live):
            try:
                next(gens[i])
            except StopIteration as done:
                results[i] = done.value
                live.remove(i)
    return results


def _gdn_kernel(heads, qkv_ref, z_ref, sm_ref, cw_ref, dtb_ref, a_ref, nw_ref, exb_ref, exg_ref, ones_ref,
                o_ref, pad_ref, st_ref):
    t = qkv_ref.shape[0]
    c = min(GDN_CHUNK, t)
    hd = GDN_HEADDIM
    width = heads * hd
    hs = range(heads)
    sl = [slice(hh * hd, (hh + 1) * hd) for hh in hs]
    first = pl.program_id(1) == 0

    @pl.when(first)
    def _():
        st_ref[...] = jnp.zeros(st_ref.shape, F32)

    def seg_sum(v):
        hi = v.astype(BF16)
        lo = (v - hi.astype(F32)).astype(BF16)
        return _mm(hi, ones_ref[...]) + _mm(lo, ones_ref[...])

    xc = _silu(_causal_conv_block(qkv_ref, cw_ref, pad_ref, first))
    qf = xc[:, 0:width]
    kf = xc[:, width:2 * width]
    vf = xc[:, 2 * width:3 * width]
    qf = qf * lax.rsqrt(seg_sum(qf * qf) + EPS) * (hd ** -0.5)
    kf = kf * lax.rsqrt(seg_sum(kf * kf) + EPS)
    sm = sm_ref[...]
    beta = _dot_sel(_sigmoid(sm), exb_ref[...])
    g = a_ref[...] * _softplus(sm + dtb_ref[...])

    gs = min(GDN_GROUP, t)
    row = _iota2((gs, gs), 0)
    col = _iota2((gs, gs), 1)
    same = (row // c) == (col // c)
    causal = same & (row >= col)
    strict = same & (row > col)
    eye = jnp.where(row == col, 1.0, 0.0)
    blk = (row // GDN_INV_BLOCK) == (col // GDN_INV_BLOCK)
    tril_sel = jnp.where(causal, 1.0, 0.0).astype(BF16)
    same_sel = jnp.where(same, 1.0, 0.0).astype(BF16)
    nt = (((1,), (1,)), ((), ()))

    def gates(gi):
        rg = slice(gi * gs, (gi + 1) * gs)
        gcs = _sel_dot(tril_sel, g[rg])
        gtot = _sel_dot(same_sel, g[rg])
        gcs_t = gcs.T
        yield
        gw = _dot_sel(gcs, exg_ref[...])
        gtw = _dot_sel(gtot, exg_ref[...])
        gexp = jnp.exp(gw)
        kb = kf[rg] * beta[rg]
        yield
        decay = []
        for hh in hs:
            lg = 2 * heads + hh
            diff = jnp.where(causal, gcs[:, lg:lg + 1] - gcs_t[lg:lg + 1, :], 0.0)
            decay.append(jnp.where(causal, jnp.exp(diff), 0.0))
            yield
        return dict(kb=kb, vb=vf[rg] * beta[rg], kcd_in=kb * gexp, qd=(qf[rg] * gexp).astype(BF16),
                    kend=(kf[rg] * jnp.exp(gtw - gw)).astype(BF16), cdec=jnp.exp(gtw), decay=decay,
                    kfb=kf[rg].astype(BF16), qfb=qf[rg].astype(BF16))

    def solve(p):
        kbb = p["kb"].astype(BF16)
        kk = [lax.dot_general(kbb[:, sl[hh]], p["kfb"][:, sl[hh]], nt, preferred_element_type=F32) for hh in hs]
        qk = [lax.dot_general(p["qfb"][:, sl[hh]], p["kfb"][:, sl[hh]], nt, preferred_element_type=F32)
              for hh in hs]
        yield
        lms = [jnp.where(strict, kk[hh] * p["decay"][hh], 0.0) for hh in hs]
        qk = [(qk[hh] * p["decay"][hh]).astype(BF16) for hh in hs]
        tinv = yield from _unit_lower_inverses(lms, eye, blk, c)
        yield
        rhs = [jnp.concatenate([p["vb"][:, sl[hh]], p["kcd_in"][:, sl[hh]]], axis=1).astype(BF16) for hh in hs]
        return qk, [_mm(tinv[hh].astype(BF16), rhs[hh]) for hh in hs]

    def recur(state, p, qk, vk):
        outs = []
        for ci in range(gs // c):
            rs = slice(ci * c, (ci + 1) * c)
            lhs = [jnp.concatenate([vk[hh][rs, hd:2 * hd].astype(BF16), p["qd"][rs, sl[hh]]], axis=0)
                   for hh in hs]
            sb = [s.astype(BF16) for s in state]
            both = [_mm(lhs[hh], sb[hh]) for hh in hs]
            yield
            v_new = [(vk[hh][rs, 0:hd] - both[hh][0:c]).astype(BF16) for hh in hs]
            o = [both[hh][c:2 * c] + _mm(qk[hh][rs, ci * c:(ci + 1) * c], v_new[hh]) for hh in hs]
            upd = [lax.dot_general(p["kend"][rs, sl[hh]], v_new[hh], (((0,), (0,)), ((), ())),
                                   preferred_element_type=F32) for hh in hs]
            state = [state[hh] * p["cdec"][ci * c:ci * c + 1, sl[hh]] + upd[hh] for hh in hs]
            outs.append(jnp.concatenate(o, axis=1))
            yield
        return state, outs

    n_groups = t // gs
    parts, solved, outs = {}, {}, []
    state = [st_ref[hh] for hh in hs]
    for k in range(n_groups + 2):
        jobs = {}
        if k < n_groups:
            jobs["gates"] = gates(k)
        if 0 <= k - 1 < n_groups:
            jobs["solve"] = solve(parts[k - 1])
        if 0 <= k - 2 < n_groups:
            jobs["recur"] = recur(state, parts[k - 2], *solved[k - 2])
        done = dict(zip(jobs, _interleave(list(jobs.values()))))
        if "gates" in done:
            parts[k] = done["gates"]
        if "solve" in done:
            solved[k - 1] = done["solve"]
        if "recur" in done:
            state, o_g = done["recur"]
            outs += o_g
    for hh in hs:
        st_ref[hh] = state[hh]
    o = jnp.concatenate(outs, axis=0)
    o = o * lax.rsqrt(seg_sum(o * o) / hd + EPS) * nw_ref[...]
    o_ref[...] = o * _silu(z_ref[...])


def _gdn(qkv, z, small, conv_w, a_log, dt_bias, norm_w, bsz, seq):
    n, width = z.shape
    heads = a_log.shape[0]
    cdim = qkv.shape[1]
    t = min(GDN_BLOCK, seq)
    nblk = seq // t
    pad = lambda v, off: jnp.zeros((1, LANES), F32).at[0, off:off + heads].set(v)
    head_of = jnp.arange(width)[None, :] // GDN_HEADDIM
    lanes = jnp.arange(LANES)[:, None]
    exb = (lanes == heads + head_of).astype(BF16)
    exg = (lanes == 2 * heads + head_of).astype(BF16)
    ones = (head_of.T == head_of).astype(BF16)
    tok = lambda b, c: (b * nblk + c, 0)
    const = lambda b, c: (0, 0)
    return pl.pallas_call(
        functools.partial(_gdn_kernel, heads),
        grid=(bsz, nblk),
        in_specs=[pl.BlockSpec((t, cdim), tok),
                  pl.BlockSpec((t, width), tok),
                  pl.BlockSpec((t, LANES), tok),
                  pl.BlockSpec((CONV_K, cdim), const),
                  pl.BlockSpec((1, LANES), const),
                  pl.BlockSpec((1, LANES), const),
                  pl.BlockSpec((1, width), const),
                  pl.BlockSpec((LANES, width), const),
                  pl.BlockSpec((LANES, width), const),
                  pl.BlockSpec((width, width), const)],
        out_specs=pl.BlockSpec((t, width), tok),
        out_shape=jax.ShapeDtypeStruct((n, width), F32),
        scratch_shapes=[pltpu.VMEM((SUBLANES + t, cdim), F32),
                        pltpu.VMEM((heads, GDN_HEADDIM, GDN_HEADDIM), F32)],
        compiler_params=_params("parallel", "arbitrary"),
        name="gdn",
    )(qkv, z, small, conv_w, pad(dt_bias, 2 * heads), pad(-jnp.exp(a_log), 2 * heads),
      jnp.tile(norm_w, heads).reshape(1, width), exb, exg, ones)


def _outproj_kernel(x_ref, s5_ref, ssd_ref, gdn_ref, wglu_ref, s5n_ref, wout_ref, o_ref):
    y = _gelu_tanh(s5_ref[...])
    y = y * _sigmoid(jnp.dot(y.astype(BF16), wglu_ref[...], preferred_element_type=F32))
    y = _rms(y, s5n_ref[...])
    mix = jnp.concatenate([y, ssd_ref[...], gdn_ref[...]], axis=1)
    o_ref[...] = x_ref[...] + jnp.dot(mix.astype(BF16), wout_ref[...], preferred_element_type=F32)


def _outproj(x2, y_s5, y_ssd, y_gdn, w_glu, s5_norm, w_out):
    n, d = x2.shape
    tm = min(ROW_BLOCK, n)
    ws = [y_s5.shape[1], y_ssd.shape[1], y_gdn.shape[1]]
    row = lambda i: (i, 0)
    const = lambda i: (0, 0)
    return pl.pallas_call(
        _outproj_kernel,
        grid=(n // tm,),
        in_specs=[pl.BlockSpec((tm, d), row)] + [pl.BlockSpec((tm, w), row) for w in ws]
                 + [pl.BlockSpec((ws[0], ws[0]), const), pl.BlockSpec((1, ws[0]), const),
                    pl.BlockSpec((sum(ws), d), const)],
        out_specs=pl.BlockSpec((tm, d), row),
        out_shape=jax.ShapeDtypeStruct((n, d), F32),
        compiler_params=_params("parallel"),
        name="outproj",
    )(x2, y_s5, y_ssd, y_gdn, w_glu.astype(BF16), s5_norm.reshape(1, -1), w_out.astype(BF16))


def _ffn_kernel(final, x_ref, g_ref, wg_ref, wu_ref, wd_ref, gf_ref, o_ref, h_ref, acc_ref):
    f = pl.program_id(1)

    @pl.when(f == 0)
    def _():
        h_ref[...] = _rms(x_ref[...], g_ref[...]).astype(BF16)
        acc_ref[...] = jnp.zeros(acc_ref.shape, F32)

    h = h_ref[...]
    a = jnp.dot(h, wg_ref[...].astype(BF16), preferred_element_type=F32)
    u = jnp.dot(h, wu_ref[...].astype(BF16), preferred_element_type=F32)
    acc_ref[...] += jnp.dot((_silu(a) * u).astype(BF16), wd_ref[...].astype(BF16), preferred_element_type=F32)

    @pl.when(f == pl.num_programs(1) - 1)
    def _():
        out = x_ref[...] + acc_ref[...]
        o_ref[...] = _rms(out, gf_ref[...]) if final else out


def _ffn(x2, gain, w_gate, w_up, w_down, final_gain):
    n, d = x2.shape
    ff = w_gate.shape[1]
    tm = min(FFN_ROW_BLOCK, n)
    tf = min(FFN_COL_BLOCK, ff)
    final = final_gain is not None
    gf = (final_gain if final else gain).reshape(1, d)
    return pl.pallas_call(
        functools.partial(_ffn_kernel, final),
        grid=(n // tm, ff // tf),
        in_specs=[pl.BlockSpec((tm, d), lambda i, f: (i, 0)),
                  pl.BlockSpec((1, d), lambda i, f: (0, 0)),
                  pl.BlockSpec((d, tf), lambda i, f: (0, f)),
                  pl.BlockSpec((d, tf), lambda i, f: (0, f)),
                  pl.BlockSpec((tf, d), lambda i, f: (f, 0)),
                  pl.BlockSpec((1, d), lambda i, f: (0, 0))],
        out_specs=pl.BlockSpec((tm, d), lambda i, f: (i, 0)),
        out_shape=jax.ShapeDtypeStruct((n, d), F32),
        scratch_shapes=[pltpu.VMEM((tm, d), BF16), pltpu.VMEM((tm, d), F32)],
        compiler_params=_params("parallel", "arbitrary"),
        name="ffn",
    )(x2, gain.reshape(1, d), w_gate, w_up, w_down, gf)


_R_E1, _R_E2, _R_W1, _R_W2, _R_RANK1, _R_RANK2 = range(6)


def _router_kernel(n_exp, x_ref, g_ref, wr_ref, tri_ref, hn_ref, rt_ref, cnt_ref, base_ref):
    @pl.when(pl.program_id(0) == 0)
    def _():
        base_ref[...] = jnp.zeros(base_ref.shape, F32)

    hn = _rms(x_ref[...], g_ref[...])
    hn_ref[...] = hn
    h_hi = hn.astype(BF16)
    h_lo = (hn - h_hi.astype(F32)).astype(BF16)
    w_hi = wr_ref[...].astype(BF16)
    w_lo = (wr_ref[...] - w_hi.astype(F32)).astype(BF16)
    logits = _mm(h_hi, w_hi) + _mm(h_lo, w_hi) + _mm(h_hi, w_lo)
    lane = _iota2(logits.shape, 1)
    neg = jnp.float32(-jnp.inf)
    logits = jnp.where(lane < n_exp, logits, neg)
    m1 = jnp.max(logits, axis=-1, keepdims=True)
    i1 = jnp.min(jnp.where(logits == m1, lane, LANES), axis=-1, keepdims=True)
    rest = jnp.where(lane == i1, neg, logits)
    m2 = jnp.max(rest, axis=-1, keepdims=True)
    i2 = jnp.min(jnp.where(rest == m2, lane, LANES), axis=-1, keepdims=True)
    w1 = 1.0 / (1.0 + jnp.exp(m2 - m1))
    w2 = 1.0 / (1.0 + jnp.exp(m1 - m2))
    hit = (lane == i1) | (lane == i2)
    onehot = jnp.where(hit, 1.0, 0.0)
    before = jnp.dot(tri_ref[...], onehot.astype(BF16), preferred_element_type=F32) + base_ref[...]
    r1 = jnp.sum(jnp.where(lane == i1, before, 0.0), axis=-1, keepdims=True)
    r2 = jnp.sum(jnp.where(lane == i2, before, 0.0), axis=-1, keepdims=True)
    cols = {_R_E1: i1.astype(F32), _R_E2: i2.astype(F32), _R_W1: w1, _R_W2: w2, _R_RANK1: r1, _R_RANK2: r2}
    rt = jnp.zeros(logits.shape, F32)
    for k, v in cols.items():
        rt = jnp.where(lane == k, v, rt)
    rt_ref[...] = rt
    base_ref[...] += jnp.sum(onehot, axis=0, keepdims=True)
    cnt_ref[...] = base_ref[...]


def _row_copy(src_ref, src_row, dst_ref, dst_row, sem):
    return pltpu.make_async_copy(src_ref.at[pl.ds(src_row, 1)], dst_ref.at[pl.ds(dst_row, 1)], sem)


def _dispatch_kernel(n_exp, rb, s1_ref, s2_ref, fill_ref, hn_ref, xs_ref, zero_ref, sem, zsem):
    tm = hn_ref.shape[0]
    base = pl.program_id(0) * tm

    @pl.when(pl.program_id(0) == 0)
    def _():
        zero_ref[...] = jnp.zeros(zero_ref.shape, F32)
        for e in range(n_exp):
            lo, hi = fill_ref[e], fill_ref[n_exp + e]

            def start_row(r, carry):
                _row_copy(zero_ref, 0, xs_ref, r, zsem).start()
                return carry

            def wait_row(r, carry):
                _row_copy(zero_ref, 0, xs_ref, r, zsem).wait()
                return carry

            lax.fori_loop(lo, hi, start_row, 0)
            lax.fori_loop(lo, hi, wait_row, 0)

        def block_copy(b):
            return pltpu.make_async_copy(zero_ref, xs_ref.at[pl.ds(pl.multiple_of(b * rb, rb), rb)], zsem)

        def start_block(b, carry):
            block_copy(b).start()
            return carry

        def wait_block(b, carry):
            block_copy(b).wait()
            return carry

        n_blocks = xs_ref.shape[0] // rb
        lax.fori_loop(fill_ref[2 * n_exp], n_blocks, start_block, 0)
        lax.fori_loop(fill_ref[2 * n_exp], n_blocks, wait_block, 0)

    def issue(r, carry):
        _row_copy(hn_ref, r, xs_ref, s1_ref[base + r], sem).start()
        _row_copy(hn_ref, r, xs_ref, s2_ref[base + r], sem).start()
        return carry

    lax.fori_loop(0, tm, issue, 0, unroll=DMA_ISSUE_UNROLL)
    for _ in range(TOP_K):
        pltpu.make_async_copy(hn_ref, xs_ref.at[pl.ds(0, tm)], sem).wait()


def _expert_ffn_kernel(be_ref, bv_ref, xs_ref, wg_ref, wu_ref, wd_ref, ys_ref, h_ref, acc_ref):
    i = pl.program_id(0)
    f = pl.program_id(1)
    last = pl.num_programs(1) - 1
    valid = bv_ref[i]

    @pl.when(valid > 0)
    def _():
        @pl.when(f == 0)
        def _():
            row = _iota2((xs_ref.shape[0], 1), 0)
            h_ref[...] = jnp.where(row < valid, xs_ref[...], 0.0).astype(BF16)
            acc_ref[...] = jnp.zeros(acc_ref.shape, F32)

        h = h_ref[...]
        a = jnp.dot(h, wg_ref[0].astype(BF16), preferred_element_type=F32)
        u = jnp.dot(h, wu_ref[0].astype(BF16), preferred_element_type=F32)
        acc_ref[...] += jnp.dot((_silu(a) * u).astype(BF16), wd_ref[0].astype(BF16), preferred_element_type=F32)

        @pl.when(f == last)
        def _():
            ys_ref[...] = acc_ref[...]

    @pl.when((valid == 0) & (f == last))
    def _():
        ys_ref[...] = jnp.zeros(ys_ref.shape, F32)


def _combine_kernel(final, s1_ref, s2_ref, x_ref, rt_ref, gf_ref, ys_ref, o_ref, b1_ref, b2_ref, sem):
    tm = x_ref.shape[0]
    i = pl.program_id(0)
    cur = i % 2

    def issue(block, buf):
        def one(r, carry):
            _row_copy(ys_ref, s1_ref[block * tm + r], b1_ref.at[buf], r, sem.at[buf]).start()
            _row_copy(ys_ref, s2_ref[block * tm + r], b2_ref.at[buf], r, sem.at[buf]).start()
            return carry

        lax.fori_loop(0, tm, one, 0, unroll=DMA_ISSUE_UNROLL)

    @pl.when(i == 0)
    def _():
        issue(0, 0)

    @pl.when(i + 1 < pl.num_programs(0))
    def _():
        issue(i + 1, 1 - cur)

    for b_ref in (b1_ref, b2_ref):
        pltpu.make_async_copy(ys_ref.at[pl.ds(0, tm)], b_ref.at[cur], sem.at[cur]).wait()
    rt = rt_ref[...]
    out = (x_ref[...] + rt[:, _R_W1:_R_W1 + 1] * b1_ref[cur] + rt[:, _R_W2:_R_W2 + 1] * b2_ref[cur])
    o_ref[...] = _rms(out, gf_ref[...]) if final else out


def _moe(x2, gain, w_router, w_gate, w_up, w_down, final_gain):
    n, d = x2.shape
    n_exp, _, ff = w_gate.shape
    final = final_gain is not None
    gf = (final_gain if final else gain).reshape(1, d)
    wr = jnp.zeros((d, LANES), F32).at[:, :n_exp].set(w_router)

    tr = min(ROUTER_BLOCK, n)
    tri = (jnp.arange(tr)[:, None] > jnp.arange(tr)[None, :]).astype(BF16)
    hn, route, counts = pl.pallas_call(
        functools.partial(_router_kernel, n_exp),
        grid=(n // tr,),
        in_specs=[pl.BlockSpec((tr, d), lambda i: (i, 0)),
                  pl.BlockSpec((1, d), lambda i: (0, 0)),
                  pl.BlockSpec((d, LANES), lambda i: (0, 0)),
                  pl.BlockSpec((tr, tr), lambda i: (0, 0))],
        out_specs=[pl.BlockSpec((tr, d), lambda i: (i, 0)),
                   pl.BlockSpec((tr, LANES), lambda i: (i, 0)),
                   pl.BlockSpec((1, LANES), lambda i: (0, 0))],
        out_shape=[jax.ShapeDtypeStruct((n, d), F32),
                   jax.ShapeDtypeStruct((n, LANES), F32),
                   jax.ShapeDtypeStruct((1, LANES), F32)],
        scratch_shapes=[pltpu.VMEM((1, LANES), F32)],
        compiler_params=_params("arbitrary"),
        name="moe_router",
    )(x2, gain.reshape(1, d), wr, tri)

    rb = min(MOE_ROW_BLOCK, TOP_K * n)
    n_slots = TOP_K * n + n_exp * rb
    n_blocks = n_slots // rb
    cnt = counts[0, :n_exp].astype(jnp.int32)
    padded = ((cnt + rb - 1) // rb) * rb
    ends = jnp.cumsum(padded)
    starts = ends - padded
    as_int = lambda k: route[:, k].astype(jnp.int32)
    slot1 = starts[as_int(_R_E1)] + as_int(_R_RANK1)
    slot2 = starts[as_int(_R_E2)] + as_int(_R_RANK2)
    block_row = jnp.arange(n_blocks, dtype=jnp.int32) * rb
    block_exp = jnp.minimum(jnp.sum(block_row[:, None] >= ends[None, :], axis=1), n_exp - 1).astype(jnp.int32)
    block_valid = jnp.clip(cnt[block_exp] - (block_row - starts[block_exp]), 0, rb).astype(jnp.int32)

    tdp = min(DISPATCH_BLOCK, n)
    fill = jnp.concatenate([starts + cnt, ends, ends[-1:] // rb]).astype(jnp.int32)
    xs = pl.pallas_call(
        functools.partial(_dispatch_kernel, n_exp, rb),
        grid_spec=pltpu.PrefetchScalarGridSpec(
            num_scalar_prefetch=3,
            grid=(n // tdp,),
            in_specs=[pl.BlockSpec((tdp, d), lambda i, s1, s2, fl: (i, 0))],
            out_specs=pl.BlockSpec(memory_space=pl.ANY),
            scratch_shapes=[pltpu.VMEM((rb, d), F32), pltpu.SemaphoreType.DMA, pltpu.SemaphoreType.DMA]),
        out_shape=jax.ShapeDtypeStruct((n_slots, d), F32),
        compiler_params=_params("arbitrary"),
        name="moe_dispatch",
    )(slot1, slot2, fill, hn)

    tf = min(FFN_COL_BLOCK, ff)
    ys = pl.pallas_call(
        _expert_ffn_kernel,
        grid_spec=pltpu.PrefetchScalarGridSpec(
            num_scalar_prefetch=2,
            grid=(n_blocks, ff // tf),
            in_specs=[pl.BlockSpec((rb, d), lambda i, f, be, bv: (i, 0)),
                      pl.BlockSpec((1, d, tf), lambda i, f, be, bv: (be[i], 0, f)),
                      pl.BlockSpec((1, d, tf), lambda i, f, be, bv: (be[i], 0, f)),
                      pl.BlockSpec((1, tf, d), lambda i, f, be, bv: (be[i], f, 0))],
            out_specs=pl.BlockSpec((rb, d), lambda i, f, be, bv: (i, 0)),
            scratch_shapes=[pltpu.VMEM((rb, d), BF16), pltpu.VMEM((rb, d), F32)]),
        out_shape=jax.ShapeDtypeStruct((n_slots, d), F32),
        compiler_params=_params("parallel", "arbitrary"),
        name="moe_experts",
    )(block_exp, block_valid, xs, w_gate, w_up, w_down)

    tc = min(COMBINE_BLOCK, n)
    return pl.pallas_call(
        functools.partial(_combine_kernel, final),
        grid_spec=pltpu.PrefetchScalarGridSpec(
            num_scalar_prefetch=2,
            grid=(n // tc,),
            in_specs=[pl.BlockSpec((tc, d), lambda i, s1, s2: (i, 0)),
                      pl.BlockSpec((tc, LANES), lambda i, s1, s2: (i, 0)),
                      pl.BlockSpec((1, d), lambda i, s1, s2: (0, 0)),
                      pl.BlockSpec(memory_space=pl.ANY)],
            out_specs=pl.BlockSpec((tc, d), lambda i, s1, s2: (i, 0)),
            scratch_shapes=[pltpu.VMEM((2, tc, d), F32), pltpu.VMEM((2, tc, d), F32),
                            pltpu.SemaphoreType.DMA((2,))]),
        out_shape=jax.ShapeDtypeStruct((n, d), F32),
        compiler_params=_params("arbitrary"),
        name="moe_combine",
    )(slot1, slot2, x2, route, gf, ys)


def _arrange_in_proj(w_in, s5_w, ssd_w, ssd_cdim, ssd_h, gdn_cdim, gdn_w, gdn_h):
    sizes = (s5_w, ssd_w, ssd_cdim, ssd_h, gdn_cdim, gdn_w, gdn_h, gdn_h)
    offs = [int(o) for o in np.cumsum((0,) + sizes)]
    seg = lambda i: w_in[:, offs[i]:offs[i + 1]]
    small = jnp.concatenate([seg(3), seg(6), seg(7)], axis=1)
    small = jnp.pad(small, ((0, 0), (0, LANES - small.shape[1])))
    cuts = ((offs[0], offs[3]), (offs[4], offs[5]), (offs[5], offs[6]))
    return small, (s5_w, ssd_w, ssd_cdim, gdn_cdim, gdn_w, LANES), cuts


def kernel(x, norm_mix, w_in, w_out, s5_a_re, s5_a_im, s5_b_re, s5_b_im, s5_c_re, s5_c_im, s5_d, s5_log_step, s5_w_glu, s5_norm, ssd_conv_w, ssd_conv_b, ssd_dt_bias, ssd_a_log, ssd_d, ssd_norm, gdn_conv_w, gdn_a_log, gdn_dt_bias, gdn_norm, norm_ffn, ff_w_gate, ff_w_up, ff_w_down, moe_router, moe_w_gate, moe_w_up, moe_w_down, norm_final):
    bsz, seq, d = x.shape
    depth = norm_mix.shape[0]
    s5_w = s5_w_glu.shape[-1]
    ssd_w, ssd_h, ssd_cdim = ssd_norm.shape[-1], ssd_a_log.shape[-1], ssd_conv_w.shape[-1]
    gdn_h, gdn_cdim = gdn_a_log.shape[-1], gdn_conv_w.shape[-1]
    gdn_w = gdn_h * GDN_HEADDIM
    xr = x.reshape(bsz * seq, d)
    for layer in range(depth):
        w_small, sizes, cuts = _arrange_in_proj(w_in[layer], s5_w, ssd_w, ssd_cdim, ssd_h, gdn_cdim, gdn_w, gdn_h)
        u_s5, z_ssd, xbc, qkv, z_gdn, small = _inproj(xr, norm_mix[layer], w_in, layer, w_small, sizes, cuts)
        s5_ops = _s5_operands(s5_a_re[layer], s5_a_im[layer], s5_b_re[layer], s5_b_im[layer], s5_c_re[layer],
                              s5_c_im[layer], s5_d[layer], s5_log_step[layer])
        y_s5 = _s5_scan(u_s5, s5_ops, bsz, seq)
        y_ssd = _ssd(z_ssd, xbc, small, ssd_conv_w[layer], ssd_conv_b[layer], ssd_dt_bias[layer],
                     ssd_a_log[layer], ssd_d[layer], ssd_norm[layer], bsz, seq)
        y_gdn = _gdn(qkv, z_gdn, small, gdn_conv_w[layer], gdn_a_log[layer], gdn_dt_bias[layer],
                     gdn_norm[layer], bsz, seq)
        xr = _outproj(xr, y_s5, y_ssd, y_gdn, s5_w_glu[layer], s5_norm[layer], w_out[layer])
        final_gain = norm_final if layer == depth - 1 else None
        i = layer // 2
        if layer % 2 == 0:
            xr = _ffn(xr, norm_ffn[layer], ff_w_gate[i], ff_w_up[i], ff_w_down[i], final_gain)
        else:
            xr = _moe(xr, norm_ffn[layer], moe_router[i], moe_w_gate[i], moe_w_up[i], moe_w_down[i], final_gain)
    return xr.reshape(bsz, seq, d)
```

```python
import functools
import math

import jax
import jax.numpy as jnp
import numpy as np
from jax import lax
from jax.experimental import pallas as pl
from jax.experimental.pallas import tpu as pltpu

F32 = jnp.float32
BF16 = jnp.bfloat16
EPS = 1e-6

S5_GROUP = 16
S5_STATE = 64
SSD_HEADDIM = 64
SSD_NGROUPS = 2
SSD_STATE = 128
GDN_HEADDIM = 64
CONV_K = 4
TOP_K = 2

LANES = 128
SUBLANES = 8
VMEM_LIMIT_BYTES = 56 * 1024 * 1024

SSD_CHUNK = 128
GDN_CHUNK = 64
GDN_INV_BLOCK = 16
GDN_GROUP = 256
SEQ_BLOCK = 512
S5_TILES_PER_STAGE = 4
MIXER_DELAY_ROUNDS = 10
ROW_BLOCK = 512
FFN_ROW_BLOCK = 1024
FFN_COL_BLOCK = 512
ROUTER_BLOCK = 1024
DISPATCH_BLOCK = 1024
MOE_ROW_BLOCK = 1024
COMBINE_BLOCK = 512
DMA_ISSUE_UNROLL = 8


def _params(*semantics):
    return pltpu.CompilerParams(dimension_semantics=semantics, vmem_limit_bytes=VMEM_LIMIT_BYTES)


def _dot(a, b):
    return jnp.dot(a.astype(BF16), b.astype(BF16), preferred_element_type=F32)


def _mm(a, b):
    return jnp.dot(a, b, preferred_element_type=F32)


def _dot_nt(a, b):
    return lax.dot_general(a.astype(BF16), b.astype(BF16), (((1,), (1,)), ((), ())),
                           preferred_element_type=F32)


def _dot_tn(a, b):
    return lax.dot_general(a.astype(BF16), b.astype(BF16), (((0,), (0,)), ((), ())),
                           preferred_element_type=F32)


def _split3(a):
    hi = a.astype(BF16)
    r1 = a - hi.astype(F32)
    mid = r1.astype(BF16)
    lo = (r1 - mid.astype(F32)).astype(BF16)
    return hi, mid, lo


def _dot_sel(a, sel):
    hi, mid, lo = _split3(a)
    out = jnp.dot(hi, sel, preferred_element_type=F32)
    out += jnp.dot(mid, sel, preferred_element_type=F32)
    out += jnp.dot(lo, sel, preferred_element_type=F32)
    return out


def _sel_dot(sel, a):
    hi, mid, lo = _split3(a)
    out = jnp.dot(sel, hi, preferred_element_type=F32)
    out += jnp.dot(sel, mid, preferred_element_type=F32)
    out += jnp.dot(sel, lo, preferred_element_type=F32)
    return out


def _sigmoid(x):
    return 1.0 / (1.0 + jnp.exp(-x))


def _silu(x):
    return x * _sigmoid(x)


def _softplus(x):
    return jnp.maximum(x, 0.0) + jnp.log(1.0 + jnp.exp(-jnp.abs(x)))


def _gelu_tanh(x):
    c = math.sqrt(2.0 / math.pi)
    return 0.5 * x * (1.0 + jnp.tanh(c * (x + 0.044715 * (x * x * x))))


def _rms(x, w):
    return x * lax.rsqrt(jnp.mean(x * x, axis=-1, keepdims=True) + EPS) * w


def _iota2(shape, axis):
    return lax.broadcasted_iota(jnp.int32, shape, axis)


def _inproj_kernel(layer, sizes, cuts, x_ref, g_ref, w_hbm, small_ref, *rest):
    out_refs = rest[:len(sizes)]
    raw_ref, w_ref = rest[len(sizes):]

    @pl.when(pl.program_id(0) == 0)
    def _():
        pltpu.sync_copy(w_hbm.at[layer], raw_ref)
        ncols = raw_ref.shape[1]
        off = 0
        for lo, hi in cuts:
            a = (lo // LANES) * LANES
            b = min(-(-hi // LANES) * LANES, ncols)
            w_ref[:, off:off + hi - lo] = raw_ref[:, a:b][:, lo - a:hi - a].astype(BF16)
            off += hi - lo
        w_ref[:, off:off + LANES] = small_ref[...].astype(BF16)

    h = _rms(x_ref[...], g_ref[...])
    p = jnp.dot(h.astype(BF16), w_ref[...], preferred_element_type=F32)
    off = 0
    for o_ref, sz in zip(out_refs, sizes):
        o_ref[...] = p[:, off:off + sz]
        off += sz


def _inproj(x2, gain, w_all, layer, small, sizes, cuts):
    n, d = x2.shape
    tm = min(ROW_BLOCK, n)
    total = sum(sizes)
    return pl.pallas_call(
        functools.partial(_inproj_kernel, layer, sizes, cuts),
        grid=(n // tm,),
        in_specs=[pl.BlockSpec((tm, d), lambda i: (i, 0)),
                  pl.BlockSpec((1, d), lambda i: (0, 0)),
                  pl.BlockSpec(memory_space=pl.ANY),
                  pl.BlockSpec((d, LANES), lambda i: (0, 0))],
        out_specs=[pl.BlockSpec((tm, sz), lambda i: (i, 0)) for sz in sizes],
        out_shape=[jax.ShapeDtypeStruct((n, sz), F32) for sz in sizes],
        scratch_shapes=[pltpu.VMEM(w_all.shape[1:], F32), pltpu.VMEM((d, total), BF16)],
        compiler_params=_params("arbitrary"),
        name="inproj",
    )(x2, gain.reshape(1, d), w_all, small)


def _s5_operands(a_re, a_im, b_re, b_im, c_re, c_im, d_skip, log_step):
    g, p = a_re.shape
    h = b_re.shape[-1]
    step = jnp.exp(log_step)[:, None]
    mag = jnp.exp(a_re * step)
    ang = a_im * step
    lb_re, lb_im = mag * jnp.cos(ang), mag * jnp.sin(ang)
    den = a_re * a_re + a_im * a_im
    f_re = ((lb_re - 1.0) * a_re + lb_im * a_im) / den
    f_im = (lb_im * a_re - (lb_re - 1.0) * a_im) / den
    bb_re = f_re[..., None] * b_re - f_im[..., None] * b_im
    bb_im = f_re[..., None] * b_im + f_im[..., None] * b_re

    eye = jnp.eye(g, dtype=F32)
    to_state = lambda m: (m.transpose(0, 2, 1)[:, :, None, :] * eye[:, None, :, None]).reshape(g * h, g * p)
    to_out = lambda m: (m.transpose(0, 2, 1)[:, :, None, :] * eye[:, None, :, None]).reshape(g * p, g * h)
    bbd = jnp.concatenate([to_state(bb_re), to_state(bb_im)], axis=1)
    cbd = jnp.concatenate([to_out(c_re), -to_out(c_im)], axis=0)
    e = jnp.arange(1, SUBLANES + 1, dtype=F32)[:, None, None]
    m = jnp.exp(a_re * step * e)
    lam = jnp.concatenate([(m * jnp.cos(ang * e)).reshape(SUBLANES, g * p),
                           (m * jnp.sin(ang * e)).reshape(SUBLANES, g * p)], axis=1)
    return bbd, cbd, lam, d_skip.reshape(1, g * h)


def _s5_stages(u_ref, bbd_ref, cbd_ref, lam_ref, d_ref, y_ref, s_ref, carry_ref):
    n_slab, t, _ = s_ref.shape
    n_pair = n_slab // 2
    tile = SUBLANES
    pairs = range(n_pair)
    lanes = lambda j: slice(j * LANES, (j + 1) * LANES)

    u = u_ref[...]
    bu = jnp.dot(u.astype(BF16), bbd_ref[...], preferred_element_type=F32)
    for j in range(n_slab):
        s_ref[j] = bu[:, lanes(j)]
    yield
    lam = lam_ref[...]
    lr = [lam[:, lanes(j)] for j in pairs]
    li = [lam[:, lanes(n_pair + j)] for j in pairs]

    def rows(j, r):
        return s_ref.at[j, pl.ds(r, t // tile, stride=tile), :]

    pr = [rows(j, 0)[...] for j in pairs]
    pi = [rows(n_pair + j, 0)[...] for j in pairs]
    for r in range(1, tile):
        nr = [rows(j, r)[...] + lr[j][0:1] * pr[j] - li[j][0:1] * pi[j] for j in pairs]
        ni = [rows(n_pair + j, r)[...] + lr[j][0:1] * pi[j] + li[j][0:1] * pr[j] for j in pairs]
        for j in pairs:
            rows(j, r)[...] = nr[j]
            rows(n_pair + j, r)[...] = ni[j]
        pr, pi = nr, ni
        yield

    last = lambda v: jnp.broadcast_to(v[tile - 1:tile], v.shape)
    cr = [carry_ref[j] for j in pairs]
    ci = [carry_ref[n_pair + j] for j in pairs]
    for k in range(t // tile):
        r0 = k * tile
        tr = [s_ref[j, pl.ds(r0, tile), :] + lr[j] * cr[j] - li[j] * ci[j] for j in pairs]
        ti = [s_ref[n_pair + j, pl.ds(r0, tile), :] + lr[j] * ci[j] + li[j] * cr[j] for j in pairs]
        for j in pairs:
            s_ref[j, pl.ds(r0, tile), :] = tr[j]
            s_ref[n_pair + j, pl.ds(r0, tile), :] = ti[j]
        cr, ci = [last(v) for v in tr], [last(v) for v in ti]
        if k % S5_TILES_PER_STAGE == S5_TILES_PER_STAGE - 1:
            yield
    for j in pairs:
        carry_ref[j] = cr[j]
        carry_ref[n_pair + j] = ci[j]
    s_all = jnp.concatenate([s_ref[j] for j in range(n_slab)], axis=1)
    y_ref[...] = jnp.dot(s_all.astype(BF16), cbd_ref[...], preferred_element_type=F32) + d_ref[...] * u


def _conv_halo_init(pad_ref, t, first):
    halo = SUBLANES

    @pl.when(first)
    def _():
        pad_ref[0:halo, :] = jnp.zeros((halo, pad_ref.shape[1]), F32)

    @pl.when(jnp.logical_not(first))
    def _():
        pad_ref[0:halo, :] = pad_ref[t:t + halo, :]


def _causal_conv_block(x_ref, w_ref, pad_ref):
    t = x_ref.shape[0]
    halo = SUBLANES
    pad_ref[halo:halo + t, :] = x_ref[...]
    acc = w_ref[CONV_K - 1:CONV_K, :] * pad_ref[halo:halo + t, :]
    for j in range(1, CONV_K):
        acc += w_ref[CONV_K - 1 - j:CONV_K - j, :] * pad_ref[halo - j:halo - j + t, :]
    return acc


def _ssd_stages(heads, z_ref, xbc_ref, sm_ref, cw_ref, cb_ref, dtb_ref, a_ref, dexp_ref, nw_ref, ex_ref,
                o_ref, pad_ref, xc_ref, dt_ref, st_ref):
    t = xbc_ref.shape[0]
    q = min(SSD_CHUNK, t)
    width = heads * SSD_HEADDIM
    gw = SSD_NGROUPS * SSD_STATE
    xc_ref[...] = _silu(_causal_conv_block(xbc_ref, cw_ref, pad_ref) + cb_ref[...])
    dt_ref[...] = _softplus(sm_ref[...] + dtb_ref[...])
    yield

    row = _iota2((q, q), 0)
    col = _iota2((q, q), 1)
    causal = row >= col
    tril = jnp.where(causal, 1.0, 0.0).astype(BF16)
    lane = _iota2((1, width), 1)
    rep = heads // SSD_NGROUPS

    for ci in range(t // q):
        r0 = ci * q
        xs = xc_ref[pl.ds(r0, q), 0:width]
        bm = xc_ref[pl.ds(r0, q), width:width + gw]
        cm = xc_ref[pl.ds(r0, q), width + gw:width + 2 * gw]
        dt = dt_ref[pl.ds(r0, q), :]
        acs = _sel_dot(tril, dt * a_ref[...])
        acs_t = acs.T
        xdt = xs * _dot_sel(dt, ex_ref[...])
        yield
        hs = range(heads)
        bgs = [bm[:, gi * SSD_STATE:(gi + 1) * SSD_STATE] for gi in range(SSD_NGROUPS)]
        cgs = [cm[:, gi * SSD_STATE:(gi + 1) * SSD_STATE].astype(BF16) for gi in range(SSD_NGROUPS)]
        cbs = [_dot_nt(cgs[gi], bgs[gi]) for gi in range(SSD_NGROUPS)]
        xdb = xdt.astype(BF16)
        xh = [xdb[:, hh * SSD_HEADDIM:(hh + 1) * SSD_HEADDIM] for hh in hs]
        acol = [acs[:, hh:hh + 1] for hh in hs]
        alast = [acs[q - 1:q, hh:hh + 1] for hh in hs]
        decay = [jnp.where(causal, jnp.exp(jnp.where(causal, acol[hh] - acs_t[hh:hh + 1, :], 0.0)), 0.0)
                 for hh in hs]
        yield
        state = [st_ref[hh] for hh in hs]
        y_diag = [_mm((cbs[hh // rep] * decay[hh]).astype(BF16), xh[hh]) for hh in hs]
        y_off = [_mm(cgs[hh // rep], state[hh].astype(BF16)) * jnp.exp(acol[hh]) for hh in hs]
        upd = [_dot_tn(bgs[hh // rep] * jnp.exp(alast[hh] - acol[hh]), xh[hh]) for hh in hs]
        for hh in hs:
            st_ref[hh] = state[hh] * jnp.exp(alast[hh]) + upd[hh]
        yield
        y = jnp.concatenate([y_diag[hh] + y_off[hh] for hh in hs], axis=1) + dexp_ref[...] * xs
        y = y * _silu(z_ref[pl.ds(r0, q), :])
        gsz = width // SSD_NGROUPS
        y2 = y * y
        out = jnp.zeros_like(y)
        for gi in range(SSD_NGROUPS):
            m = (lane >= gi * gsz) & (lane < (gi + 1) * gsz)
            ms = jnp.sum(jnp.where(m, y2, 0.0), axis=-1, keepdims=True) / gsz
            out = jnp.where(m, y * lax.rsqrt(ms + EPS), out)
        o_ref[pl.ds(r0, q), :] = out * nw_ref[...]
        yield


def _unit_lower_solves(lms, rhs, eye, blk, chunk):
    lds = [jnp.where(blk, lm, 0.0) for lm in lms]
    offs = [(lm - ld).astype(BF16) for lm, ld in zip(lms, lds)]
    dinv = [eye - ld for ld in lds]
    pw = [ld.astype(BF16) for ld in lds]
    yield
    span = 2
    while span < GDN_INV_BLOCK:
        sq = [_mm(p, p) for p in pw]
        yield
        dinv = [_mm(d.astype(BF16), (eye + s).astype(BF16)) for d, s in zip(dinv, sq)]
        pw = [s.astype(BF16) for s in sq]
        yield
        span *= 2
    dinv = [d.astype(BF16) for d in dinv]
    ms = [_mm(d, o).astype(BF16) for d, o in zip(dinv, offs)]
    x0 = [_mm(d, r) for d, r in zip(dinv, rhs)]
    yield
    acc = x0
    for _ in range(chunk // GDN_INV_BLOCK - 1):
        acc = [x - _mm(m, a.astype(BF16)) for x, m, a in zip(x0, ms, acc)]
        yield
    return acc


def _interleave(gens):
    results = [None] * len(gens)
    live = list(range(len(gens)))
    while live:
        for i in list(live):
            try:
                next(gens[i])
            except StopIteration as done:
                results[i] = done.value
                live.remove(i)
        yield
    return results


def _delayed(gen, rounds):
    for _ in range(rounds):
        yield
    return (yield from gen)


def _run(gen):
    try:
        while True:
            next(gen)
    except StopIteration as done:
        return done.value


def _gdn_stages(heads, qkv_ref, z_ref, sm_ref, cw_ref, dtb_ref, a_ref, nw_ref, exb_ref, exg_ref, ones_ref,
                o_ref, pad_ref, st_ref):
    t = qkv_ref.shape[0]
    c = min(GDN_CHUNK, t)
    hd = GDN_HEADDIM
    width = heads * hd
    hs = range(heads)
    sl = [slice(hh * hd, (hh + 1) * hd) for hh in hs]

    def seg_sum(v):
        hi = v.astype(BF16)
        lo = (v - hi.astype(F32)).astype(BF16)
        return _mm(hi, ones_ref[...]) + _mm(lo, ones_ref[...])

    xc = _silu(_causal_conv_block(qkv_ref, cw_ref, pad_ref))
    yield
    qf = xc[:, 0:width]
    kf = xc[:, width:2 * width]
    vf = xc[:, 2 * width:3 * width]
    qf = qf * lax.rsqrt(seg_sum(qf * qf) + EPS) * (hd ** -0.5)
    kf = kf * lax.rsqrt(seg_sum(kf * kf) + EPS)
    sm = sm_ref[...]
    beta = _dot_sel(_sigmoid(sm), exb_ref[...])
    g = a_ref[...] * _softplus(sm + dtb_ref[...])
    yield

    gs = min(GDN_GROUP, t)
    row = _iota2((gs, gs), 0)
    col = _iota2((gs, gs), 1)
    same = (row // c) == (col // c)
    causal = same & (row >= col)
    strict = same & (row > col)
    eye = jnp.where(row == col, 1.0, 0.0)
    blk = (row // GDN_INV_BLOCK) == (col // GDN_INV_BLOCK)
    tril_sel = jnp.where(causal, 1.0, 0.0).astype(BF16)
    same_sel = jnp.where(same, 1.0, 0.0).astype(BF16)
    nt = (((1,), (1,)), ((), ()))

    def gates(gi):
        rg = slice(gi * gs, (gi + 1) * gs)
        gcs = _sel_dot(tril_sel, g[rg])
        gtot = _sel_dot(same_sel, g[rg])
        gcs_t = gcs.T
        yield
        gw = _dot_sel(gcs, exg_ref[...])
        gtw = _dot_sel(gtot, exg_ref[...])
        gexp = jnp.exp(gw)
        kb = kf[rg] * beta[rg]
        yield
        decay = []
        for hh in hs:
            lg = 2 * heads + hh
            diff = jnp.where(causal, gcs[:, lg:lg + 1] - gcs_t[lg:lg + 1, :], 0.0)
            decay.append(jnp.where(causal, jnp.exp(diff), 0.0))
            yield
        return dict(kb=kb, vb=vf[rg] * beta[rg], kcd_in=kb * gexp, qd=(qf[rg] * gexp).astype(BF16),
                    kend=(kf[rg] * jnp.exp(gtw - gw)).astype(BF16), cdec=jnp.exp(gtw), decay=decay,
                    kfb=kf[rg].astype(BF16), qfb=qf[rg].astype(BF16))

    def solve(p):
        both = jnp.concatenate([p["kb"].astype(BF16), p["qfb"]], axis=0)
        kq = [lax.dot_general(both[:, sl[hh]], p["kfb"][:, sl[hh]], nt, preferred_element_type=F32) for hh in hs]
        yield
        lms = [jnp.where(strict, kq[hh][0:gs] * p["decay"][hh], 0.0) for hh in hs]
        qk = [(kq[hh][gs:2 * gs] * p["decay"][hh]).astype(BF16) for hh in hs]
        rhs = [jnp.concatenate([p["vb"][:, sl[hh]], p["kcd_in"][:, sl[hh]]], axis=1).astype(BF16) for hh in hs]
        vk = yield from _unit_lower_solves(lms, rhs, eye, blk, c)
        return qk, vk

    def recur(state, p, qk, vk):
        outs = []
        for ci in range(gs // c):
            rs = slice(ci * c, (ci + 1) * c)
            lhs = [jnp.concatenate([vk[hh][rs, hd:2 * hd].astype(BF16), p["qd"][rs, sl[hh]]], axis=0)
                   for hh in hs]
            sb = [s.astype(BF16) for s in state]
            both = [_mm(lhs[hh], sb[hh]) for hh in hs]
            yield
            v_new = [(vk[hh][rs, 0:hd] - both[hh][0:c]).astype(BF16) for hh in hs]
            o = [both[hh][c:2 * c] + _mm(qk[hh][rs, ci * c:(ci + 1) * c], v_new[hh]) for hh in hs]
            upd = [lax.dot_general(p["kend"][rs, sl[hh]], v_new[hh], (((0,), (0,)), ((), ())),
                                   preferred_element_type=F32) for hh in hs]
            state = [state[hh] * p["cdec"][ci * c:ci * c + 1, sl[hh]] + upd[hh] for hh in hs]
            outs.append(jnp.concatenate(o, axis=1))
            yield
        return state, outs

    n_groups = t // gs
    parts, solved, outs = {}, {}, []
    state = [st_ref[hh] for hh in hs]
    for k in range(n_groups + 2):
        jobs = {}
        if k < n_groups:
            jobs["gates"] = gates(k)
        if 0 <= k - 1 < n_groups:
            jobs["solve"] = solve(parts[k - 1])
        if 0 <= k - 2 < n_groups:
            jobs["recur"] = recur(state, parts[k - 2], *solved[k - 2])
        done = dict(zip(jobs, (yield from _interleave(list(jobs.values())))))
        if "gates" in done:
            parts[k] = done["gates"]
        if "solve" in done:
            solved[k - 1] = done["solve"]
        if "recur" in done:
            state, o_g = done["recur"]
            outs += o_g
    for hh in hs:
        st_ref[hh] = state[hh]
    yield
    o = jnp.concatenate(outs, axis=0)
    o = o * lax.rsqrt(seg_sum(o * o) / hd + EPS) * nw_ref[...]
    o_ref[...] = o * _silu(z_ref[...])


_N_S5_IN, _N_SSD_IN, _N_GDN_IN = 5, 10, 10


def _mixers_kernel(ssd_heads, gdn_heads, *refs):
    n_in = _N_S5_IN + _N_SSD_IN + _N_GDN_IN
    s5_in = refs[:_N_S5_IN]
    ssd_in = refs[_N_S5_IN:_N_S5_IN + _N_SSD_IN]
    gdn_in = refs[_N_S5_IN + _N_SSD_IN:n_in]
    y_s5, y_ssd, y_gdn = refs[n_in:n_in + 3]
    s5_s, s5_carry, ssd_pad, ssd_xc, ssd_dt, ssd_st, gdn_pad, gdn_st = refs[n_in + 3:]
    t = y_s5.shape[0]
    first = pl.program_id(1) == 0

    @pl.when(first)
    def _():
        s5_carry[...] = jnp.zeros(s5_carry.shape, F32)
        ssd_st[...] = jnp.zeros(ssd_st.shape, F32)
        gdn_st[...] = jnp.zeros(gdn_st.shape, F32)

    _conv_halo_init(ssd_pad, t, first)
    _conv_halo_init(gdn_pad, t, first)
    _run(_interleave([
        _gdn_stages(gdn_heads, *gdn_in, y_gdn, gdn_pad, gdn_st),
        _delayed(_ssd_stages(ssd_heads, *ssd_in, y_ssd, ssd_pad, ssd_xc, ssd_dt, ssd_st), MIXER_DELAY_ROUNDS),
        _delayed(_s5_stages(*s5_in, y_s5, s5_s, s5_carry), MIXER_DELAY_ROUNDS),
    ]))


def _mixers(u_s5, s5_ops, z_ssd, xbc, small, ssd_p, qkv, z_gdn, gdn_p, bsz, seq):
    bbd, cbd, lam, d_s5 = s5_ops
    conv_w, conv_b, dt_bias, a_log, d_skip, norm_w = ssd_p
    g_conv_w, g_a_log, g_dt_bias, g_norm_w = gdn_p
    n, s5_w = u_s5.shape
    s2 = bbd.shape[1]
    ssd_w, ssd_cdim, ssd_h = z_ssd.shape[1], xbc.shape[1], a_log.shape[0]
    gdn_w, gdn_cdim, gdn_h = z_gdn.shape[1], qkv.shape[1], g_a_log.shape[0]
    t = min(SEQ_BLOCK, seq)
    nblk = seq // t
    tok = lambda b, c: (b * nblk + c, 0)
    const = lambda b, c: (0, 0)
    rows = lambda w: pl.BlockSpec((t, w), tok)
    whole = lambda a: pl.BlockSpec(a.shape, const)
    pad = lambda v, off: jnp.zeros((1, LANES), F32).at[0, off:off + v.shape[0]].set(v)
    lanes = jnp.arange(LANES)[:, None]
    ssd_ex = (lanes == (jnp.arange(ssd_w)[None, :] // SSD_HEADDIM)).astype(BF16)
    head_of = jnp.arange(gdn_w)[None, :] // GDN_HEADDIM
    s5_args = [bbd.astype(BF16), cbd.astype(BF16), lam, d_s5]
    ssd_args = [conv_w, conv_b.reshape(1, ssd_cdim), pad(dt_bias, 0), pad(-jnp.exp(a_log), 0),
                jnp.repeat(d_skip, SSD_HEADDIM).reshape(1, ssd_w), norm_w.reshape(1, ssd_w), ssd_ex]
    gdn_args = [g_conv_w, pad(g_dt_bias, 2 * gdn_h), pad(-jnp.exp(g_a_log), 2 * gdn_h),
                jnp.tile(g_norm_w, gdn_h).reshape(1, gdn_w), (lanes == gdn_h + head_of).astype(BF16),
                (lanes == 2 * gdn_h + head_of).astype(BF16), (head_of.T == head_of).astype(BF16)]
    in_specs = ([rows(s5_w)] + [whole(a) for a in s5_args]
                + [rows(ssd_w), rows(ssd_cdim), rows(LANES)] + [whole(a) for a in ssd_args]
                + [rows(gdn_cdim), rows(gdn_w), rows(LANES)] + [whole(a) for a in gdn_args])
    assert len(in_specs) == _N_S5_IN + _N_SSD_IN + _N_GDN_IN
    return pl.pallas_call(
        functools.partial(_mixers_kernel, ssd_h, gdn_h),
        grid=(bsz, nblk),
        in_specs=in_specs,
        out_specs=[rows(s5_w), rows(ssd_w), rows(gdn_w)],
        out_shape=[jax.ShapeDtypeStruct((n, w), F32) for w in (s5_w, ssd_w, gdn_w)],
        scratch_shapes=[pltpu.VMEM((s2 // LANES, t, LANES), F32),
                        pltpu.VMEM((s2 // LANES, SUBLANES, LANES), F32),
                        pltpu.VMEM((SUBLANES + t, ssd_cdim), F32),
                        pltpu.VMEM((t, ssd_cdim), F32),
                        pltpu.VMEM((t, LANES), F32),
                        pltpu.VMEM((ssd_h, SSD_STATE, SSD_HEADDIM), F32),
                        pltpu.VMEM((SUBLANES + t, gdn_cdim), F32),
                        pltpu.VMEM((gdn_h, GDN_HEADDIM, GDN_HEADDIM), F32)],
        compiler_params=_params("parallel", "arbitrary"),
        name="mixers",
    )(u_s5, *s5_args, z_ssd, xbc, small, *ssd_args, qkv, z_gdn, small, *gdn_args)


def _outproj_kernel(x_ref, s5_ref, ssd_ref, gdn_ref, wglu_ref, s5n_ref, wout_ref, o_ref):
    y = _gelu_tanh(s5_ref[...])
    y = y * _sigmoid(jnp.dot(y.astype(BF16), wglu_ref[...], preferred_element_type=F32))
    y = _rms(y, s5n_ref[...])
    mix = jnp.concatenate([y, ssd_ref[...], gdn_ref[...]], axis=1)
    o_ref[...] = x_ref[...] + jnp.dot(mix.astype(BF16), wout_ref[...], preferred_element_type=F32)


def _outproj(x2, y_s5, y_ssd, y_gdn, w_glu, s5_norm, w_out):
    n, d = x2.shape
    tm = min(ROW_BLOCK, n)
    ws = [y_s5.shape[1], y_ssd.shape[1], y_gdn.shape[1]]
    row = lambda i: (i, 0)
    const = lambda i: (0, 0)
    return pl.pallas_call(
        _outproj_kernel,
        grid=(n // tm,),
        in_specs=[pl.BlockSpec((tm, d), row)] + [pl.BlockSpec((tm, w), row) for w in ws]
                 + [pl.BlockSpec((ws[0], ws[0]), const), pl.BlockSpec((1, ws[0]), const),
                    pl.BlockSpec((sum(ws), d), const)],
        out_specs=pl.BlockSpec((tm, d), row),
        out_shape=jax.ShapeDtypeStruct((n, d), F32),
        compiler_params=_params("parallel"),
        name="outproj",
    )(x2, y_s5, y_ssd, y_gdn, w_glu.astype(BF16), s5_norm.reshape(1, -1), w_out.astype(BF16))


def _ffn_kernel(final, x_ref, g_ref, wg_ref, wu_ref, wd_ref, gf_ref, o_ref, h_ref, acc_ref):
    f = pl.program_id(1)

    @pl.when(f == 0)
    def _():
        h_ref[...] = _rms(x_ref[...], g_ref[...]).astype(BF16)
        acc_ref[...] = jnp.zeros(acc_ref.shape, F32)

    h = h_ref[...]
    a = jnp.dot(h, wg_ref[...].astype(BF16), preferred_element_type=F32)
    u = jnp.dot(h, wu_ref[...].astype(BF16), preferred_element_type=F32)
    acc_ref[...] += jnp.dot((_silu(a) * u).astype(BF16), wd_ref[...].astype(BF16), preferred_element_type=F32)

    @pl.when(f == pl.num_programs(1) - 1)
    def _():
        out = x_ref[...] + acc_ref[...]
        o_ref[...] = _rms(out, gf_ref[...]) if final else out


def _ffn(x2, gain, w_gate, w_up, w_down, final_gain):
    n, d = x2.shape
    ff = w_gate.shape[1]
    tm = min(FFN_ROW_BLOCK, n)
    tf = min(FFN_COL_BLOCK, ff)
    final = final_gain is not None
    gf = (final_gain if final else gain).reshape(1, d)
    return pl.pallas_call(
        functools.partial(_ffn_kernel, final),
        grid=(n // tm, ff // tf),
        in_specs=[pl.BlockSpec((tm, d), lambda i, f: (i, 0)),
                  pl.BlockSpec((1, d), lambda i, f: (0, 0)),
                  pl.BlockSpec((d, tf), lambda i, f: (0, f)),
                  pl.BlockSpec((d, tf), lambda i, f: (0, f)),
                  pl.BlockSpec((tf, d), lambda i, f: (f, 0)),
                  pl.BlockSpec((1, d), lambda i, f: (0, 0))],
        out_specs=pl.BlockSpec((tm, d), lambda i, f: (i, 0)),
        out_shape=jax.ShapeDtypeStruct((n, d), F32),
        scratch_shapes=[pltpu.VMEM((tm, d), BF16), pltpu.VMEM((tm, d), F32)],
        compiler_params=_params("parallel", "arbitrary"),
        name="ffn",
    )(x2, gain.reshape(1, d), w_gate, w_up, w_down, gf)


_R_E1, _R_E2, _R_W1, _R_W2, _R_RANK1, _R_RANK2 = range(6)


def _router_kernel(n_exp, x_ref, g_ref, wr_ref, tri_ref, hn_ref, rt_ref, cnt_ref, base_ref):
    @pl.when(pl.program_id(0) == 0)
    def _():
        base_ref[...] = jnp.zeros(base_ref.shape, F32)

    hn = _rms(x_ref[...], g_ref[...])
    hn_ref[...] = hn
    h_hi = hn.astype(BF16)
    h_lo = (hn - h_hi.astype(F32)).astype(BF16)
    w_hi = wr_ref[...].astype(BF16)
    w_lo = (wr_ref[...] - w_hi.astype(F32)).astype(BF16)
    logits = _mm(h_hi, w_hi) + _mm(h_lo, w_hi) + _mm(h_hi, w_lo)
    lane = _iota2(logits.shape, 1)
    neg = jnp.float32(-jnp.inf)
    logits = jnp.where(lane < n_exp, logits, neg)
    m1 = jnp.max(logits, axis=-1, keepdims=True)
    i1 = jnp.min(jnp.where(logits == m1, lane, LANES), axis=-1, keepdims=True)
    rest = jnp.where(lane == i1, neg, logits)
    m2 = jnp.max(rest, axis=-1, keepdims=True)
    i2 = jnp.min(jnp.where(rest == m2, lane, LANES), axis=-1, keepdims=True)
    w1 = 1.0 / (1.0 + jnp.exp(m2 - m1))
    w2 = 1.0 / (1.0 + jnp.exp(m1 - m2))
    hit = (lane == i1) | (lane == i2)
    onehot = jnp.where(hit, 1.0, 0.0)
    before = jnp.dot(tri_ref[...], onehot.astype(BF16), preferred_element_type=F32) + base_ref[...]
    r1 = jnp.sum(jnp.where(lane == i1, before, 0.0), axis=-1, keepdims=True)
    r2 = jnp.sum(jnp.where(lane == i2, before, 0.0), axis=-1, keepdims=True)
    cols = {_R_E1: i1.astype(F32), _R_E2: i2.astype(F32), _R_W1: w1, _R_W2: w2, _R_RANK1: r1, _R_RANK2: r2}
    rt = jnp.zeros(logits.shape, F32)
    for k, v in cols.items():
        rt = jnp.where(lane == k, v, rt)
    rt_ref[...] = rt
    base_ref[...] += jnp.sum(onehot, axis=0, keepdims=True)
    cnt_ref[...] = base_ref[...]


def _row_copy(src_ref, src_row, dst_ref, dst_row, sem):
    return pltpu.make_async_copy(src_ref.at[pl.ds(src_row, 1)], dst_ref.at[pl.ds(dst_row, 1)], sem)


def _dispatch_kernel(n_exp, rb, s1_ref, s2_ref, fill_ref, hn_ref, xs_ref, zero_ref, sem, zsem):
    tm = hn_ref.shape[0]
    base = pl.program_id(0) * tm

    @pl.when(pl.program_id(0) == 0)
    def _():
        zero_ref[...] = jnp.zeros(zero_ref.shape, F32)
        for e in range(n_exp):
            lo, hi = fill_ref[e], fill_ref[n_exp + e]

            def start_row(r, carry):
                _row_copy(zero_ref, 0, xs_ref, r, zsem).start()
                return carry

            def wait_row(r, carry):
                _row_copy(zero_ref, 0, xs_ref, r, zsem).wait()
                return carry

            lax.fori_loop(lo, hi, start_row, 0)
            lax.fori_loop(lo, hi, wait_row, 0)

        def block_copy(b):
            return pltpu.make_async_copy(zero_ref, xs_ref.at[pl.ds(pl.multiple_of(b * rb, rb), rb)], zsem)

        def start_block(b, carry):
            block_copy(b).start()
            return carry

        def wait_block(b, carry):
            block_copy(b).wait()
            return carry

        n_blocks = xs_ref.shape[0] // rb
        lax.fori_loop(fill_ref[2 * n_exp], n_blocks, start_block, 0)
        lax.fori_loop(fill_ref[2 * n_exp], n_blocks, wait_block, 0)

    def issue(r, carry):
        _row_copy(hn_ref, r, xs_ref, s1_ref[base + r], sem).start()
        _row_copy(hn_ref, r, xs_ref, s2_ref[base + r], sem).start()
        return carry

    lax.fori_loop(0, tm, issue, 0, unroll=DMA_ISSUE_UNROLL)
    for _ in range(TOP_K):
        pltpu.make_async_copy(hn_ref, xs_ref.at[pl.ds(0, tm)], sem).wait()


def _expert_ffn_kernel(be_ref, bv_ref, xs_ref, wg_ref, wu_ref, wd_ref, ys_ref, h_ref, acc_ref):
    i = pl.program_id(0)
    f = pl.program_id(1)
    last = pl.num_programs(1) - 1
    valid = bv_ref[i]

    @pl.when(valid > 0)
    def _():
        @pl.when(f == 0)
        def _():
            row = _iota2((xs_ref.shape[0], 1), 0)
            h_ref[...] = jnp.where(row < valid, xs_ref[...], 0.0).astype(BF16)
            acc_ref[...] = jnp.zeros(acc_ref.shape, F32)

        h = h_ref[...]
        a = jnp.dot(h, wg_ref[0].astype(BF16), preferred_element_type=F32)
        u = jnp.dot(h, wu_ref[0].astype(BF16), preferred_element_type=F32)
        acc_ref[...] += jnp.dot((_silu(a) * u).astype(BF16), wd_ref[0].astype(BF16), preferred_element_type=F32)

        @pl.when(f == last)
        def _():
            ys_ref[...] = acc_ref[...]

    @pl.when((valid == 0) & (f == last))
    def _():
        ys_ref[...] = jnp.zeros(ys_ref.shape, F32)


def _combine_kernel(final, s1_ref, s2_ref, x_ref, rt_ref, gf_ref, ys_ref, o_ref, b1_ref, b2_ref, sem):
    tm = x_ref.shape[0]
    i = pl.program_id(0)
    cur = i % 2

    def issue(block, buf):
        def one(r, carry):
            _row_copy(ys_ref, s1_ref[block * tm + r], b1_ref.at[buf], r, sem.at[buf]).start()
            _row_copy(ys_ref, s2_ref[block * tm + r], b2_ref.at[buf], r, sem.at[buf]).start()
            return carry

        lax.fori_loop(0, tm, one, 0, unroll=DMA_ISSUE_UNROLL)

    @pl.when(i == 0)
    def _():
        issue(0, 0)

    @pl.when(i + 1 < pl.num_programs(0))
    def _():
        issue(i + 1, 1 - cur)

    for b_ref in (b1_ref, b2_ref):
        pltpu.make_async_copy(ys_ref.at[pl.ds(0, tm)], b_ref.at[cur], sem.at[cur]).wait()
    rt = rt_ref[...]
    out = (x_ref[...] + rt[:, _R_W1:_R_W1 + 1] * b1_ref[cur] + rt[:, _R_W2:_R_W2 + 1] * b2_ref[cur])
    o_ref[...] = _rms(out, gf_ref[...]) if final else out


def _moe(x2, gain, w_router, w_gate, w_up, w_down, final_gain):
    n, d = x2.shape
    n_exp, _, ff = w_gate.shape
    final = final_gain is not None
    gf = (final_gain if final else gain).reshape(1, d)
    wr = jnp.zeros((d, LANES), F32).at[:, :n_exp].set(w_router)

    tr = min(ROUTER_BLOCK, n)
    tri = (jnp.arange(tr)[:, None] > jnp.arange(tr)[None, :]).astype(BF16)
    hn, route, counts = pl.pallas_call(
        functools.partial(_router_kernel, n_exp),
        grid=(n // tr,),
        in_specs=[pl.BlockSpec((tr, d), lambda i: (i, 0)),
                  pl.BlockSpec((1, d), lambda i: (0, 0)),
                  pl.BlockSpec((d, LANES), lambda i: (0, 0)),
                  pl.BlockSpec((tr, tr), lambda i: (0, 0))],
        out_specs=[pl.BlockSpec((tr, d), lambda i: (i, 0)),
                   pl.BlockSpec((tr, LANES), lambda i: (i, 0)),
                   pl.BlockSpec((1, LANES), lambda i: (0, 0))],
        out_shape=[jax.ShapeDtypeStruct((n, d), F32),
                   jax.ShapeDtypeStruct((n, LANES), F32),
                   jax.ShapeDtypeStruct((1, LANES), F32)],
        scratch_shapes=[pltpu.VMEM((1, LANES), F32)],
        compiler_params=_params("arbitrary"),
        name="moe_router",
    )(x2, gain.reshape(1, d), wr, tri)

    rb = min(MOE_ROW_BLOCK, TOP_K * n)
    n_slots = TOP_K * n + n_exp * rb
    n_blocks = n_slots // rb
    cnt = counts[0, :n_exp].astype(jnp.int32)
    padded = ((cnt + rb - 1) // rb) * rb
    ends = jnp.cumsum(padded)
    starts = ends - padded
    as_int = lambda k: route[:, k].astype(jnp.int32)
    slot1 = starts[as_int(_R_E1)] + as_int(_R_RANK1)
    slot2 = starts[as_int(_R_E2)] + as_int(_R_RANK2)
    block_row = jnp.arange(n_blocks, dtype=jnp.int32) * rb
    block_exp = jnp.minimum(jnp.sum(block_row[:, None] >= ends[None, :], axis=1), n_exp - 1).astype(jnp.int32)
    block_valid = jnp.clip(cnt[block_exp] - (block_row - starts[block_exp]), 0, rb).astype(jnp.int32)

    tdp = min(DISPATCH_BLOCK, n)
    fill = jnp.concatenate([starts + cnt, ends, ends[-1:] // rb]).astype(jnp.int32)
    xs = pl.pallas_call(
        functools.partial(_dispatch_kernel, n_exp, rb),
        grid_spec=pltpu.PrefetchScalarGridSpec(
            num_scalar_prefetch=3,
            grid=(n // tdp,),
            in_specs=[pl.BlockSpec((tdp, d), lambda i, s1, s2, fl: (i, 0))],
            out_specs=pl.BlockSpec(memory_space=pl.ANY),
            scratch_shapes=[pltpu.VMEM((rb, d), F32), pltpu.SemaphoreType.DMA, pltpu.SemaphoreType.DMA]),
        out_shape=jax.ShapeDtypeStruct((n_slots, d), F32),
        compiler_params=_params("arbitrary"),
        name="moe_dispatch",
    )(slot1, slot2, fill, hn)

    tf = min(FFN_COL_BLOCK, ff)
    ys = pl.pallas_call(
        _expert_ffn_kernel,
        grid_spec=pltpu.PrefetchScalarGridSpec(
            num_scalar_prefetch=2,
            grid=(n_blocks, ff // tf),
            in_specs=[pl.BlockSpec((rb, d), lambda i, f, be, bv: (i, 0)),
                      pl.BlockSpec((1, d, tf), lambda i, f, be, bv: (be[i], 0, f)),
                      pl.BlockSpec((1, d, tf), lambda i, f, be, bv: (be[i], 0, f)),
                      pl.BlockSpec((1, tf, d), lambda i, f, be, bv: (be[i], f, 0))],
            out_specs=pl.BlockSpec((rb, d), lambda i, f, be, bv: (i, 0)),
            scratch_shapes=[pltpu.VMEM((rb, d), BF16), pltpu.VMEM((rb, d), F32)]),
        out_shape=jax.ShapeDtypeStruct((n_slots, d), F32),
        compiler_params=_params("parallel", "arbitrary"),
        name="moe_experts",
    )(block_exp, block_valid, xs, w_gate, w_up, w_down)

    tc = min(COMBINE_BLOCK, n)
    return pl.pallas_call(
        functools.partial(_combine_kernel, final),
        grid_spec=pltpu.PrefetchScalarGridSpec(
            num_scalar_prefetch=2,
            grid=(n // tc,),
            in_specs=[pl.BlockSpec((tc, d), lambda i, s1, s2: (i, 0)),
                      pl.BlockSpec((tc, LANES), lambda i, s1, s2: (i, 0)),
                      pl.BlockSpec((1, d), lambda i, s1, s2: (0, 0)),
                      pl.BlockSpec(memory_space=pl.ANY)],
            out_specs=pl.BlockSpec((tc, d), lambda i, s1, s2: (i, 0)),
            scratch_shapes=[pltpu.VMEM((2, tc, d), F32), pltpu.VMEM((2, tc, d), F32),
                            pltpu.SemaphoreType.DMA((2,))]),
        out_shape=jax.ShapeDtypeStruct((n, d), F32),
        compiler_params=_params("arbitrary"),
        name="moe_combine",
    )(slot1, slot2, x2, route, gf, ys)


def _arrange_in_proj(w_in, s5_w, ssd_w, ssd_cdim, ssd_h, gdn_cdim, gdn_w, gdn_h):
    sizes = (s5_w, ssd_w, ssd_cdim, ssd_h, gdn_cdim, gdn_w, gdn_h, gdn_h)
    offs = [int(o) for o in np.cumsum((0,) + sizes)]
    seg = lambda i: w_in[:, offs[i]:offs[i + 1]]
    small = jnp.concatenate([seg(3), seg(6), seg(7)], axis=1)
    small = jnp.pad(small, ((0, 0), (0, LANES - small.shape[1])))
    cuts = ((offs[0], offs[3]), (offs[4], offs[5]), (offs[5], offs[6]))
    return small, (s5_w, ssd_w, ssd_cdim, gdn_cdim, gdn_w, LANES), cuts


def kernel(x, norm_mix, w_in, w_out, s5_a_re, s5_a_im, s5_b_re, s5_b_im, s5_c_re, s5_c_im, s5_d, s5_log_step, s5_w_glu, s5_norm, ssd_conv_w, ssd_conv_b, ssd_dt_bias, ssd_a_log, ssd_d, ssd_norm, gdn_conv_w, gdn_a_log, gdn_dt_bias, gdn_norm, norm_ffn, ff_w_gate, ff_w_up, ff_w_down, moe_router, moe_w_gate, moe_w_up, moe_w_down, norm_final):
    bsz, seq, d = x.shape
    depth = norm_mix.shape[0]
    s5_w = s5_w_glu.shape[-1]
    ssd_w, ssd_h, ssd_cdim = ssd_norm.shape[-1], ssd_a_log.shape[-1], ssd_conv_w.shape[-1]
    gdn_h, gdn_cdim = gdn_a_log.shape[-1], gdn_conv_w.shape[-1]
    gdn_w = gdn_h * GDN_HEADDIM
    xr = x.reshape(bsz * seq, d)
    for layer in range(depth):
        w_small, sizes, cuts = _arrange_in_proj(w_in[layer], s5_w, ssd_w, ssd_cdim, ssd_h, gdn_cdim, gdn_w, gdn_h)
        u_s5, z_ssd, xbc, qkv, z_gdn, small = _inproj(xr, norm_mix[layer], w_in, layer, w_small, sizes, cuts)
        s5_ops = _s5_operands(s5_a_re[layer], s5_a_im[layer], s5_b_re[layer], s5_b_im[layer], s5_c_re[layer],
                              s5_c_im[layer], s5_d[layer], s5_log_step[layer])
        ssd_p = (ssd_conv_w[layer], ssd_conv_b[layer], ssd_dt_bias[layer], ssd_a_log[layer], ssd_d[layer],
                 ssd_norm[layer])
        gdn_p = (gdn_conv_w[layer], gdn_a_log[layer], gdn_dt_bias[layer], gdn_norm[layer])
        y_s5, y_ssd, y_gdn = _mixers(u_s5, s5_ops, z_ssd, xbc, small, ssd_p, qkv, z_gdn, gdn_p, bsz, seq)
        xr = _outproj(xr, y_s5, y_ssd, y_gdn, s5_w_glu[layer], s5_norm[layer], w_out[layer])
        final_gain = norm_final if layer == depth - 1 else None
        i = layer // 2
        if layer % 2 == 0:
            xr = _ffn(xr, norm_ffn[layer], ff_w_gate[i], ff_w_up[i], ff_w_down[i], final_gain)
        else:
            xr = _moe(xr, norm_ffn[layer], moe_router[i], moe_w_gate[i], moe_w_up[i], moe_w_down[i], final_gain)
    return xr.reshape(bsz, seq, d)
```

```python
import functools
import math

import jax
import jax.numpy as jnp
import numpy as np
from jax import lax
from jax.experimental import pallas as pl
from jax.experimental.pallas import tpu as pltpu

F32 = jnp.float32
BF16 = jnp.bfloat16
EPS = 1e-6

S5_GROUP = 16
S5_STATE = 64
SSD_HEADDIM = 64
SSD_NGROUPS = 2
SSD_STATE = 128
GDN_HEADDIM = 64
CONV_K = 4
TOP_K = 2

LANES = 128
SUBLANES = 8
VMEM_LIMIT_BYTES = 56 * 1024 * 1024

SSD_CHUNK = 128
GDN_CHUNK = 64
GDN_INV_BLOCK = 16
GDN_GROUP = 256
SEQ_BLOCK = 512
S5_TILES_PER_STAGE = 4
MIXER_DELAY_ROUNDS = 10
ROW_BLOCK = 512
FFN_ROW_BLOCK = 1024
FFN_COL_BLOCK = 512
ROUTER_BLOCK = 1024
DISPATCH_BLOCK = 2048
MOE_ROW_BLOCK = 1024
COMBINE_BLOCK = 1024
DMA_ISSUE_UNROLL = 8


def _params(*semantics):
    return pltpu.CompilerParams(dimension_semantics=semantics, vmem_limit_bytes=VMEM_LIMIT_BYTES)


def _dot(a, b):
    return jnp.dot(a.astype(BF16), b.astype(BF16), preferred_element_type=F32)


def _mm(a, b):
    return jnp.dot(a, b, preferred_element_type=F32)


def _dot_nt(a, b):
    return lax.dot_general(a.astype(BF16), b.astype(BF16), (((1,), (1,)), ((), ())),
                           preferred_element_type=F32)


def _dot_tn(a, b):
    return lax.dot_general(a.astype(BF16), b.astype(BF16), (((0,), (0,)), ((), ())),
                           preferred_element_type=F32)


def _split3(a):
    hi = a.astype(BF16)
    r1 = a - hi.astype(F32)
    mid = r1.astype(BF16)
    lo = (r1 - mid.astype(F32)).astype(BF16)
    return hi, mid, lo


def _dot_sel(a, sel):
    hi, mid, lo = _split3(a)
    out = jnp.dot(hi, sel, preferred_element_type=F32)
    out += jnp.dot(mid, sel, preferred_element_type=F32)
    out += jnp.dot(lo, sel, preferred_element_type=F32)
    return out


def _sel_dot(sel, a):
    hi, mid, lo = _split3(a)
    out = jnp.dot(sel, hi, preferred_element_type=F32)
    out += jnp.dot(sel, mid, preferred_element_type=F32)
    out += jnp.dot(sel, lo, preferred_element_type=F32)
    return out


def _sigmoid(x):
    return 1.0 / (1.0 + jnp.exp(-x))


def _silu(x):
    return x * _sigmoid(x)


def _softplus(x):
    return jnp.maximum(x, 0.0) + jnp.log(1.0 + jnp.exp(-jnp.abs(x)))


def _gelu_tanh(x):
    c = math.sqrt(2.0 / math.pi)
    return 0.5 * x * (1.0 + jnp.tanh(c * (x + 0.044715 * (x * x * x))))


def _rms(x, w):
    return x * lax.rsqrt(jnp.mean(x * x, axis=-1, keepdims=True) + EPS) * w


def _iota2(shape, axis):
    return lax.broadcasted_iota(jnp.int32, shape, axis)


def _inproj_kernel(layer, sizes, cuts, x_ref, g_ref, w_hbm, small_ref, *rest):
    out_refs = rest[:len(sizes)]
    raw_ref, w_ref = rest[len(sizes):]

    @pl.when(pl.program_id(0) == 0)
    def _():
        pltpu.sync_copy(w_hbm.at[layer], raw_ref)
        ncols = raw_ref.shape[1]
        off = 0
        for lo, hi in cuts:
            a = (lo // LANES) * LANES
            b = min(-(-hi // LANES) * LANES, ncols)
            w_ref[:, off:off + hi - lo] = raw_ref[:, a:b][:, lo - a:hi - a].astype(BF16)
            off += hi - lo
        w_ref[:, off:off + LANES] = small_ref[...].astype(BF16)

    h = _rms(x_ref[...], g_ref[...])
    p = jnp.dot(h.astype(BF16), w_ref[...], preferred_element_type=F32)
    off = 0
    for o_ref, sz in zip(out_refs, sizes):
        o_ref[...] = p[:, off:off + sz]
        off += sz


def _inproj(x2, gain, w_all, layer, small, sizes, cuts):
    n, d = x2.shape
    tm = min(ROW_BLOCK, n)
    total = sum(sizes)
    return pl.pallas_call(
        functools.partial(_inproj_kernel, layer, sizes, cuts),
        grid=(n // tm,),
        in_specs=[pl.BlockSpec((tm, d), lambda i: (i, 0)),
                  pl.BlockSpec((1, d), lambda i: (0, 0)),
                  pl.BlockSpec(memory_space=pl.ANY),
                  pl.BlockSpec((d, LANES), lambda i: (0, 0))],
        out_specs=[pl.BlockSpec((tm, sz), lambda i: (i, 0)) for sz in sizes],
        out_shape=[jax.ShapeDtypeStruct((n, sz), F32) for sz in sizes],
        scratch_shapes=[pltpu.VMEM(w_all.shape[1:], F32), pltpu.VMEM((d, total), BF16)],
        compiler_params=_params("arbitrary"),
        name="inproj",
    )(x2, gain.reshape(1, d), w_all, small)


def _s5_operands(a_re, a_im, b_re, b_im, c_re, c_im, d_skip, log_step):
    g, p = a_re.shape
    h = b_re.shape[-1]
    step = jnp.exp(log_step)[:, None]
    mag = jnp.exp(a_re * step)
    ang = a_im * step
    lb_re, lb_im = mag * jnp.cos(ang), mag * jnp.sin(ang)
    den = a_re * a_re + a_im * a_im
    f_re = ((lb_re - 1.0) * a_re + lb_im * a_im) / den
    f_im = (lb_im * a_re - (lb_re - 1.0) * a_im) / den
    bb_re = f_re[..., None] * b_re - f_im[..., None] * b_im
    bb_im = f_re[..., None] * b_im + f_im[..., None] * b_re

    eye = jnp.eye(g, dtype=F32)
    to_state = lambda m: (m.transpose(0, 2, 1)[:, :, None, :] * eye[:, None, :, None]).reshape(g * h, g * p)
    to_out = lambda m: (m.transpose(0, 2, 1)[:, :, None, :] * eye[:, None, :, None]).reshape(g * p, g * h)
    bbd = jnp.concatenate([to_state(bb_re), to_state(bb_im)], axis=1)
    cbd = jnp.concatenate([to_out(c_re), -to_out(c_im)], axis=0)
    e = jnp.arange(1, SUBLANES + 1, dtype=F32)[:, None, None]
    m = jnp.exp(a_re * step * e)
    lam = jnp.concatenate([(m * jnp.cos(ang * e)).reshape(SUBLANES, g * p),
                           (m * jnp.sin(ang * e)).reshape(SUBLANES, g * p)], axis=1)
    return bbd, cbd, lam, d_skip.reshape(1, g * h)


def _s5_stages(u_ref, bbd_ref, cbd_ref, lam_ref, d_ref, y_ref, s_ref, carry_ref):
    n_slab, t, _ = s_ref.shape
    n_pair = n_slab // 2
    tile = SUBLANES
    pairs = range(n_pair)
    lanes = lambda j: slice(j * LANES, (j + 1) * LANES)

    u = u_ref[...]
    bu = jnp.dot(u.astype(BF16), bbd_ref[...], preferred_element_type=F32)
    for j in range(n_slab):
        s_ref[j] = bu[:, lanes(j)]
    yield
    lam = lam_ref[...]
    lr = [lam[:, lanes(j)] for j in pairs]
    li = [lam[:, lanes(n_pair + j)] for j in pairs]

    def rows(j, r):
        return s_ref.at[j, pl.ds(r, t // tile, stride=tile), :]

    pr = [rows(j, 0)[...] for j in pairs]
    pi = [rows(n_pair + j, 0)[...] for j in pairs]
    for r in range(1, tile):
        nr = [rows(j, r)[...] + lr[j][0:1] * pr[j] - li[j][0:1] * pi[j] for j in pairs]
        ni = [rows(n_pair + j, r)[...] + lr[j][0:1] * pi[j] + li[j][0:1] * pr[j] for j in pairs]
        for j in pairs:
            rows(j, r)[...] = nr[j]
            rows(n_pair + j, r)[...] = ni[j]
        pr, pi = nr, ni
        yield

    last = lambda v: jnp.broadcast_to(v[tile - 1:tile], v.shape)
    cr = [carry_ref[j] for j in pairs]
    ci = [carry_ref[n_pair + j] for j in pairs]
    for k in range(t // tile):
        r0 = k * tile
        tr = [s_ref[j, pl.ds(r0, tile), :] + lr[j] * cr[j] - li[j] * ci[j] for j in pairs]
        ti = [s_ref[n_pair + j, pl.ds(r0, tile), :] + lr[j] * ci[j] + li[j] * cr[j] for j in pairs]
        for j in pairs:
            s_ref[j, pl.ds(r0, tile), :] = tr[j]
            s_ref[n_pair + j, pl.ds(r0, tile), :] = ti[j]
        cr, ci = [last(v) for v in tr], [last(v) for v in ti]
        if k % S5_TILES_PER_STAGE == S5_TILES_PER_STAGE - 1:
            yield
    for j in pairs:
        carry_ref[j] = cr[j]
        carry_ref[n_pair + j] = ci[j]
    s_all = jnp.concatenate([s_ref[j] for j in range(n_slab)], axis=1)
    y_ref[...] = jnp.dot(s_all.astype(BF16), cbd_ref[...], preferred_element_type=F32) + d_ref[...] * u


def _conv_halo_init(pad_ref, t, first):
    halo = SUBLANES

    @pl.when(first)
    def _():
        pad_ref[0:halo, :] = jnp.zeros((halo, pad_ref.shape[1]), F32)

    @pl.when(jnp.logical_not(first))
    def _():
        pad_ref[0:halo, :] = pad_ref[t:t + halo, :]


def _causal_conv_block(x_ref, w_ref, pad_ref):
    t = x_ref.shape[0]
    halo = SUBLANES
    pad_ref[halo:halo + t, :] = x_ref[...]
    acc = w_ref[CONV_K - 1:CONV_K, :] * pad_ref[halo:halo + t, :]
    for j in range(1, CONV_K):
        acc += w_ref[CONV_K - 1 - j:CONV_K - j, :] * pad_ref[halo - j:halo - j + t, :]
    return acc


def _ssd_stages(heads, z_ref, xbc_ref, sm_ref, cw_ref, cb_ref, dtb_ref, a_ref, dexp_ref, nw_ref, ex_ref,
                o_ref, pad_ref, xc_ref, dt_ref, st_ref):
    t = xbc_ref.shape[0]
    q = min(SSD_CHUNK, t)
    width = heads * SSD_HEADDIM
    gw = SSD_NGROUPS * SSD_STATE
    xc_ref[...] = _silu(_causal_conv_block(xbc_ref, cw_ref, pad_ref) + cb_ref[...])
    dt_ref[...] = _softplus(sm_ref[...] + dtb_ref[...])
    yield

    row = _iota2((q, q), 0)
    col = _iota2((q, q), 1)
    causal = row >= col
    tril = jnp.where(causal, 1.0, 0.0).astype(BF16)
    lane = _iota2((1, width), 1)
    rep = heads // SSD_NGROUPS

    for ci in range(t // q):
        r0 = ci * q
        xs = xc_ref[pl.ds(r0, q), 0:width]
        bm = xc_ref[pl.ds(r0, q), width:width + gw]
        cm = xc_ref[pl.ds(r0, q), width + gw:width + 2 * gw]
        dt = dt_ref[pl.ds(r0, q), :]
        acs = _sel_dot(tril, dt * a_ref[...])
        acs_t = acs.T
        xdt = xs * _dot_sel(dt, ex_ref[...])
        yield
        hs = range(heads)
        bgs = [bm[:, gi * SSD_STATE:(gi + 1) * SSD_STATE] for gi in range(SSD_NGROUPS)]
        cgs = [cm[:, gi * SSD_STATE:(gi + 1) * SSD_STATE].astype(BF16) for gi in range(SSD_NGROUPS)]
        cbs = [_dot_nt(cgs[gi], bgs[gi]) for gi in range(SSD_NGROUPS)]
        xdb = xdt.astype(BF16)
        xh = [xdb[:, hh * SSD_HEADDIM:(hh + 1) * SSD_HEADDIM] for hh in hs]
        acol = [acs[:, hh:hh + 1] for hh in hs]
        alast = [acs[q - 1:q, hh:hh + 1] for hh in hs]
        decay = [jnp.where(causal, jnp.exp(jnp.where(causal, acol[hh] - acs_t[hh:hh + 1, :], 0.0)), 0.0)
                 for hh in hs]
        yield
        state = [st_ref[hh] for hh in hs]
        y_diag = [_mm((cbs[hh // rep] * decay[hh]).astype(BF16), xh[hh]) for hh in hs]
        y_off = [_mm(cgs[hh // rep], state[hh].astype(BF16)) * jnp.exp(acol[hh]) for hh in hs]
        upd = [_dot_tn(bgs[hh // rep] * jnp.exp(alast[hh] - acol[hh]), xh[hh]) for hh in hs]
        for hh in hs:
            st_ref[hh] = state[hh] * jnp.exp(alast[hh]) + upd[hh]
        yield
        y = jnp.concatenate([y_diag[hh] + y_off[hh] for hh in hs], axis=1) + dexp_ref[...] * xs
        y = y * _silu(z_ref[pl.ds(r0, q), :])
        gsz = width // SSD_NGROUPS
        y2 = y * y
        out = jnp.zeros_like(y)
        for gi in range(SSD_NGROUPS):
            m = (lane >= gi * gsz) & (lane < (gi + 1) * gsz)
            ms = jnp.sum(jnp.where(m, y2, 0.0), axis=-1, keepdims=True) / gsz
            out = jnp.where(m, y * lax.rsqrt(ms + EPS), out)
        o_ref[pl.ds(r0, q), :] = out * nw_ref[...]
        yield


def _unit_lower_solves(lms, rhs, eye, blk, chunk):
    lds = [jnp.where(blk, lm, 0.0) for lm in lms]
    offs = [(lm - ld).astype(BF16) for lm, ld in zip(lms, lds)]
    dinv = [eye - ld for ld in lds]
    pw = [ld.astype(BF16) for ld in lds]
    yield
    span = 2
    while span < GDN_INV_BLOCK:
        sq = [_mm(p, p) for p in pw]
        yield
        dinv = [_mm(d.astype(BF16), (eye + s).astype(BF16)) for d, s in zip(dinv, sq)]
        pw = [s.astype(BF16) for s in sq]
        yield
        span *= 2
    dinv = [d.astype(BF16) for d in dinv]
    ms = [_mm(d, o).astype(BF16) for d, o in zip(dinv, offs)]
    x0 = [_mm(d, r) for d, r in zip(dinv, rhs)]
    yield
    acc = x0
    for _ in range(chunk // GDN_INV_BLOCK - 1):
        acc = [x - _mm(m, a.astype(BF16)) for x, m, a in zip(x0, ms, acc)]
        yield
    return acc


def _interleave(gens):
    results = [None] * len(gens)
    live = list(range(len(gens)))
    while live:
        for i in list(live):
            try:
                next(gens[i])
            except StopIteration as done:
                results[i] = done.value
                live.remove(i)
        yield
    return results


def _delayed(gen, rounds):
    for _ in range(rounds):
        yield
    return (yield from gen)


def _run(gen):
    try:
        while True:
            next(gen)
    except StopIteration as done:
        return done.value


def _gdn_stages(heads, qkv_ref, z_ref, sm_ref, cw_ref, dtb_ref, a_ref, nw_ref, exb_ref, exg_ref, ones_ref,
                o_ref, pad_ref, st_ref):
    t = qkv_ref.shape[0]
    c = min(GDN_CHUNK, t)
    hd = GDN_HEADDIM
    width = heads * hd
    hs = range(heads)
    sl = [slice(hh * hd, (hh + 1) * hd) for hh in hs]

    def seg_sum(v):
        hi = v.astype(BF16)
        lo = (v - hi.astype(F32)).astype(BF16)
        return _mm(hi, ones_ref[...]) + _mm(lo, ones_ref[...])

    xc = _silu(_causal_conv_block(qkv_ref, cw_ref, pad_ref))
    yield
    qf = xc[:, 0:width]
    kf = xc[:, width:2 * width]
    vf = xc[:, 2 * width:3 * width]
    qf = qf * lax.rsqrt(seg_sum(qf * qf) + EPS) * (hd ** -0.5)
    kf = kf * lax.rsqrt(seg_sum(kf * kf) + EPS)
    sm = sm_ref[...]
    beta = _dot_sel(_sigmoid(sm), exb_ref[...])
    g = a_ref[...] * _softplus(sm + dtb_ref[...])
    yield

    gs = min(GDN_GROUP, t)
    row = _iota2((gs, gs), 0)
    col = _iota2((gs, gs), 1)
    same = (row // c) == (col // c)
    causal = same & (row >= col)
    strict = same & (row > col)
    eye = jnp.where(row == col, 1.0, 0.0)
    blk = (row // GDN_INV_BLOCK) == (col // GDN_INV_BLOCK)
    tril_sel = jnp.where(causal, 1.0, 0.0).astype(BF16)
    same_sel = jnp.where(same, 1.0, 0.0).astype(BF16)
    nt = (((1,), (1,)), ((), ()))

    def gates(gi):
        rg = slice(gi * gs, (gi + 1) * gs)
        gcs = _sel_dot(tril_sel, g[rg])
        gtot = _sel_dot(same_sel, g[rg])
        gcs_t = gcs.T
        yield
        gw = _dot_sel(gcs, exg_ref[...])
        gtw = _dot_sel(gtot, exg_ref[...])
        gexp = jnp.exp(gw)
        kb = kf[rg] * beta[rg]
        yield
        decay = []
        for hh in hs:
            lg = 2 * heads + hh
            diff = jnp.where(causal, gcs[:, lg:lg + 1] - gcs_t[lg:lg + 1, :], 0.0)
            decay.append(jnp.where(causal, jnp.exp(diff), 0.0))
            yield
        return dict(kb=kb, vb=vf[rg] * beta[rg], kcd_in=kb * gexp, qd=(qf[rg] * gexp).astype(BF16),
                    kend=(kf[rg] * jnp.exp(gtw - gw)).astype(BF16), cdec=jnp.exp(gtw), decay=decay,
                    kfb=kf[rg].astype(BF16), qfb=qf[rg].astype(BF16))

    def solve(p):
        both = jnp.concatenate([p["kb"].astype(BF16), p["qfb"]], axis=0)
        kq = [lax.dot_general(both[:, sl[hh]], p["kfb"][:, sl[hh]], nt, preferred_element_type=F32) for hh in hs]
        yield
        lms = [jnp.where(strict, kq[hh][0:gs] * p["decay"][hh], 0.0) for hh in hs]
        qk = [(kq[hh][gs:2 * gs] * p["decay"][hh]).astype(BF16) for hh in hs]
        rhs = [jnp.concatenate([p["vb"][:, sl[hh]], p["kcd_in"][:, sl[hh]]], axis=1).astype(BF16) for hh in hs]
        vk = yield from _unit_lower_solves(lms, rhs, eye, blk, c)
        return qk, vk

    def recur(state, p, qk, vk):
        outs = []
        for ci in range(gs // c):
            rs = slice(ci * c, (ci + 1) * c)
            lhs = [jnp.concatenate([vk[hh][rs, hd:2 * hd].astype(BF16), p["qd"][rs, sl[hh]]], axis=0)
                   for hh in hs]
            sb = [s.astype(BF16) for s in state]
            both = [_mm(lhs[hh], sb[hh]) for hh in hs]
            yield
            v_new = [(vk[hh][rs, 0:hd] - both[hh][0:c]).astype(BF16) for hh in hs]
            o = [both[hh][c:2 * c] + _mm(qk[hh][rs, ci * c:(ci + 1) * c], v_new[hh]) for hh in hs]
            upd = [lax.dot_general(p["kend"][rs, sl[hh]], v_new[hh], (((0,), (0,)), ((), ())),
                                   preferred_element_type=F32) for hh in hs]
            state = [state[hh] * p["cdec"][ci * c:ci * c + 1, sl[hh]] + upd[hh] for hh in hs]
            outs.append(jnp.concatenate(o, axis=1))
            yield
        return state, outs

    n_groups = t // gs
    parts, solved, outs = {}, {}, []
    state = [st_ref[hh] for hh in hs]
    for k in range(n_groups + 2):
        jobs = {}
        if k < n_groups:
            jobs["gates"] = gates(k)
        if 0 <= k - 1 < n_groups:
            jobs["solve"] = solve(parts[k - 1])
        if 0 <= k - 2 < n_groups:
            jobs["recur"] = recur(state, parts[k - 2], *solved[k - 2])
        done = dict(zip(jobs, (yield from _interleave(list(jobs.values())))))
        if "gates" in done:
            parts[k] = done["gates"]
        if "solve" in done:
            solved[k - 1] = done["solve"]
        if "recur" in done:
            state, o_g = done["recur"]
            outs += o_g
    for hh in hs:
        st_ref[hh] = state[hh]
    yield
    o = jnp.concatenate(outs, axis=0)
    o = o * lax.rsqrt(seg_sum(o * o) / hd + EPS) * nw_ref[...]
    o_ref[...] = o * _silu(z_ref[...])


_N_S5_IN, _N_SSD_IN, _N_GDN_IN, _N_OUT_IN = 5, 10, 10, 4


def _mixers_kernel(ssd_heads, gdn_heads, *refs):
    n_mix = _N_S5_IN + _N_SSD_IN + _N_GDN_IN
    s5_in = refs[:_N_S5_IN]
    ssd_in = refs[_N_S5_IN:_N_S5_IN + _N_SSD_IN]
    gdn_in = refs[_N_S5_IN + _N_SSD_IN:n_mix]
    x_ref, wglu_ref, s5n_ref, wout_ref = refs[n_mix:n_mix + _N_OUT_IN]
    o_ref = refs[n_mix + _N_OUT_IN]
    (y_s5, y_ssd, y_gdn, s5_s, s5_carry, ssd_pad, ssd_xc, ssd_dt, ssd_st, gdn_pad,
     gdn_st) = refs[n_mix + _N_OUT_IN + 1:]
    t = y_s5.shape[0]
    first = pl.program_id(1) == 0

    @pl.when(first)
    def _():
        s5_carry[...] = jnp.zeros(s5_carry.shape, F32)
        ssd_st[...] = jnp.zeros(ssd_st.shape, F32)
        gdn_st[...] = jnp.zeros(gdn_st.shape, F32)

    _conv_halo_init(ssd_pad, t, first)
    _conv_halo_init(gdn_pad, t, first)
    _run(_interleave([
        _gdn_stages(gdn_heads, *gdn_in, y_gdn, gdn_pad, gdn_st),
        _delayed(_ssd_stages(ssd_heads, *ssd_in, y_ssd, ssd_pad, ssd_xc, ssd_dt, ssd_st), MIXER_DELAY_ROUNDS),
        _delayed(_s5_stages(*s5_in, y_s5, s5_s, s5_carry), MIXER_DELAY_ROUNDS),
    ]))
    y = _gelu_tanh(y_s5[...])
    y = y * _sigmoid(jnp.dot(y.astype(BF16), wglu_ref[...], preferred_element_type=F32))
    y = _rms(y, s5n_ref[...])
    mix = jnp.concatenate([y, y_ssd[...], y_gdn[...]], axis=1)
    o_ref[...] = x_ref[...] + jnp.dot(mix.astype(BF16), wout_ref[...], preferred_element_type=F32)


def _mixers(x2, u_s5, s5_ops, z_ssd, xbc, small, ssd_p, qkv, z_gdn, gdn_p, w_glu, s5_norm, w_out, bsz, seq):
    bbd, cbd, lam, d_s5 = s5_ops
    d = x2.shape[1]
    conv_w, conv_b, dt_bias, a_log, d_skip, norm_w = ssd_p
    g_conv_w, g_a_log, g_dt_bias, g_norm_w = gdn_p
    n, s5_w = u_s5.shape
    s2 = bbd.shape[1]
    ssd_w, ssd_cdim, ssd_h = z_ssd.shape[1], xbc.shape[1], a_log.shape[0]
    gdn_w, gdn_cdim, gdn_h = z_gdn.shape[1], qkv.shape[1], g_a_log.shape[0]
    t = min(SEQ_BLOCK, seq)
    nblk = seq // t
    tok = lambda b, c: (b * nblk + c, 0)
    const = lambda b, c: (0, 0)
    rows = lambda w: pl.BlockSpec((t, w), tok)
    whole = lambda a: pl.BlockSpec(a.shape, const)
    pad = lambda v, off: jnp.zeros((1, LANES), F32).at[0, off:off + v.shape[0]].set(v)
    lanes = jnp.arange(LANES)[:, None]
    ssd_ex = (lanes == (jnp.arange(ssd_w)[None, :] // SSD_HEADDIM)).astype(BF16)
    head_of = jnp.arange(gdn_w)[None, :] // GDN_HEADDIM
    s5_args = [bbd.astype(BF16), cbd.astype(BF16), lam, d_s5]
    ssd_args = [conv_w, conv_b.reshape(1, ssd_cdim), pad(dt_bias, 0), pad(-jnp.exp(a_log), 0),
                jnp.repeat(d_skip, SSD_HEADDIM).reshape(1, ssd_w), norm_w.reshape(1, ssd_w), ssd_ex]
    gdn_args = [g_conv_w, pad(g_dt_bias, 2 * gdn_h), pad(-jnp.exp(g_a_log), 2 * gdn_h),
                jnp.tile(g_norm_w, gdn_h).reshape(1, gdn_w), (lanes == gdn_h + head_of).astype(BF16),
                (lanes == 2 * gdn_h + head_of).astype(BF16), (head_of.T == head_of).astype(BF16)]
    out_args = [w_glu.astype(BF16), s5_norm.reshape(1, s5_w), w_out.astype(BF16)]
    in_specs = ([rows(s5_w)] + [whole(a) for a in s5_args]
                + [rows(ssd_w), rows(ssd_cdim), rows(LANES)] + [whole(a) for a in ssd_args]
                + [rows(gdn_cdim), rows(gdn_w), rows(LANES)] + [whole(a) for a in gdn_args]
                + [rows(d)] + [whole(a) for a in out_args])
    assert len(in_specs) == _N_S5_IN + _N_SSD_IN + _N_GDN_IN + _N_OUT_IN
    return pl.pallas_call(
        functools.partial(_mixers_kernel, ssd_h, gdn_h),
        grid=(bsz, nblk),
        in_specs=in_specs,
        out_specs=rows(d),
        out_shape=jax.ShapeDtypeStruct((n, d), F32),
        scratch_shapes=[pltpu.VMEM((t, s5_w), F32), pltpu.VMEM((t, ssd_w), F32), pltpu.VMEM((t, gdn_w), F32),
                        pltpu.VMEM((s2 // LANES, t, LANES), F32),
                        pltpu.VMEM((s2 // LANES, SUBLANES, LANES), F32),
                        pltpu.VMEM((SUBLANES + t, ssd_cdim), F32),
                        pltpu.VMEM((t, ssd_cdim), F32),
                        pltpu.VMEM((t, LANES), F32),
                        pltpu.VMEM((ssd_h, SSD_STATE, SSD_HEADDIM), F32),
                        pltpu.VMEM((SUBLANES + t, gdn_cdim), F32),
                        pltpu.VMEM((gdn_h, GDN_HEADDIM, GDN_HEADDIM), F32)],
        compiler_params=_params("parallel", "arbitrary"),
        name="mixers",
    )(u_s5, *s5_args, z_ssd, xbc, small, *ssd_args, qkv, z_gdn, small, *gdn_args, x2, *out_args)


def _ffn_kernel(final, x_ref, g_ref, wg_ref, wu_ref, wd_ref, gf_ref, o_ref, h_ref, acc_ref):
    f = pl.program_id(1)

    @pl.when(f == 0)
    def _():
        h_ref[...] = _rms(x_ref[...], g_ref[...]).astype(BF16)
        acc_ref[...] = jnp.zeros(acc_ref.shape, F32)

    h = h_ref[...]
    a = jnp.dot(h, wg_ref[...].astype(BF16), preferred_element_type=F32)
    u = jnp.dot(h, wu_ref[...].astype(BF16), preferred_element_type=F32)
    acc_ref[...] += jnp.dot((_silu(a) * u).astype(BF16), wd_ref[...].astype(BF16), preferred_element_type=F32)

    @pl.when(f == pl.num_programs(1) - 1)
    def _():
        out = x_ref[...] + acc_ref[...]
        o_ref[...] = _rms(out, gf_ref[...]) if final else out


def _ffn(x2, gain, w_gate, w_up, w_down, final_gain):
    n, d = x2.shape
    ff = w_gate.shape[1]
    tm = min(FFN_ROW_BLOCK, n)
    tf = min(FFN_COL_BLOCK, ff)
    final = final_gain is not None
    gf = (final_gain if final else gain).reshape(1, d)
    return pl.pallas_call(
        functools.partial(_ffn_kernel, final),
        grid=(n // tm, ff // tf),
        in_specs=[pl.BlockSpec((tm, d), lambda i, f: (i, 0)),
                  pl.BlockSpec((1, d), lambda i, f: (0, 0)),
                  pl.BlockSpec((d, tf), lambda i, f: (0, f)),
                  pl.BlockSpec((d, tf), lambda i, f: (0, f)),
                  pl.BlockSpec((tf, d), lambda i, f: (f, 0)),
                  pl.BlockSpec((1, d), lambda i, f: (0, 0))],
        out_specs=pl.BlockSpec((tm, d), lambda i, f: (i, 0)),
        out_shape=jax.ShapeDtypeStruct((n, d), F32),
        scratch_shapes=[pltpu.VMEM((tm, d), BF16), pltpu.VMEM((tm, d), F32)],
        compiler_params=_params("parallel", "arbitrary"),
        name="ffn",
    )(x2, gain.reshape(1, d), w_gate, w_up, w_down, gf)


_R_E1, _R_E2, _R_W1, _R_W2, _R_RANK1, _R_RANK2 = range(6)


def _router_kernel(n_exp, x_ref, g_ref, wr_ref, tri_ref, hn_ref, rt_ref, cnt_ref, base_ref):
    @pl.when(pl.program_id(0) == 0)
    def _():
        base_ref[...] = jnp.zeros(base_ref.shape, F32)

    hn = _rms(x_ref[...], g_ref[...])
    hn_ref[...] = hn
    h_hi = hn.astype(BF16)
    h_lo = (hn - h_hi.astype(F32)).astype(BF16)
    w_hi = wr_ref[...].astype(BF16)
    w_lo = (wr_ref[...] - w_hi.astype(F32)).astype(BF16)
    logits = _mm(h_hi, w_hi) + _mm(h_lo, w_hi) + _mm(h_hi, w_lo)
    lane = _iota2(logits.shape, 1)
    neg = jnp.float32(-jnp.inf)
    logits = jnp.where(lane < n_exp, logits, neg)
    m1 = jnp.max(logits, axis=-1, keepdims=True)
    i1 = jnp.min(jnp.where(logits == m1, lane, LANES), axis=-1, keepdims=True)
    rest = jnp.where(lane == i1, neg, logits)
    m2 = jnp.max(rest, axis=-1, keepdims=True)
    i2 = jnp.min(jnp.where(rest == m2, lane, LANES), axis=-1, keepdims=True)
    w1 = 1.0 / (1.0 + jnp.exp(m2 - m1))
    w2 = 1.0 / (1.0 + jnp.exp(m1 - m2))
    hit = (lane == i1) | (lane == i2)
    onehot = jnp.where(hit, 1.0, 0.0)
    before = jnp.dot(tri_ref[...], onehot.astype(BF16), preferred_element_type=F32) + base_ref[...]
    r1 = jnp.sum(jnp.where(lane == i1, before, 0.0), axis=-1, keepdims=True)
    r2 = jnp.sum(jnp.where(lane == i2, before, 0.0), axis=-1, keepdims=True)
    cols = {_R_E1: i1.astype(F32), _R_E2: i2.astype(F32), _R_W1: w1, _R_W2: w2, _R_RANK1: r1, _R_RANK2: r2}
    rt = jnp.zeros(logits.shape, F32)
    for k, v in cols.items():
        rt = jnp.where(lane == k, v, rt)
    rt_ref[...] = rt
    base_ref[...] += jnp.sum(onehot, axis=0, keepdims=True)
    cnt_ref[...] = base_ref[...]


def _row_copy(src_ref, src_row, dst_ref, dst_row, sem):
    return pltpu.make_async_copy(src_ref.at[pl.ds(src_row, 1)], dst_ref.at[pl.ds(dst_row, 1)], sem)


def _dispatch_kernel(n_exp, rb, s1_ref, s2_ref, fill_ref, hn_ref, xs_ref, zero_ref, sem, zsem):
    tm = hn_ref.shape[0]
    base = pl.program_id(0) * tm

    @pl.when(pl.program_id(0) == 0)
    def _():
        zero_ref[...] = jnp.zeros(zero_ref.shape, F32)
        for e in range(n_exp):
            lo, hi = fill_ref[e], fill_ref[n_exp + e]

            def start_row(r, carry):
                _row_copy(zero_ref, 0, xs_ref, r, zsem).start()
                return carry

            def wait_row(r, carry):
                _row_copy(zero_ref, 0, xs_ref, r, zsem).wait()
                return carry

            lax.fori_loop(lo, hi, start_row, 0)
            lax.fori_loop(lo, hi, wait_row, 0)

        def block_copy(b):
            return pltpu.make_async_copy(zero_ref, xs_ref.at[pl.ds(pl.multiple_of(b * rb, rb), rb)], zsem)

        def start_block(b, carry):
            block_copy(b).start()
            return carry

        def wait_block(b, carry):
            block_copy(b).wait()
            return carry

        n_blocks = xs_ref.shape[0] // rb
        lax.fori_loop(fill_ref[2 * n_exp], n_blocks, start_block, 0)
        lax.fori_loop(fill_ref[2 * n_exp], n_blocks, wait_block, 0)

    def issue(r, carry):
        _row_copy(hn_ref, r, xs_ref, s1_ref[base + r], sem).start()
        _row_copy(hn_ref, r, xs_ref, s2_ref[base + r], sem).start()
        return carry

    lax.fori_loop(0, tm, issue, 0, unroll=DMA_ISSUE_UNROLL)
    for _ in range(TOP_K):
        pltpu.make_async_copy(hn_ref, xs_ref.at[pl.ds(0, tm)], sem).wait()


def _expert_ffn_kernel(be_ref, bv_ref, xs_ref, wg_ref, wu_ref, wd_ref, ys_ref, h_ref, acc_ref):
    i = pl.program_id(0)
    f = pl.program_id(1)
    last = pl.num_programs(1) - 1
    valid = bv_ref[i]

    @pl.when(valid > 0)
    def _():
        @pl.when(f == 0)
        def _():
            row = _iota2((xs_ref.shape[0], 1), 0)
            h_ref[...] = jnp.where(row < valid, xs_ref[...], 0.0).astype(BF16)
            acc_ref[...] = jnp.zeros(acc_ref.shape, F32)

        h = h_ref[...]
        a = jnp.dot(h, wg_ref[0].astype(BF16), preferred_element_type=F32)
        u = jnp.dot(h, wu_ref[0].astype(BF16), preferred_element_type=F32)
        acc_ref[...] += jnp.dot((_silu(a) * u).astype(BF16), wd_ref[0].astype(BF16), preferred_element_type=F32)

        @pl.when(f == last)
        def _():
            ys_ref[...] = acc_ref[...]

    @pl.when((valid == 0) & (f == last))
    def _():
        ys_ref[...] = jnp.zeros(ys_ref.shape, F32)


def _combine_kernel(final, s1_ref, s2_ref, x_ref, rt_ref, gf_ref, ys_ref, o_ref, b1_ref, b2_ref, sem):
    tm = x_ref.shape[0]
    i = pl.program_id(0)
    cur = i % 2

    def issue(block, buf):
        def one(r, carry):
            _row_copy(ys_ref, s1_ref[block * tm + r], b1_ref.at[buf], r, sem.at[buf]).start()
            _row_copy(ys_ref, s2_ref[block * tm + r], b2_ref.at[buf], r, sem.at[buf]).start()
            return carry

        lax.fori_loop(0, tm, one, 0, unroll=DMA_ISSUE_UNROLL)

    @pl.when(i == 0)
    def _():
        issue(0, 0)

    @pl.when(i + 1 < pl.num_programs(0))
    def _():
        issue(i + 1, 1 - cur)

    for b_ref in (b1_ref, b2_ref):
        pltpu.make_async_copy(ys_ref.at[pl.ds(0, tm)], b_ref.at[cur], sem.at[cur]).wait()
    rt = rt_ref[...]
    out = (x_ref[...] + rt[:, _R_W1:_R_W1 + 1] * b1_ref[cur] + rt[:, _R_W2:_R_W2 + 1] * b2_ref[cur])
    o_ref[...] = _rms(out, gf_ref[...]) if final else out


def _moe(x2, gain, w_router, w_gate, w_up, w_down, final_gain):
    n, d = x2.shape
    n_exp, _, ff = w_gate.shape
    final = final_gain is not None
    gf = (final_gain if final else gain).reshape(1, d)
    wr = jnp.zeros((d, LANES), F32).at[:, :n_exp].set(w_router)

    tr = min(ROUTER_BLOCK, n)
    tri = (jnp.arange(tr)[:, None] > jnp.arange(tr)[None, :]).astype(BF16)
    hn, route, counts = pl.pallas_call(
        functools.partial(_router_kernel, n_exp),
        grid=(n // tr,),
        in_specs=[pl.BlockSpec((tr, d), lambda i: (i, 0)),
                  pl.BlockSpec((1, d), lambda i: (0, 0)),
                  pl.BlockSpec((d, LANES), lambda i: (0, 0)),
                  pl.BlockSpec((tr, tr), lambda i: (0, 0))],
        out_specs=[pl.BlockSpec((tr, d), lambda i: (i, 0)),
                   pl.BlockSpec((tr, LANES), lambda i: (i, 0)),
                   pl.BlockSpec((1, LANES), lambda i: (0, 0))],
        out_shape=[jax.ShapeDtypeStruct((n, d), F32),
                   jax.ShapeDtypeStruct((n, LANES), F32),
                   jax.ShapeDtypeStruct((1, LANES), F32)],
        scratch_shapes=[pltpu.VMEM((1, LANES), F32)],
        compiler_params=_params("arbitrary"),
        name="moe_router",
    )(x2, gain.reshape(1, d), wr, tri)

    rb = min(MOE_ROW_BLOCK, TOP_K * n)
    n_slots = TOP_K * n + n_exp * rb
    n_blocks = n_slots // rb
    cnt = counts[0, :n_exp].astype(jnp.int32)
    padded = ((cnt + rb - 1) // rb) * rb
    ends = jnp.cumsum(padded)
    starts = ends - padded
    as_int = lambda k: route[:, k].astype(jnp.int32)
    slot1 = starts[as_int(_R_E1)] + as_int(_R_RANK1)
    slot2 = starts[as_int(_R_E2)] + as_int(_R_RANK2)
    block_row = jnp.arange(n_blocks, dtype=jnp.int32) * rb
    block_exp = jnp.minimum(jnp.sum(block_row[:, None] >= ends[None, :], axis=1), n_exp - 1).astype(jnp.int32)
    block_valid = jnp.clip(cnt[block_exp] - (block_row - starts[block_exp]), 0, rb).astype(jnp.int32)

    tdp = min(DISPATCH_BLOCK, n)
    fill = jnp.concatenate([starts + cnt, ends, ends[-1:] // rb]).astype(jnp.int32)
    xs = pl.pallas_call(
        functools.partial(_dispatch_kernel, n_exp, rb),
        grid_spec=pltpu.PrefetchScalarGridSpec(
            num_scalar_prefetch=3,
            grid=(n // tdp,),
            in_specs=[pl.BlockSpec((tdp, d), lambda i, s1, s2, fl: (i, 0))],
            out_specs=pl.BlockSpec(memory_space=pl.ANY),
            scratch_shapes=[pltpu.VMEM((rb, d), F32), pltpu.SemaphoreType.DMA, pltpu.SemaphoreType.DMA]),
        out_shape=jax.ShapeDtypeStruct((n_slots, d), F32),
        compiler_params=_params("arbitrary"),
        name="moe_dispatch",
    )(slot1, slot2, fill, hn)

    tf = min(FFN_COL_BLOCK, ff)
    ys = pl.pallas_call(
        _expert_ffn_kernel,
        grid_spec=pltpu.PrefetchScalarGridSpec(
            num_scalar_prefetch=2,
            grid=(n_blocks, ff // tf),
            in_specs=[pl.BlockSpec((rb, d), lambda i, f, be, bv: (i, 0)),
                      pl.BlockSpec((1, d, tf), lambda i, f, be, bv: (be[i], 0, f)),
                      pl.BlockSpec((1, d, tf), lambda i, f, be, bv: (be[i], 0, f)),
                      pl.BlockSpec((1, tf, d), lambda i, f, be, bv: (be[i], f, 0))],
            out_specs=pl.BlockSpec((rb, d), lambda i, f, be, bv: (i, 0)),
            scratch_shapes=[pltpu.VMEM((rb, d), BF16), pltpu.VMEM((rb, d), F32)]),
        out_shape=jax.ShapeDtypeStruct((n_slots, d), F32),
        compiler_params=_params("parallel", "arbitrary"),
        name="moe_experts",
    )(block_exp, block_valid, xs, w_gate, w_up, w_down)

    tc = min(COMBINE_BLOCK, n)
    return pl.pallas_call(
        functools.partial(_combine_kernel, final),
        grid_spec=pltpu.PrefetchScalarGridSpec(
            num_scalar_prefetch=2,
            grid=(n // tc,),
            in_specs=[pl.BlockSpec((tc, d), lambda i, s1, s2: (i, 0)),
                      pl.BlockSpec((tc, LANES), lambda i, s1, s2: (i, 0)),
                      pl.BlockSpec((1, d), lambda i, s1, s2: (0, 0)),
                      pl.BlockSpec(memory_space=pl.ANY)],
            out_specs=pl.BlockSpec((tc, d), lambda i, s1, s2: (i, 0)),
            scratch_shapes=[pltpu.VMEM((2, tc, d), F32), pltpu.VMEM((2, tc, d), F32),
                            pltpu.SemaphoreType.DMA((2,))]),
        out_shape=jax.ShapeDtypeStruct((n, d), F32),
        compiler_params=_params("arbitrary"),
        name="moe_combine",
    )(slot1, slot2, x2, route, gf, ys)


def _arrange_in_proj(w_in, s5_w, ssd_w, ssd_cdim, ssd_h, gdn_cdim, gdn_w, gdn_h):
    sizes = (s5_w, ssd_w, ssd_cdim, ssd_h, gdn_cdim, gdn_w, gdn_h, gdn_h)
    offs = [int(o) for o in np.cumsum((0,) + sizes)]
    seg = lambda i: w_in[:, offs[i]:offs[i + 1]]
    small = jnp.concatenate([seg(3), seg(6), seg(7)], axis=1)
    small = jnp.pad(small, ((0, 0), (0, LANES - small.shape[1])))
    cuts = ((offs[0], offs[3]), (offs[4], offs[5]), (offs[5], offs[6]))
    return small, (s5_w, ssd_w, ssd_cdim, gdn_cdim, gdn_w, LANES), cuts


def kernel(x, norm_mix, w_in, w_out, s5_a_re, s5_a_im, s5_b_re, s5_b_im, s5_c_re, s5_c_im, s5_d, s5_log_step, s5_w_glu, s5_norm, ssd_conv_w, ssd_conv_b, ssd_dt_bias, ssd_a_log, ssd_d, ssd_norm, gdn_conv_w, gdn_a_log, gdn_dt_bias, gdn_norm, norm_ffn, ff_w_gate, ff_w_up, ff_w_down, moe_router, moe_w_gate, moe_w_up, moe_w_down, norm_final):
    bsz, seq, d = x.shape
    depth = norm_mix.shape[0]
    s5_w = s5_w_glu.shape[-1]
    ssd_w, ssd_h, ssd_cdim = ssd_norm.shape[-1], ssd_a_log.shape[-1], ssd_conv_w.shape[-1]
    gdn_h, gdn_cdim = gdn_a_log.shape[-1], gdn_conv_w.shape[-1]
    gdn_w = gdn_h * GDN_HEADDIM
    xr = x.reshape(bsz * seq, d)
    for layer in range(depth):
        w_small, sizes, cuts = _arrange_in_proj(w_in[layer], s5_w, ssd_w, ssd_cdim, ssd_h, gdn_cdim, gdn_w, gdn_h)
        u_s5, z_ssd, xbc, qkv, z_gdn, small = _inproj(xr, norm_mix[layer], w_in, layer, w_small, sizes, cuts)
        s5_ops = _s5_operands(s5_a_re[layer], s5_a_im[layer], s5_b_re[layer], s5_b_im[layer], s5_c_re[layer],
                              s5_c_im[layer], s5_d[layer], s5_log_step[layer])
        ssd_p = (ssd_conv_w[layer], ssd_conv_b[layer], ssd_dt_bias[layer], ssd_a_log[layer], ssd_d[layer],
                 ssd_norm[layer])
        gdn_p = (gdn_conv_w[layer], gdn_a_log[layer], gdn_dt_bias[layer], gdn_norm[layer])
        xr = _mixers(xr, u_s5, s5_ops, z_ssd, xbc, small, ssd_p, qkv, z_gdn, gdn_p, s5_w_glu[layer],
                     s5_norm[layer], w_out[layer], bsz, seq)
        final_gain = norm_final if layer == depth - 1 else None
        i = layer // 2
        if layer % 2 == 0:
            xr = _ffn(xr, norm_ffn[layer], ff_w_gate[i], ff_w_up[i], ff_w_down[i], final_gain)
        else:
            xr = _moe(xr, norm_ffn[layer], moe_router[i], moe_w_gate[i], moe_w_up[i], moe_w_down[i], final_gain)
    return xr.reshape(bsz, seq, d)
```

```python
import functools
import math

import jax
import jax.numpy as jnp
import numpy as np
from jax import lax
from jax.experimental import pallas as pl
from jax.experimental.pallas import tpu as pltpu

F32 = jnp.float32
BF16 = jnp.bfloat16
EPS = 1e-6

S5_GROUP = 16
S5_STATE = 64
SSD_HEADDIM = 64
SSD_NGROUPS = 2
SSD_STATE = 128
GDN_HEADDIM = 64
CONV_K = 4
TOP_K = 2

LANES = 128
SUBLANES = 8
VMEM_LIMIT_BYTES = 56 * 1024 * 1024

SSD_CHUNK = 128
GDN_CHUNK = 64
GDN_INV_BLOCK = 16
GDN_GROUP = 256
SEQ_BLOCK = 512
S5_TILES_PER_STAGE = 4
MIXER_DELAY_ROUNDS = 10
ROW_BLOCK = 512
OUT_ROW_BLOCK = 1024
FFN_ROW_BLOCK = 1024
FFN_COL_BLOCK = 512
ROUTER_BLOCK = 512
DISPATCH_BLOCK = 1024
MOE_ROW_BLOCK = 1024
COMBINE_BLOCK = 512
DMA_ISSUE_UNROLL = 16


def _params(*semantics):
    return pltpu.CompilerParams(dimension_semantics=semantics, vmem_limit_bytes=VMEM_LIMIT_BYTES)


def _dot(a, b):
    return jnp.dot(a.astype(BF16), b.astype(BF16), preferred_element_type=F32)


def _mm(a, b):
    return jnp.dot(a, b, preferred_element_type=F32)


def _dot_nt(a, b):
    return lax.dot_general(a.astype(BF16), b.astype(BF16), (((1,), (1,)), ((), ())),
                           preferred_element_type=F32)


def _dot_tn(a, b):
    return lax.dot_general(a.astype(BF16), b.astype(BF16), (((0,), (0,)), ((), ())),
                           preferred_element_type=F32)


def _split3(a):
    hi = a.astype(BF16)
    r1 = a - hi.astype(F32)
    mid = r1.astype(BF16)
    lo = (r1 - mid.astype(F32)).astype(BF16)
    return hi, mid, lo


def _dot_sel(a, sel):
    hi, mid, lo = _split3(a)
    out = jnp.dot(hi, sel, preferred_element_type=F32)
    out += jnp.dot(mid, sel, preferred_element_type=F32)
    out += jnp.dot(lo, sel, preferred_element_type=F32)
    return out


def _sel_dot(sel, a):
    hi, mid, lo = _split3(a)
    out = jnp.dot(sel, hi, preferred_element_type=F32)
    out += jnp.dot(sel, mid, preferred_element_type=F32)
    out += jnp.dot(sel, lo, preferred_element_type=F32)
    return out


def _sigmoid(x):
    return 1.0 / (1.0 + jnp.exp(-x))


def _silu(x):
    return x * _sigmoid(x)


def _softplus(x):
    return jnp.maximum(x, 0.0) + jnp.log(1.0 + jnp.exp(-jnp.abs(x)))


def _gelu_tanh(x):
    c = math.sqrt(2.0 / math.pi)
    return 0.5 * x * (1.0 + jnp.tanh(c * (x + 0.044715 * (x * x * x))))


def _rms(x, w):
    return x * lax.rsqrt(jnp.mean(x * x, axis=-1, keepdims=True) + EPS) * w


def _iota2(shape, axis):
    return lax.broadcasted_iota(jnp.int32, shape, axis)


def _inproj_kernel(layer, sizes, cuts, x_ref, g_ref, w_hbm, small_ref, *rest):
    out_refs = rest[:len(sizes)]
    raw_ref, w_ref = rest[len(sizes):]

    @pl.when(pl.program_id(0) == 0)
    def _():
        pltpu.sync_copy(w_hbm.at[layer], raw_ref)
        ncols = raw_ref.shape[1]
        off = 0
        for lo, hi in cuts:
            a = (lo // LANES) * LANES
            b = min(-(-hi // LANES) * LANES, ncols)
            w_ref[:, off:off + hi - lo] = raw_ref[:, a:b][:, lo - a:hi - a].astype(BF16)
            off += hi - lo
        w_ref[:, off:off + LANES] = small_ref[...].astype(BF16)

    h = _rms(x_ref[...], g_ref[...])
    p = jnp.dot(h.astype(BF16), w_ref[...], preferred_element_type=F32)
    off = 0
    for o_ref, sz in zip(out_refs, sizes):
        o_ref[...] = p[:, off:off + sz]
        off += sz


def _inproj(x2, gain, w_all, layer, small, sizes, cuts):
    n, d = x2.shape
    tm = min(ROW_BLOCK, n)
    total = sum(sizes)
    return pl.pallas_call(
        functools.partial(_inproj_kernel, layer, sizes, cuts),
        grid=(n // tm,),
        in_specs=[pl.BlockSpec((tm, d), lambda i: (i, 0)),
                  pl.BlockSpec((1, d), lambda i: (0, 0)),
                  pl.BlockSpec(memory_space=pl.ANY),
                  pl.BlockSpec((d, LANES), lambda i: (0, 0))],
        out_specs=[pl.BlockSpec((tm, sz), lambda i: (i, 0)) for sz in sizes],
        out_shape=[jax.ShapeDtypeStruct((n, sz), F32) for sz in sizes],
        scratch_shapes=[pltpu.VMEM(w_all.shape[1:], F32), pltpu.VMEM((d, total), BF16)],
        compiler_params=_params("arbitrary"),
        name="inproj",
    )(x2, gain.reshape(1, d), w_all, small)


def _s5_operands(a_re, a_im, b_re, b_im, c_re, c_im, d_skip, log_step):
    g, p = a_re.shape
    h = b_re.shape[-1]
    step = jnp.exp(log_step)[:, None]
    mag = jnp.exp(a_re * step)
    ang = a_im * step
    lb_re, lb_im = mag * jnp.cos(ang), mag * jnp.sin(ang)
    den = a_re * a_re + a_im * a_im
    f_re = ((lb_re - 1.0) * a_re + lb_im * a_im) / den
    f_im = (lb_im * a_re - (lb_re - 1.0) * a_im) / den
    bb_re = f_re[..., None] * b_re - f_im[..., None] * b_im
    bb_im = f_re[..., None] * b_im + f_im[..., None] * b_re

    eye = jnp.eye(g, dtype=F32)
    to_state = lambda m: (m.transpose(0, 2, 1)[:, :, None, :] * eye[:, None, :, None]).reshape(g * h, g * p)
    to_out = lambda m: (m.transpose(0, 2, 1)[:, :, None, :] * eye[:, None, :, None]).reshape(g * p, g * h)
    bbd = jnp.concatenate([to_state(bb_re), to_state(bb_im)], axis=1)
    cbd = jnp.concatenate([to_out(c_re), -to_out(c_im)], axis=0)
    e = jnp.arange(1, SUBLANES + 1, dtype=F32)[:, None, None]
    m = jnp.exp(a_re * step * e)
    lam = jnp.concatenate([(m * jnp.cos(ang * e)).reshape(SUBLANES, g * p),
                           (m * jnp.sin(ang * e)).reshape(SUBLANES, g * p)], axis=1)
    return bbd, cbd, lam, d_skip.reshape(1, g * h)


def _s5_stages(u_ref, bbd_ref, cbd_ref, lam_ref, d_ref, y_ref, s_ref, carry_ref):
    n_slab, t, _ = s_ref.shape
    n_pair = n_slab // 2
    tile = SUBLANES
    pairs = range(n_pair)
    lanes = lambda j: slice(j * LANES, (j + 1) * LANES)

    u = u_ref[...]
    bu = jnp.dot(u.astype(BF16), bbd_ref[...], preferred_element_type=F32)
    for j in range(n_slab):
        s_ref[j] = bu[:, lanes(j)]
    yield
    lam = lam_ref[...]
    lr = [lam[:, lanes(j)] for j in pairs]
    li = [lam[:, lanes(n_pair + j)] for j in pairs]

    def rows(j, r):
        return s_ref.at[j, pl.ds(r, t // tile, stride=tile), :]

    pr = [rows(j, 0)[...] for j in pairs]
    pi = [rows(n_pair + j, 0)[...] for j in pairs]
    for r in range(1, tile):
        nr = [rows(j, r)[...] + lr[j][0:1] * pr[j] - li[j][0:1] * pi[j] for j in pairs]
        ni = [rows(n_pair + j, r)[...] + lr[j][0:1] * pi[j] + li[j][0:1] * pr[j] for j in pairs]
        for j in pairs:
            rows(j, r)[...] = nr[j]
            rows(n_pair + j, r)[...] = ni[j]
        pr, pi = nr, ni
        yield

    last = lambda v: jnp.broadcast_to(v[tile - 1:tile], v.shape)
    cr = [carry_ref[j] for j in pairs]
    ci = [carry_ref[n_pair + j] for j in pairs]
    for k in range(t // tile):
        r0 = k * tile
        tr = [s_ref[j, pl.ds(r0, tile), :] + lr[j] * cr[j] - li[j] * ci[j] for j in pairs]
        ti = [s_ref[n_pair + j, pl.ds(r0, tile), :] + lr[j] * ci[j] + li[j] * cr[j] for j in pairs]
        for j in pairs:
            s_ref[j, pl.ds(r0, tile), :] = tr[j]
            s_ref[n_pair + j, pl.ds(r0, tile), :] = ti[j]
        cr, ci = [last(v) for v in tr], [last(v) for v in ti]
        if k % S5_TILES_PER_STAGE == S5_TILES_PER_STAGE - 1:
            yield
    for j in pairs:
        carry_ref[j] = cr[j]
        carry_ref[n_pair + j] = ci[j]
    s_all = jnp.concatenate([s_ref[j] for j in range(n_slab)], axis=1)
    y_ref[...] = jnp.dot(s_all.astype(BF16), cbd_ref[...], preferred_element_type=F32) + d_ref[...] * u


def _conv_halo_init(pad_ref, t, first):
    halo = SUBLANES

    @pl.when(first)
    def _():
        pad_ref[0:halo, :] = jnp.zeros((halo, pad_ref.shape[1]), F32)

    @pl.when(jnp.logical_not(first))
    def _():
        pad_ref[0:halo, :] = pad_ref[t:t + halo, :]


def _causal_conv_block(x_ref, w_ref, pad_ref):
    t = x_ref.shape[0]
    halo = SUBLANES
    pad_ref[halo:halo + t, :] = x_ref[...]
    acc = w_ref[CONV_K - 1:CONV_K, :] * pad_ref[halo:halo + t, :]
    for j in range(1, CONV_K):
        acc += w_ref[CONV_K - 1 - j:CONV_K - j, :] * pad_ref[halo - j:halo - j + t, :]
    return acc


def _ssd_stages(heads, z_ref, xbc_ref, sm_ref, cw_ref, cb_ref, dtb_ref, a_ref, dexp_ref, nw_ref, ex_ref,
                o_ref, pad_ref, xc_ref, dt_ref, st_ref):
    t = xbc_ref.shape[0]
    q = min(SSD_CHUNK, t)
    width = heads * SSD_HEADDIM
    gw = SSD_NGROUPS * SSD_STATE
    xc_ref[...] = _silu(_causal_conv_block(xbc_ref, cw_ref, pad_ref) + cb_ref[...])
    dt_ref[...] = _softplus(sm_ref[...] + dtb_ref[...])
    yield

    row = _iota2((q, q), 0)
    col = _iota2((q, q), 1)
    causal = row >= col
    tril = jnp.where(causal, 1.0, 0.0).astype(BF16)
    lane = _iota2((1, width), 1)
    rep = heads // SSD_NGROUPS

    for ci in range(t // q):
        r0 = ci * q
        xs = xc_ref[pl.ds(r0, q), 0:width]
        bm = xc_ref[pl.ds(r0, q), width:width + gw]
        cm = xc_ref[pl.ds(r0, q), width + gw:width + 2 * gw]
        dt = dt_ref[pl.ds(r0, q), :]
        acs = _sel_dot(tril, dt * a_ref[...])
        acs_t = acs.T
        xdt = xs * _dot_sel(dt, ex_ref[...])
        yield
        hs = range(heads)
        bgs = [bm[:, gi * SSD_STATE:(gi + 1) * SSD_STATE] for gi in range(SSD_NGROUPS)]
        cgs = [cm[:, gi * SSD_STATE:(gi + 1) * SSD_STATE].astype(BF16) for gi in range(SSD_NGROUPS)]
        cbs = [_dot_nt(cgs[gi], bgs[gi]) for gi in range(SSD_NGROUPS)]
        xdb = xdt.astype(BF16)
        xh = [xdb[:, hh * SSD_HEADDIM:(hh + 1) * SSD_HEADDIM] for hh in hs]
        acol = [acs[:, hh:hh + 1] for hh in hs]
        alast = [acs[q - 1:q, hh:hh + 1] for hh in hs]
        decay = [jnp.where(causal, jnp.exp(jnp.where(causal, acol[hh] - acs_t[hh:hh + 1, :], 0.0)), 0.0)
                 for hh in hs]
        yield
        state = [st_ref[hh] for hh in hs]
        y_diag = [_mm((cbs[hh // rep] * decay[hh]).astype(BF16), xh[hh]) for hh in hs]
        y_off = [_mm(cgs[hh // rep], state[hh].astype(BF16)) * jnp.exp(acol[hh]) for hh in hs]
        upd = [_dot_tn(bgs[hh // rep] * jnp.exp(alast[hh] - acol[hh]), xh[hh]) for hh in hs]
        for hh in hs:
            st_ref[hh] = state[hh] * jnp.exp(alast[hh]) + upd[hh]
        yield
        y = jnp.concatenate([y_diag[hh] + y_off[hh] for hh in hs], axis=1) + dexp_ref[...] * xs
        y = y * _silu(z_ref[pl.ds(r0, q), :])
        gsz = width // SSD_NGROUPS
        y2 = y * y
        out = jnp.zeros_like(y)
        for gi in range(SSD_NGROUPS):
            m = (lane >= gi * gsz) & (lane < (gi + 1) * gsz)
            ms = jnp.sum(jnp.where(m, y2, 0.0), axis=-1, keepdims=True) / gsz
            out = jnp.where(m, y * lax.rsqrt(ms + EPS), out)
        o_ref[pl.ds(r0, q), :] = out * nw_ref[...]
        yield


def _unit_lower_solves(lms, rhs, eye, blk, chunk):
    lds = [jnp.where(blk, lm, 0.0) for lm in lms]
    offs = [(lm - ld).astype(BF16) for lm, ld in zip(lms, lds)]
    dinv = [eye - ld for ld in lds]
    pw = [ld.astype(BF16) for ld in lds]
    yield
    span = 2
    while span < GDN_INV_BLOCK:
        sq = [_mm(p, p) for p in pw]
        yield
        dinv = [_mm(d.astype(BF16), (eye + s).astype(BF16)) for d, s in zip(dinv, sq)]
        pw = [s.astype(BF16) for s in sq]
        yield
        span *= 2
    dinv = [d.astype(BF16) for d in dinv]
    ms = [_mm(d, o).astype(BF16) for d, o in zip(dinv, offs)]
    x0 = [_mm(d, r) for d, r in zip(dinv, rhs)]
    yield
    acc = x0
    for _ in range(chunk // GDN_INV_BLOCK - 1):
        acc = [x - _mm(m, a.astype(BF16)) for x, m, a in zip(x0, ms, acc)]
        yield
    return acc


def _interleave(gens):
    results = [None] * len(gens)
    live = list(range(len(gens)))
    while live:
        for i in list(live):
            try:
                next(gens[i])
            except StopIteration as done:
                results[i] = done.value
                live.remove(i)
        yield
    return results


def _delayed(gen, rounds):
    for _ in range(rounds):
        yield
    return (yield from gen)


def _run(gen):
    try:
        while True:
            next(gen)
    except StopIteration as done:
        return done.value


def _gdn_stages(heads, qkv_ref, z_ref, sm_ref, cw_ref, dtb_ref, a_ref, nw_ref, exb_ref, exg_ref, ones_ref,
                o_ref, pad_ref, st_ref):
    t = qkv_ref.shape[0]
    c = min(GDN_CHUNK, t)
    hd = GDN_HEADDIM
    width = heads * hd
    hs = range(heads)
    sl = [slice(hh * hd, (hh + 1) * hd) for hh in hs]

    def seg_sum(v):
        hi = v.astype(BF16)
        lo = (v - hi.astype(F32)).astype(BF16)
        return _mm(hi, ones_ref[...]) + _mm(lo, ones_ref[...])

    xc = _silu(_causal_conv_block(qkv_ref, cw_ref, pad_ref))
    yield
    qf = xc[:, 0:width]
    kf = xc[:, width:2 * width]
    vf = xc[:, 2 * width:3 * width]
    qf = qf * lax.rsqrt(seg_sum(qf * qf) + EPS) * (hd ** -0.5)
    kf = kf * lax.rsqrt(seg_sum(kf * kf) + EPS)
    sm = sm_ref[...]
    beta = _dot_sel(_sigmoid(sm), exb_ref[...])
    g = a_ref[...] * _softplus(sm + dtb_ref[...])
    yield

    gs = min(GDN_GROUP, t)
    row = _iota2((gs, gs), 0)
    col = _iota2((gs, gs), 1)
    same = (row // c) == (col // c)
    causal = same & (row >= col)
    strict = same & (row > col)
    eye = jnp.where(row == col, 1.0, 0.0)
    blk = (row // GDN_INV_BLOCK) == (col // GDN_INV_BLOCK)
    tril_sel = jnp.where(causal, 1.0, 0.0).astype(BF16)
    same_sel = jnp.where(same, 1.0, 0.0).astype(BF16)
    nt = (((1,), (1,)), ((), ()))

    def gates(gi):
        rg = slice(gi * gs, (gi + 1) * gs)
        gcs = _sel_dot(tril_sel, g[rg])
        gtot = _sel_dot(same_sel, g[rg])
        gcs_t = gcs.T
        yield
        gw = _dot_sel(gcs, exg_ref[...])
        gtw = _dot_sel(gtot, exg_ref[...])
        gexp = jnp.exp(gw)
        kb = kf[rg] * beta[rg]
        yield
        decay = []
        for hh in hs:
            lg = 2 * heads + hh
            diff = jnp.where(causal, gcs[:, lg:lg + 1] - gcs_t[lg:lg + 1, :], 0.0)
            decay.append(jnp.where(causal, jnp.exp(diff), 0.0))
            yield
        return dict(kb=kb, vb=vf[rg] * beta[rg], kcd_in=kb * gexp, qd=(qf[rg] * gexp).astype(BF16),
                    kend=(kf[rg] * jnp.exp(gtw - gw)).astype(BF16), cdec=jnp.exp(gtw), decay=decay,
                    kfb=kf[rg].astype(BF16), qfb=qf[rg].astype(BF16))

    def solve(p):
        both = jnp.concatenate([p["kb"].astype(BF16), p["qfb"]], axis=0)
        kq = [lax.dot_general(both[:, sl[hh]], p["kfb"][:, sl[hh]], nt, preferred_element_type=F32) for hh in hs]
        yield
        lms = [jnp.where(strict, kq[hh][0:gs] * p["decay"][hh], 0.0) for hh in hs]
        qk = [(kq[hh][gs:2 * gs] * p["decay"][hh]).astype(BF16) for hh in hs]
        rhs = [jnp.concatenate([p["vb"][:, sl[hh]], p["kcd_in"][:, sl[hh]]], axis=1).astype(BF16) for hh in hs]
        vk = yield from _unit_lower_solves(lms, rhs, eye, blk, c)
        return qk, vk

    def recur(state, p, qk, vk):
        outs = []
        for ci in range(gs // c):
            rs = slice(ci * c, (ci + 1) * c)
            lhs = [jnp.concatenate([vk[hh][rs, hd:2 * hd].astype(BF16), p["qd"][rs, sl[hh]]], axis=0)
                   for hh in hs]
            sb = [s.astype(BF16) for s in state]
            both = [_mm(lhs[hh], sb[hh]) for hh in hs]
            yield
            v_new = [(vk[hh][rs, 0:hd] - both[hh][0:c]).astype(BF16) for hh in hs]
            o = [both[hh][c:2 * c] + _mm(qk[hh][rs, ci * c:(ci + 1) * c], v_new[hh]) for hh in hs]
            upd = [lax.dot_general(p["kend"][rs, sl[hh]], v_new[hh], (((0,), (0,)), ((), ())),
                                   preferred_element_type=F32) for hh in hs]
            state = [state[hh] * p["cdec"][ci * c:ci * c + 1, sl[hh]] + upd[hh] for hh in hs]
            outs.append(jnp.concatenate(o, axis=1))
            yield
        return state, outs

    n_groups = t // gs
    parts, solved, outs = {}, {}, []
    state = [st_ref[hh] for hh in hs]
    for k in range(n_groups + 2):
        jobs = {}
        if k < n_groups:
            jobs["gates"] = gates(k)
        if 0 <= k - 1 < n_groups:
            jobs["solve"] = solve(parts[k - 1])
        if 0 <= k - 2 < n_groups:
            jobs["recur"] = recur(state, parts[k - 2], *solved[k - 2])
        done = dict(zip(jobs, (yield from _interleave(list(jobs.values())))))
        if "gates" in done:
            parts[k] = done["gates"]
        if "solve" in done:
            solved[k - 1] = done["solve"]
        if "recur" in done:
            state, o_g = done["recur"]
            outs += o_g
    for hh in hs:
        st_ref[hh] = state[hh]
    yield
    o = jnp.concatenate(outs, axis=0)
    o = o * lax.rsqrt(seg_sum(o * o) / hd + EPS) * nw_ref[...]
    o_ref[...] = o * _silu(z_ref[...])


_N_S5_IN, _N_SSD_IN, _N_GDN_IN = 5, 10, 10


def _mixers_kernel(ssd_heads, gdn_heads, *refs):
    n_in = _N_S5_IN + _N_SSD_IN + _N_GDN_IN
    s5_in = refs[:_N_S5_IN]
    ssd_in = refs[_N_S5_IN:_N_S5_IN + _N_SSD_IN]
    gdn_in = refs[_N_S5_IN + _N_SSD_IN:n_in]
    y_s5, y_ssd, y_gdn = refs[n_in:n_in + 3]
    s5_s, s5_carry, ssd_pad, ssd_xc, ssd_dt, ssd_st, gdn_pad, gdn_st = refs[n_in + 3:]
    t = y_s5.shape[0]
    first = pl.program_id(1) == 0

    @pl.when(first)
    def _():
        s5_carry[...] = jnp.zeros(s5_carry.shape, F32)
        ssd_st[...] = jnp.zeros(ssd_st.shape, F32)
        gdn_st[...] = jnp.zeros(gdn_st.shape, F32)

    _conv_halo_init(ssd_pad, t, first)
    _conv_halo_init(gdn_pad, t, first)
    _run(_interleave([
        _gdn_stages(gdn_heads, *gdn_in, y_gdn, gdn_pad, gdn_st),
        _delayed(_ssd_stages(ssd_heads, *ssd_in, y_ssd, ssd_pad, ssd_xc, ssd_dt, ssd_st), MIXER_DELAY_ROUNDS),
        _delayed(_s5_stages(*s5_in, y_s5, s5_s, s5_carry), MIXER_DELAY_ROUNDS),
    ]))


def _mixers(u_s5, s5_ops, z_ssd, xbc, small, ssd_p, qkv, z_gdn, gdn_p, bsz, seq):
    bbd, cbd, lam, d_s5 = s5_ops
    conv_w, conv_b, dt_bias, a_log, d_skip, norm_w = ssd_p
    g_conv_w, g_a_log, g_dt_bias, g_norm_w = gdn_p
    n, s5_w = u_s5.shape
    s2 = bbd.shape[1]
    ssd_w, ssd_cdim, ssd_h = z_ssd.shape[1], xbc.shape[1], a_log.shape[0]
    gdn_w, gdn_cdim, gdn_h = z_gdn.shape[1], qkv.shape[1], g_a_log.shape[0]
    t = min(SEQ_BLOCK, seq)
    nblk = seq // t
    tok = lambda b, c: (b * nblk + c, 0)
    const = lambda b, c: (0, 0)
    rows = lambda w: pl.BlockSpec((t, w), tok)
    whole = lambda a: pl.BlockSpec(a.shape, const)
    pad = lambda v, off: jnp.zeros((1, LANES), F32).at[0, off:off + v.shape[0]].set(v)
    lanes = jnp.arange(LANES)[:, None]
    ssd_ex = (lanes == (jnp.arange(ssd_w)[None, :] // SSD_HEADDIM)).astype(BF16)
    head_of = jnp.arange(gdn_w)[None, :] // GDN_HEADDIM
    s5_args = [bbd.astype(BF16), cbd.astype(BF16), lam, d_s5]
    ssd_args = [conv_w, conv_b.reshape(1, ssd_cdim), pad(dt_bias, 0), pad(-jnp.exp(a_log), 0),
                jnp.repeat(d_skip, SSD_HEADDIM).reshape(1, ssd_w), norm_w.reshape(1, ssd_w), ssd_ex]
    gdn_args = [g_conv_w, pad(g_dt_bias, 2 * gdn_h), pad(-jnp.exp(g_a_log), 2 * gdn_h),
                jnp.tile(g_norm_w, gdn_h).reshape(1, gdn_w), (lanes == gdn_h + head_of).astype(BF16),
                (lanes == 2 * gdn_h + head_of).astype(BF16), (head_of.T == head_of).astype(BF16)]
    in_specs = ([rows(s5_w)] + [whole(a) for a in s5_args]
                + [rows(ssd_w), rows(ssd_cdim), rows(LANES)] + [whole(a) for a in ssd_args]
                + [rows(gdn_cdim), rows(gdn_w), rows(LANES)] + [whole(a) for a in gdn_args])
    assert len(in_specs) == _N_S5_IN + _N_SSD_IN + _N_GDN_IN
    return pl.pallas_call(
        functools.partial(_mixers_kernel, ssd_h, gdn_h),
        grid=(bsz, nblk),
        in_specs=in_specs,
        out_specs=[rows(s5_w), rows(ssd_w), rows(gdn_w)],
        out_shape=[jax.ShapeDtypeStruct((n, w), F32) for w in (s5_w, ssd_w, gdn_w)],
        scratch_shapes=[pltpu.VMEM((s2 // LANES, t, LANES), F32),
                        pltpu.VMEM((s2 // LANES, SUBLANES, LANES), F32),
                        pltpu.VMEM((SUBLANES + t, ssd_cdim), F32),
                        pltpu.VMEM((t, ssd_cdim), F32),
                        pltpu.VMEM((t, LANES), F32),
                        pltpu.VMEM((ssd_h, SSD_STATE, SSD_HEADDIM), F32),
                        pltpu.VMEM((SUBLANES + t, gdn_cdim), F32),
                        pltpu.VMEM((gdn_h, GDN_HEADDIM, GDN_HEADDIM), F32)],
        compiler_params=_params("parallel", "arbitrary"),
        name="mixers",
    )(u_s5, *s5_args, z_ssd, xbc, small, *ssd_args, qkv, z_gdn, small, *gdn_args)


def _outproj_kernel(x_ref, s5_ref, ssd_ref, gdn_ref, wglu_ref, s5n_ref, wout_ref, o_ref):
    y = _gelu_tanh(s5_ref[...])
    y = y * _sigmoid(jnp.dot(y.astype(BF16), wglu_ref[...], preferred_element_type=F32))
    y = _rms(y, s5n_ref[...])
    mix = jnp.concatenate([y, ssd_ref[...], gdn_ref[...]], axis=1)
    o_ref[...] = x_ref[...] + jnp.dot(mix.astype(BF16), wout_ref[...], preferred_element_type=F32)


def _outproj(x2, y_s5, y_ssd, y_gdn, w_glu, s5_norm, w_out):
    n, d = x2.shape
    tm = min(OUT_ROW_BLOCK, n)
    ws = [y_s5.shape[1], y_ssd.shape[1], y_gdn.shape[1]]
    row = lambda i: (i, 0)
    const = lambda i: (0, 0)
    return pl.pallas_call(
        _outproj_kernel,
        grid=(n // tm,),
        in_specs=[pl.BlockSpec((tm, d), row)] + [pl.BlockSpec((tm, w), row) for w in ws]
                 + [pl.BlockSpec((ws[0], ws[0]), const), pl.BlockSpec((1, ws[0]), const),
                    pl.BlockSpec((sum(ws), d), const)],
        out_specs=pl.BlockSpec((tm, d), row),
        out_shape=jax.ShapeDtypeStruct((n, d), F32),
        compiler_params=_params("parallel"),
        name="outproj",
    )(x2, y_s5, y_ssd, y_gdn, w_glu.astype(BF16), s5_norm.reshape(1, -1), w_out.astype(BF16))


def _ffn_kernel(final, x_ref, g_ref, wg_ref, wu_ref, wd_ref, gf_ref, o_ref, h_ref, acc_ref):
    f = pl.program_id(1)

    @pl.when(f == 0)
    def _():
        h_ref[...] = _rms(x_ref[...], g_ref[...]).astype(BF16)
        acc_ref[...] = jnp.zeros(acc_ref.shape, F32)

    h = h_ref[...]
    a = jnp.dot(h, wg_ref[...].astype(BF16), preferred_element_type=F32)
    u = jnp.dot(h, wu_ref[...].astype(BF16), preferred_element_type=F32)
    acc_ref[...] += jnp.dot((_silu(a) * u).astype(BF16), wd_ref[...].astype(BF16), preferred_element_type=F32)

    @pl.when(f == pl.num_programs(1) - 1)
    def _():
        out = x_ref[...] + acc_ref[...]
        o_ref[...] = _rms(out, gf_ref[...]) if final else out


def _ffn(x2, gain, w_gate, w_up, w_down, final_gain):
    n, d = x2.shape
    ff = w_gate.shape[1]
    tm = min(FFN_ROW_BLOCK, n)
    tf = min(FFN_COL_BLOCK, ff)
    final = final_gain is not None
    gf = (final_gain if final else gain).reshape(1, d)
    return pl.pallas_call(
        functools.partial(_ffn_kernel, final),
        grid=(n // tm, ff // tf),
        in_specs=[pl.BlockSpec((tm, d), lambda i, f: (i, 0)),
                  pl.BlockSpec((1, d), lambda i, f: (0, 0)),
                  pl.BlockSpec((d, tf), lambda i, f: (0, f)),
                  pl.BlockSpec((d, tf), lambda i, f: (0, f)),
                  pl.BlockSpec((tf, d), lambda i, f: (f, 0)),
                  pl.BlockSpec((1, d), lambda i, f: (0, 0))],
        out_specs=pl.BlockSpec((tm, d), lambda i, f: (i, 0)),
        out_shape=jax.ShapeDtypeStruct((n, d), F32),
        scratch_shapes=[pltpu.VMEM((tm, d), BF16), pltpu.VMEM((tm, d), F32)],
        compiler_params=_params("parallel", "arbitrary"),
        name="ffn",
    )(x2, gain.reshape(1, d), w_gate, w_up, w_down, gf)


_R_E1, _R_E2, _R_W1, _R_W2, _R_RANK1, _R_RANK2 = range(6)


def _router_kernel(n_exp, x_ref, g_ref, wr_ref, tri_ref, hn_ref, rt_ref, cnt_ref, base_ref):
    @pl.when(pl.program_id(0) == 0)
    def _():
        base_ref[...] = jnp.zeros(base_ref.shape, F32)

    hn = _rms(x_ref[...], g_ref[...])
    hn_ref[...] = hn
    h_hi = hn.astype(BF16)
    h_lo = (hn - h_hi.astype(F32)).astype(BF16)
    w_hi = wr_ref[...].astype(BF16)
    w_lo = (wr_ref[...] - w_hi.astype(F32)).astype(BF16)
    logits = _mm(h_hi, w_hi) + _mm(h_lo, w_hi) + _mm(h_hi, w_lo)
    lane = _iota2(logits.shape, 1)
    neg = jnp.float32(-jnp.inf)
    logits = jnp.where(lane < n_exp, logits, neg)
    m1 = jnp.max(logits, axis=-1, keepdims=True)
    i1 = jnp.min(jnp.where(logits == m1, lane, LANES), axis=-1, keepdims=True)
    rest = jnp.where(lane == i1, neg, logits)
    m2 = jnp.max(rest, axis=-1, keepdims=True)
    i2 = jnp.min(jnp.where(rest == m2, lane, LANES), axis=-1, keepdims=True)
    w1 = 1.0 / (1.0 + jnp.exp(m2 - m1))
    w2 = 1.0 / (1.0 + jnp.exp(m1 - m2))
    hit = (lane == i1) | (lane == i2)
    onehot = jnp.where(hit, 1.0, 0.0)
    before = jnp.dot(tri_ref[...], onehot.astype(BF16), preferred_element_type=F32) + base_ref[...]
    r1 = jnp.sum(jnp.where(lane == i1, before, 0.0), axis=-1, keepdims=True)
    r2 = jnp.sum(jnp.where(lane == i2, before, 0.0), axis=-1, keepdims=True)
    cols = {_R_E1: i1.astype(F32), _R_E2: i2.astype(F32), _R_W1: w1, _R_W2: w2, _R_RANK1: r1, _R_RANK2: r2}
    rt = jnp.zeros(logits.shape, F32)
    for k, v in cols.items():
        rt = jnp.where(lane == k, v, rt)
    rt_ref[...] = rt
    base_ref[...] += jnp.sum(onehot, axis=0, keepdims=True)
    cnt_ref[...] = base_ref[...]


def _row_copy(src_ref, src_row, dst_ref, dst_row, sem):
    return pltpu.make_async_copy(src_ref.at[pl.ds(src_row, 1)], dst_ref.at[pl.ds(dst_row, 1)], sem)


def _dispatch_kernel(n_exp, rb, s1_ref, s2_ref, fill_ref, hn_ref, xs_ref, zero_ref, sem, zsem):
    tm = hn_ref.shape[0]
    base = pl.program_id(0) * tm

    @pl.when(pl.program_id(0) == 0)
    def _():
        zero_ref[...] = jnp.zeros(zero_ref.shape, F32)
        for e in range(n_exp):
            lo, hi = fill_ref[e], fill_ref[n_exp + e]

            def start_row(r, carry):
                _row_copy(zero_ref, 0, xs_ref, r, zsem).start()
                return carry

            def wait_row(r, carry):
                _row_copy(zero_ref, 0, xs_ref, r, zsem).wait()
                return carry

            lax.fori_loop(lo, hi, start_row, 0)
            lax.fori_loop(lo, hi, wait_row, 0)

        def block_copy(b):
            return pltpu.make_async_copy(zero_ref, xs_ref.at[pl.ds(pl.multiple_of(b * rb, rb), rb)], zsem)

        def start_block(b, carry):
            block_copy(b).start()
            return carry

        def wait_block(b, carry):
            block_copy(b).wait()
            return carry

        n_blocks = xs_ref.shape[0] // rb
        lax.fori_loop(fill_ref[2 * n_exp], n_blocks, start_block, 0)
        lax.fori_loop(fill_ref[2 * n_exp], n_blocks, wait_block, 0)

    def issue(r, carry):
        _row_copy(hn_ref, r, xs_ref, s1_ref[base + r], sem).start()
        _row_copy(hn_ref, r, xs_ref, s2_ref[base + r], sem).start()
        return carry

    lax.fori_loop(0, tm, issue, 0, unroll=DMA_ISSUE_UNROLL)
    for _ in range(TOP_K):
        pltpu.make_async_copy(hn_ref, xs_ref.at[pl.ds(0, tm)], sem).wait()


def _expert_ffn_kernel(be_ref, bv_ref, xs_ref, wg_ref, wu_ref, wd_ref, ys_ref, h_ref, acc_ref):
    i = pl.program_id(0)
    f = pl.program_id(1)
    last = pl.num_programs(1) - 1
    valid = bv_ref[i]

    @pl.when(valid > 0)
    def _():
        @pl.when(f == 0)
        def _():
            row = _iota2((xs_ref.shape[0], 1), 0)
            h_ref[...] = jnp.where(row < valid, xs_ref[...], 0.0).astype(BF16)
            acc_ref[...] = jnp.zeros(acc_ref.shape, F32)

        h = h_ref[...]
        a = jnp.dot(h, wg_ref[0].astype(BF16), preferred_element_type=F32)
        u = jnp.dot(h, wu_ref[0].astype(BF16), preferred_element_type=F32)
        acc_ref[...] += jnp.dot((_silu(a) * u).astype(BF16), wd_ref[0].astype(BF16), preferred_element_type=F32)

        @pl.when(f == last)
        def _():
            ys_ref[...] = acc_ref[...]

    @pl.when((valid == 0) & (f == last))
    def _():
        ys_ref[...] = jnp.zeros(ys_ref.shape, F32)


def _combine_kernel(final, s1_ref, s2_ref, x_ref, rt_ref, gf_ref, ys_ref, o_ref, b1_ref, b2_ref, sem):
    tm = x_ref.shape[0]
    i = pl.program_id(0)
    cur = i % 2

    def issue(block, buf):
        def one(r, carry):
            _row_copy(ys_ref, s1_ref[block * tm + r], b1_ref.at[buf], r, sem.at[buf]).start()
            _row_copy(ys_ref, s2_ref[block * tm + r], b2_ref.at[buf], r, sem.at[buf]).start()
            return carry

        lax.fori_loop(0, tm, one, 0, unroll=DMA_ISSUE_UNROLL)

    @pl.when(i == 0)
    def _():
        issue(0, 0)

    @pl.when(i + 1 < pl.num_programs(0))
    def _():
        issue(i + 1, 1 - cur)

    for b_ref in (b1_ref, b2_ref):
        pltpu.make_async_copy(ys_ref.at[pl.ds(0, tm)], b_ref.at[cur], sem.at[cur]).wait()
    rt = rt_ref[...]
    out = (x_ref[...] + rt[:, _R_W1:_R_W1 + 1] * b1_ref[cur] + rt[:, _R_W2:_R_W2 + 1] * b2_ref[cur])
    o_ref[...] = _rms(out, gf_ref[...]) if final else out


def _moe(x2, gain, w_router, w_gate, w_up, w_down, final_gain):
    n, d = x2.shape
    n_exp, _, ff = w_gate.shape
    final = final_gain is not None
    gf = (final_gain if final else gain).reshape(1, d)
    wr = jnp.zeros((d, LANES), F32).at[:, :n_exp].set(w_router)

    tr = min(ROUTER_BLOCK, n)
    tri = (jnp.arange(tr)[:, None] > jnp.arange(tr)[None, :]).astype(BF16)
    hn, route, counts = pl.pallas_call(
        functools.partial(_router_kernel, n_exp),
        grid=(n // tr,),
        in_specs=[pl.BlockSpec((tr, d), lambda i: (i, 0)),
                  pl.BlockSpec((1, d), lambda i: (0, 0)),
                  pl.BlockSpec((d, LANES), lambda i: (0, 0)),
                  pl.BlockSpec((tr, tr), lambda i: (0, 0))],
        out_specs=[pl.BlockSpec((tr, d), lambda i: (i, 0)),
                   pl.BlockSpec((tr, LANES), lambda i: (i, 0)),
                   pl.BlockSpec((1, LANES), lambda i: (0, 0))],
        out_shape=[jax.ShapeDtypeStruct((n, d), F32),
                   jax.ShapeDtypeStruct((n, LANES), F32),
                   jax.ShapeDtypeStruct((1, LANES), F32)],
        scratch_shapes=[pltpu.VMEM((1, LANES), F32)],
        compiler_params=_params("arbitrary"),
        name="moe_router",
    )(x2, gain.reshape(1, d), wr, tri)

    rb = min(MOE_ROW_BLOCK, TOP_K * n)
    n_slots = TOP_K * n + n_exp * rb
    n_blocks = n_slots // rb
    cnt = counts[0, :n_exp].astype(jnp.int32)
    padded = ((cnt + rb - 1) // rb) * rb
    ends = jnp.cumsum(padded)
    starts = ends - padded
    as_int = lambda k: route[:, k].astype(jnp.int32)
    slot1 = starts[as_int(_R_E1)] + as_int(_R_RANK1)
    slot2 = starts[as_int(_R_E2)] + as_int(_R_RANK2)
    block_row = jnp.arange(n_blocks, dtype=jnp.int32) * rb
    block_exp = jnp.minimum(jnp.sum(block_row[:, None] >= ends[None, :], axis=1), n_exp - 1).astype(jnp.int32)
    block_valid = jnp.clip(cnt[block_exp] - (block_row - starts[block_exp]), 0, rb).astype(jnp.int32)

    tdp = min(DISPATCH_BLOCK, n)
    fill = jnp.concatenate([starts + cnt, ends, ends[-1:] // rb]).astype(jnp.int32)
    xs = pl.pallas_call(
        functools.partial(_dispatch_kernel, n_exp, rb),
        grid_spec=pltpu.PrefetchScalarGridSpec(
            num_scalar_prefetch=3,
            grid=(n // tdp,),
            in_specs=[pl.BlockSpec((tdp, d), lambda i, s1, s2, fl: (i, 0))],
            out_specs=pl.BlockSpec(memory_space=pl.ANY),
            scratch_shapes=[pltpu.VMEM((rb, d), F32), pltpu.SemaphoreType.DMA, pltpu.SemaphoreType.DMA]),
        out_shape=jax.ShapeDtypeStruct((n_slots, d), F32),
        compiler_params=_params("arbitrary"),
        name="moe_dispatch",
    )(slot1, slot2, fill, hn)

    tf = min(FFN_COL_BLOCK, ff)
    ys = pl.pallas_call(
        _expert_ffn_kernel,
        grid_spec=pltpu.PrefetchScalarGridSpec(
            num_scalar_prefetch=2,
            grid=(n_blocks, ff // tf),
            in_specs=[pl.BlockSpec((rb, d), lambda i, f, be, bv: (i, 0)),
                      pl.BlockSpec((1, d, tf), lambda i, f, be, bv: (be[i], 0, f)),
                      pl.BlockSpec((1, d, tf), lambda i, f, be, bv: (be[i], 0, f)),
                      pl.BlockSpec((1, tf, d), lambda i, f, be, bv: (be[i], f, 0))],
            out_specs=pl.BlockSpec((rb, d), lambda i, f, be, bv: (i, 0)),
            scratch_shapes=[pltpu.VMEM((rb, d), BF16), pltpu.VMEM((rb, d), F32)]),
        out_shape=jax.ShapeDtypeStruct((n_slots, d), F32),
        compiler_params=_params("parallel", "arbitrary"),
        name="moe_experts",
    )(block_exp, block_valid, xs, w_gate, w_up, w_down)

    tc = min(COMBINE_BLOCK, n)
    return pl.pallas_call(
        functools.partial(_combine_kernel, final),
        grid_spec=pltpu.PrefetchScalarGridSpec(
            num_scalar_prefetch=2,
            grid=(n // tc,),
            in_specs=[pl.BlockSpec((tc, d), lambda i, s1, s2: (i, 0)),
                      pl.BlockSpec((tc, LANES), lambda i, s1, s2: (i, 0)),
                      pl.BlockSpec((1, d), lambda i, s1, s2: (0, 0)),
                      pl.BlockSpec(memory_space=pl.ANY)],
            out_specs=pl.BlockSpec((tc, d), lambda i, s1, s2: (i, 0)),
            scratch_shapes=[pltpu.VMEM((2, tc, d), F32), pltpu.VMEM((2, tc, d), F32),
                            pltpu.SemaphoreType.DMA((2,))]),
        out_shape=jax.ShapeDtypeStruct((n, d), F32),
        compiler_params=_params("arbitrary"),
        name="moe_combine",
    )(slot1, slot2, x2, route, gf, ys)


def _arrange_in_proj(w_in, s5_w, ssd_w, ssd_cdim, ssd_h, gdn_cdim, gdn_w, gdn_h):
    sizes = (s5_w, ssd_w, ssd_cdim, ssd_h, gdn_cdim, gdn_w, gdn_h, gdn_h)
    offs = [int(o) for o in np.cumsum((0,) + sizes)]
    seg = lambda i: w_in[:, offs[i]:offs[i + 1]]
    small = jnp.concatenate([seg(3), seg(6), seg(7)], axis=1)
    small = jnp.pad(small, ((0, 0), (0, LANES - small.shape[1])))
    cuts = ((offs[0], offs[3]), (offs[4], offs[5]), (offs[5], offs[6]))
    return small, (s5_w, ssd_w, ssd_cdim, gdn_cdim, gdn_w, LANES), cuts


def kernel(x, norm_mix, w_in, w_out, s5_a_re, s5_a_im, s5_b_re, s5_b_im, s5_c_re, s5_c_im, s5_d, s5_log_step, s5_w_glu, s5_norm, ssd_conv_w, ssd_conv_b, ssd_dt_bias, ssd_a_log, ssd_d, ssd_norm, gdn_conv_w, gdn_a_log, gdn_dt_bias, gdn_norm, norm_ffn, ff_w_gate, ff_w_up, ff_w_down, moe_router, moe_w_gate, moe_w_up, moe_w_down, norm_final):
    bsz, seq, d = x.shape
    depth = norm_mix.shape[0]
    s5_w = s5_w_glu.shape[-1]
    ssd_w, ssd_h, ssd_cdim = ssd_norm.shape[-1], ssd_a_log.shape[-1], ssd_conv_w.shape[-1]
    gdn_h, gdn_cdim = gdn_a_log.shape[-1], gdn_conv_w.shape[-1]
    gdn_w = gdn_h * GDN_HEADDIM
    xr = x.reshape(bsz * seq, d)
    for layer in range(depth):
        w_small, sizes, cuts = _arrange_in_proj(w_in[layer], s5_w, ssd_w, ssd_cdim, ssd_h, gdn_cdim, gdn_w, gdn_h)
        u_s5, z_ssd, xbc, qkv, z_gdn, small = _inproj(xr, norm_mix[layer], w_in, layer, w_small, sizes, cuts)
        s5_ops = _s5_operands(s5_a_re[layer], s5_a_im[layer], s5_b_re[layer], s5_b_im[layer], s5_c_re[layer],
                              s5_c_im[layer], s5_d[layer], s5_log_step[layer])
        ssd_p = (ssd_conv_w[layer], ssd_conv_b[layer], ssd_dt_bias[layer], ssd_a_log[layer], ssd_d[layer],
                 ssd_norm[layer])
        gdn_p = (gdn_conv_w[layer], gdn_a_log[layer], gdn_dt_bias[layer], gdn_norm[layer])
        y_s5, y_ssd, y_gdn = _mixers(u_s5, s5_ops, z_ssd, xbc, small, ssd_p, qkv, z_gdn, gdn_p, bsz, seq)
        xr = _outproj(xr, y_s5, y_ssd, y_gdn, s5_w_glu[layer], s5_norm[layer], w_out[layer])
        final_gain = norm_final if layer == depth - 1 else None
        i = layer // 2
        if layer % 2 == 0:
            xr = _ffn(xr, norm_ffn[layer], ff_w_gate[i], ff_w_up[i], ff_w_down[i], final_gain)
        else:
            xr = _moe(xr, norm_ffn[layer], moe_router[i], moe_w_gate[i], moe_w_up[i], moe_w_down[i], final_gain)
    return xr.reshape(bsz, seq, d)
```

```python
import functools
import math

import jax
import jax.numpy as jnp
import numpy as np
from jax import lax
from jax.experimental import pallas as pl
from jax.experimental.pallas import tpu as pltpu

F32 = jnp.float32
BF16 = jnp.bfloat16
EPS = 1e-6

S5_GROUP = 16
S5_STATE = 64
SSD_HEADDIM = 64
SSD_NGROUPS = 2
SSD_STATE = 128
GDN_HEADDIM = 64
CONV_K = 4
TOP_K = 2

LANES = 128
SUBLANES = 8
VMEM_LIMIT_BYTES = 56 * 1024 * 1024

SSD_CHUNK = 128
GDN_CHUNK = 64
GDN_INV_BLOCK = 16
GDN_GROUP = 256
SEQ_BLOCK = 512
S5_TILES_PER_STAGE = 4
MIXER_DELAY_ROUNDS = 10
ROW_BLOCK = 512
OUT_ROW_BLOCK = 1024
FFN_ROW_BLOCK = 1024
FFN_COL_BLOCK = 512
ROUTER_BLOCK = 512
DISPATCH_BLOCK = 1024
MOE_ROW_BLOCK = 1024
COMBINE_BLOCK = 512
DMA_ISSUE_UNROLL = 16


def _params(*semantics):
    return pltpu.CompilerParams(dimension_semantics=semantics, vmem_limit_bytes=VMEM_LIMIT_BYTES)


def _dot(a, b):
    return jnp.dot(a.astype(BF16), b.astype(BF16), preferred_element_type=F32)


def _mm(a, b):
    return jnp.dot(a, b, preferred_element_type=F32)


def _dot_nt(a, b):
    return lax.dot_general(a.astype(BF16), b.astype(BF16), (((1,), (1,)), ((), ())),
                           preferred_element_type=F32)


def _dot_tn(a, b):
    return lax.dot_general(a.astype(BF16), b.astype(BF16), (((0,), (0,)), ((), ())),
                           preferred_element_type=F32)


def _split3(a):
    hi = a.astype(BF16)
    r1 = a - hi.astype(F32)
    mid = r1.astype(BF16)
    lo = (r1 - mid.astype(F32)).astype(BF16)
    return hi, mid, lo


def _dot_sel(a, sel):
    hi, mid, lo = _split3(a)
    out = jnp.dot(hi, sel, preferred_element_type=F32)
    out += jnp.dot(mid, sel, preferred_element_type=F32)
    out += jnp.dot(lo, sel, preferred_element_type=F32)
    return out


def _sel_dot(sel, a):
    hi, mid, lo = _split3(a)
    out = jnp.dot(sel, hi, preferred_element_type=F32)
    out += jnp.dot(sel, mid, preferred_element_type=F32)
    out += jnp.dot(sel, lo, preferred_element_type=F32)
    return out


def _sigmoid(x):
    return 1.0 / (1.0 + jnp.exp(-x))


def _silu(x):
    return x * _sigmoid(x)


def _softplus(x):
    return jnp.maximum(x, 0.0) + jnp.log(1.0 + jnp.exp(-jnp.abs(x)))


def _gelu_tanh(x):
    c = math.sqrt(2.0 / math.pi)
    return 0.5 * x * (1.0 + jnp.tanh(c * (x + 0.044715 * (x * x * x))))


def _rms(x, w):
    return x * lax.rsqrt(jnp.mean(x * x, axis=-1, keepdims=True) + EPS) * w


def _iota2(shape, axis):
    return lax.broadcasted_iota(jnp.int32, shape, axis)


def _inproj_kernel(layer, sizes, cuts, x_ref, g_ref, w_hbm, small_ref, *rest):
    out_refs = rest[:len(sizes)]
    raw_ref, w_ref = rest[len(sizes):]

    @pl.when(pl.program_id(0) == 0)
    def _():
        pltpu.sync_copy(w_hbm.at[layer], raw_ref)
        ncols = raw_ref.shape[1]
        off = 0
        for lo, hi in cuts:
            a = (lo // LANES) * LANES
            b = min(-(-hi // LANES) * LANES, ncols)
            w_ref[:, off:off + hi - lo] = raw_ref[:, a:b][:, lo - a:hi - a].astype(BF16)
            off += hi - lo
        w_ref[:, off:off + LANES] = small_ref[...].astype(BF16)

    h = _rms(x_ref[...], g_ref[...])
    p = jnp.dot(h.astype(BF16), w_ref[...], preferred_element_type=F32)
    off = 0
    for o_ref, sz in zip(out_refs, sizes):
        o_ref[...] = p[:, off:off + sz]
        off += sz


def _inproj(x2, gain, w_all, layer, small, sizes, cuts):
    n, d = x2.shape
    tm = min(ROW_BLOCK, n)
    total = sum(sizes)
    return pl.pallas_call(
        functools.partial(_inproj_kernel, layer, sizes, cuts),
        grid=(n // tm,),
        in_specs=[pl.BlockSpec((tm, d), lambda i: (i, 0)),
                  pl.BlockSpec((1, d), lambda i: (0, 0)),
                  pl.BlockSpec(memory_space=pl.ANY),
                  pl.BlockSpec((d, LANES), lambda i: (0, 0))],
        out_specs=[pl.BlockSpec((tm, sz), lambda i: (i, 0)) for sz in sizes],
        out_shape=[jax.ShapeDtypeStruct((n, sz), F32) for sz in sizes],
        scratch_shapes=[pltpu.VMEM(w_all.shape[1:], F32), pltpu.VMEM((d, total), BF16)],
        compiler_params=_params("arbitrary"),
        name="inproj",
    )(x2, gain.reshape(1, d), w_all, small)


def _s5_operands(a_re, a_im, b_re, b_im, c_re, c_im, d_skip, log_step):
    g, p = a_re.shape
    h = b_re.shape[-1]
    step = jnp.exp(log_step)[:, None]
    mag = jnp.exp(a_re * step)
    ang = a_im * step
    lb_re, lb_im = mag * jnp.cos(ang), mag * jnp.sin(ang)
    den = a_re * a_re + a_im * a_im
    f_re = ((lb_re - 1.0) * a_re + lb_im * a_im) / den
    f_im = (lb_im * a_re - (lb_re - 1.0) * a_im) / den
    bb_re = f_re[..., None] * b_re - f_im[..., None] * b_im
    bb_im = f_re[..., None] * b_im + f_im[..., None] * b_re

    eye = jnp.eye(g, dtype=F32)
    to_state = lambda m: (m.transpose(0, 2, 1)[:, :, None, :] * eye[:, None, :, None]).reshape(g * h, g * p)
    to_out = lambda m: (m.transpose(0, 2, 1)[:, :, None, :] * eye[:, None, :, None]).reshape(g * p, g * h)
    bbd = jnp.concatenate([to_state(bb_re), to_state(bb_im)], axis=1)
    cbd = jnp.concatenate([to_out(c_re), -to_out(c_im)], axis=0)
    e = jnp.arange(1, SUBLANES + 1, dtype=F32)[:, None, None]
    m = jnp.exp(a_re * step * e)
    lam = jnp.concatenate([(m * jnp.cos(ang * e)).reshape(SUBLANES, g * p),
                           (m * jnp.sin(ang * e)).reshape(SUBLANES, g * p)], axis=1)
    return bbd, cbd, lam, d_skip.reshape(1, g * h)


def _s5_stages(u_ref, bbd_ref, cbd_ref, lam_ref, d_ref, y_ref, s_ref, carry_ref):
    n_slab, t, _ = s_ref.shape
    n_pair = n_slab // 2
    tile = SUBLANES
    pairs = range(n_pair)
    lanes = lambda j: slice(j * LANES, (j + 1) * LANES)

    u = u_ref[...]
    bu = jnp.dot(u.astype(BF16), bbd_ref[...], preferred_element_type=F32)
    for j in range(n_slab):
        s_ref[j] = bu[:, lanes(j)]
    yield
    lam = lam_ref[...]
    lr = [lam[:, lanes(j)] for j in pairs]
    li = [lam[:, lanes(n_pair + j)] for j in pairs]

    def rows(j, r):
        return s_ref.at[j, pl.ds(r, t // tile, stride=tile), :]

    pr = [rows(j, 0)[...] for j in pairs]
    pi = [rows(n_pair + j, 0)[...] for j in pairs]
    for r in range(1, tile):
        nr = [rows(j, r)[...] + lr[j][0:1] * pr[j] - li[j][0:1] * pi[j] for j in pairs]
        ni = [rows(n_pair + j, r)[...] + lr[j][0:1] * pi[j] + li[j][0:1] * pr[j] for j in pairs]
        for j in pairs:
            rows(j, r)[...] = nr[j]
            rows(n_pair + j, r)[...] = ni[j]
        pr, pi = nr, ni
        yield

    last = lambda v: jnp.broadcast_to(v[tile - 1:tile], v.shape)
    cr = [carry_ref[j] for j in pairs]
    ci = [carry_ref[n_pair + j] for j in pairs]
    for k in range(t // tile):
        r0 = k * tile
        tr = [s_ref[j, pl.ds(r0, tile), :] + lr[j] * cr[j] - li[j] * ci[j] for j in pairs]
        ti = [s_ref[n_pair + j, pl.ds(r0, tile), :] + lr[j] * ci[j] + li[j] * cr[j] for j in pairs]
        for j in pairs:
            s_ref[j, pl.ds(r0, tile), :] = tr[j]
            s_ref[n_pair + j, pl.ds(r0, tile), :] = ti[j]
        cr, ci = [last(v) for v in tr], [last(v) for v in ti]
        if k % S5_TILES_PER_STAGE == S5_TILES_PER_STAGE - 1:
            yield
    for j in pairs:
        carry_ref[j] = cr[j]
        carry_ref[n_pair + j] = ci[j]
    s_all = jnp.concatenate([s_ref[j] for j in range(n_slab)], axis=1)
    y_ref[...] = jnp.dot(s_all.astype(BF16), cbd_ref[...], preferred_element_type=F32) + d_ref[...] * u


def _conv_halo_init(pad_ref, t, first):
    halo = SUBLANES

    @pl.when(first)
    def _():
        pad_ref[0:halo, :] = jnp.zeros((halo, pad_ref.shape[1]), F32)

    @pl.when(jnp.logical_not(first))
    def _():
        pad_ref[0:halo, :] = pad_ref[t:t + halo, :]


def _causal_conv_block(x_ref, w_ref, pad_ref):
    t = x_ref.shape[0]
    halo = SUBLANES
    pad_ref[halo:halo + t, :] = x_ref[...]
    acc = w_ref[CONV_K - 1:CONV_K, :] * pad_ref[halo:halo + t, :]
    for j in range(1, CONV_K):
        acc += w_ref[CONV_K - 1 - j:CONV_K - j, :] * pad_ref[halo - j:halo - j + t, :]
    return acc


def _ssd_stages(heads, z_ref, xbc_ref, sm_ref, cw_ref, cb_ref, dtb_ref, a_ref, dexp_ref, nw_ref, ex_ref,
                o_ref, pad_ref, xc_ref, dt_ref, st_ref):
    t = xbc_ref.shape[0]
    q = min(SSD_CHUNK, t)
    width = heads * SSD_HEADDIM
    gw = SSD_NGROUPS * SSD_STATE
    xc_ref[...] = _silu(_causal_conv_block(xbc_ref, cw_ref, pad_ref) + cb_ref[...])
    dt_ref[...] = _softplus(sm_ref[...] + dtb_ref[...])
    yield

    row = _iota2((q, q), 0)
    col = _iota2((q, q), 1)
    causal = row >= col
    tril = jnp.where(causal, 1.0, 0.0).astype(BF16)
    lane = _iota2((1, width), 1)
    rep = heads // SSD_NGROUPS

    for ci in range(t // q):
        r0 = ci * q
        xs = xc_ref[pl.ds(r0, q), 0:width]
        bm = xc_ref[pl.ds(r0, q), width:width + gw]
        cm = xc_ref[pl.ds(r0, q), width + gw:width + 2 * gw]
        dt = dt_ref[pl.ds(r0, q), :]
        acs = _sel_dot(tril, dt * a_ref[...])
        acs_t = acs.T
        xdt = xs * _dot_sel(dt, ex_ref[...])
        yield
        hs = range(heads)
        bgs = [bm[:, gi * SSD_STATE:(gi + 1) * SSD_STATE] for gi in range(SSD_NGROUPS)]
        cgs = [cm[:, gi * SSD_STATE:(gi + 1) * SSD_STATE].astype(BF16) for gi in range(SSD_NGROUPS)]
        cbs = [_dot_nt(cgs[gi], bgs[gi]) for gi in range(SSD_NGROUPS)]
        xdb = xdt.astype(BF16)
        xh = [xdb[:, hh * SSD_HEADDIM:(hh + 1) * SSD_HEADDIM] for hh in hs]
        acol = [acs[:, hh:hh + 1] for hh in hs]
        alast = [acs[q - 1:q, hh:hh + 1] for hh in hs]
        decay = [jnp.where(causal, jnp.exp(jnp.where(causal, acol[hh] - acs_t[hh:hh + 1, :], 0.0)), 0.0)
                 for hh in hs]
        yield
        state = [st_ref[hh] for hh in hs]
        y_diag = [_mm((cbs[hh // rep] * decay[hh]).astype(BF16), xh[hh]) for hh in hs]
        y_off = [_mm(cgs[hh // rep], state[hh].astype(BF16)) * jnp.exp(acol[hh]) for hh in hs]
        upd = [_dot_tn(bgs[hh // rep] * jnp.exp(alast[hh] - acol[hh]), xh[hh]) for hh in hs]
        for hh in hs:
            st_ref[hh] = state[hh] * jnp.exp(alast[hh]) + upd[hh]
        yield
        y = jnp.concatenate([y_diag[hh] + y_off[hh] for hh in hs], axis=1) + dexp_ref[...] * xs
        y = y * _silu(z_ref[pl.ds(r0, q), :])
        gsz = width // SSD_NGROUPS
        y2 = y * y
        out = jnp.zeros_like(y)
        for gi in range(SSD_NGROUPS):
            m = (lane >= gi * gsz) & (lane < (gi + 1) * gsz)
            ms = jnp.sum(jnp.where(m, y2, 0.0), axis=-1, keepdims=True) / gsz
            out = jnp.where(m, y * lax.rsqrt(ms + EPS), out)
        o_ref[pl.ds(r0, q), :] = out * nw_ref[...]
        yield


def _unit_lower_solves(lms, rhs, eye, blk, chunk):
    lds = [jnp.where(blk, lm, 0.0) for lm in lms]
    offs = [(lm - ld).astype(BF16) for lm, ld in zip(lms, lds)]
    dinv = [eye - ld for ld in lds]
    pw = [ld.astype(BF16) for ld in lds]
    yield
    span = 2
    while span < GDN_INV_BLOCK:
        sq = [_mm(p, p) for p in pw]
        yield
        dinv = [_mm(d.astype(BF16), (eye + s).astype(BF16)) for d, s in zip(dinv, sq)]
        pw = [s.astype(BF16) for s in sq]
        yield
        span *= 2
    dinv = [d.astype(BF16) for d in dinv]
    ms = [_mm(d, o).astype(BF16) for d, o in zip(dinv, offs)]
    x0 = [_mm(d, r) for d, r in zip(dinv, rhs)]
    yield
    acc = x0
    for _ in range(chunk // GDN_INV_BLOCK - 1):
        acc = [x - _mm(m, a.astype(BF16)) for x, m, a in zip(x0, ms, acc)]
        yield
    return acc


def _interleave(gens):
    results = [None] * len(gens)
    live = list(range(len(gens)))
    while live:
        for i in list(live):
            try:
                next(gens[i])
            except StopIteration as done:
                results[i] = done.value
                live.remove(i)
        yield
    return results


def _delayed(gen, rounds):
    for _ in range(rounds):
        yield
    return (yield from gen)


def _run(gen):
    try:
        while True:
            next(gen)
    except StopIteration as done:
        return done.value


def _gdn_stages(heads, qkv_ref, z_ref, sm_ref, cw_ref, dtb_ref, a_ref, nw_ref, exb_ref, exg_ref, ones_ref,
                o_ref, pad_ref, st_ref):
    t = qkv_ref.shape[0]
    c = min(GDN_CHUNK, t)
    hd = GDN_HEADDIM
    width = heads * hd
    hs = range(heads)
    sl = [slice(hh * hd, (hh + 1) * hd) for hh in hs]

    def seg_sum(v):
        hi = v.astype(BF16)
        lo = (v - hi.astype(F32)).astype(BF16)
        return _mm(hi, ones_ref[...]) + _mm(lo, ones_ref[...])

    xc = _silu(_causal_conv_block(qkv_ref, cw_ref, pad_ref))
    yield
    qf = xc[:, 0:width]
    kf = xc[:, width:2 * width]
    vf = xc[:, 2 * width:3 * width]
    qf = qf * lax.rsqrt(seg_sum(qf * qf) + EPS) * (hd ** -0.5)
    kf = kf * lax.rsqrt(seg_sum(kf * kf) + EPS)
    sm = sm_ref[...]
    beta = _dot_sel(_sigmoid(sm), exb_ref[...])
    g = a_ref[...] * _softplus(sm + dtb_ref[...])
    yield

    gs = min(GDN_GROUP, t)
    row = _iota2((gs, gs), 0)
    col = _iota2((gs, gs), 1)
    same = (row // c) == (col // c)
    causal = same & (row >= col)
    strict = same & (row > col)
    eye = jnp.where(row == col, 1.0, 0.0)
    blk = (row // GDN_INV_BLOCK) == (col // GDN_INV_BLOCK)
    tril_sel = jnp.where(causal, 1.0, 0.0).astype(BF16)
    same_sel = jnp.where(same, 1.0, 0.0).astype(BF16)
    nt = (((1,), (1,)), ((), ()))

    def gates(gi):
        rg = slice(gi * gs, (gi + 1) * gs)
        gcs = _sel_dot(tril_sel, g[rg])
        gtot = _sel_dot(same_sel, g[rg])
        gcs_t = gcs.T
        yield
        gw = _dot_sel(gcs, exg_ref[...])
        gtw = _dot_sel(gtot, exg_ref[...])
        gexp = jnp.exp(gw)
        kb = kf[rg] * beta[rg]
        yield
        decay = []
        for hh in hs:
            lg = 2 * heads + hh
            diff = jnp.where(causal, gcs[:, lg:lg + 1] - gcs_t[lg:lg + 1, :], 0.0)
            decay.append(jnp.where(causal, jnp.exp(diff), 0.0))
            yield
        return dict(kb=kb, vb=vf[rg] * beta[rg], kcd_in=kb * gexp, qd=(qf[rg] * gexp).astype(BF16),
                    kend=(kf[rg] * jnp.exp(gtw - gw)).astype(BF16), cdec=jnp.exp(gtw), decay=decay,
                    kfb=kf[rg].astype(BF16), qfb=qf[rg].astype(BF16))

    def solve(p):
        both = jnp.concatenate([p["kb"].astype(BF16), p["qfb"]], axis=0)
        kq = [lax.dot_general(both[:, sl[hh]], p["kfb"][:, sl[hh]], nt, preferred_element_type=F32) for hh in hs]
        yield
        lms = [jnp.where(strict, kq[hh][0:gs] * p["decay"][hh], 0.0) for hh in hs]
        qk = [(kq[hh][gs:2 * gs] * p["decay"][hh]).astype(BF16) for hh in hs]
        rhs = [jnp.concatenate([p["vb"][:, sl[hh]], p["kcd_in"][:, sl[hh]]], axis=1).astype(BF16) for hh in hs]
        vk = yield from _unit_lower_solves(lms, rhs, eye, blk, c)
        return qk, vk

    def recur(state, p, qk, vk):
        outs = []
        for ci in range(gs // c):
            rs = slice(ci * c, (ci + 1) * c)
            lhs = [jnp.concatenate([vk[hh][rs, hd:2 * hd].astype(BF16), p["qd"][rs, sl[hh]]], axis=0)
                   for hh in hs]
            sb = [s.astype(BF16) for s in state]
            both = [_mm(lhs[hh], sb[hh]) for hh in hs]
            yield
            v_new = [(vk[hh][rs, 0:hd] - both[hh][0:c]).astype(BF16) for hh in hs]
            o = [both[hh][c:2 * c] + _mm(qk[hh][rs, ci * c:(ci + 1) * c], v_new[hh]) for hh in hs]
            upd = [lax.dot_general(p["kend"][rs, sl[hh]], v_new[hh], (((0,), (0,)), ((), ())),
                                   preferred_element_type=F32) for hh in hs]
            state = [state[hh] * p["cdec"][ci * c:ci * c + 1, sl[hh]] + upd[hh] for hh in hs]
            outs.append(jnp.concatenate(o, axis=1))
            yield
        return state, outs

    n_groups = t // gs
    parts, solved, outs = {}, {}, []
    state = [st_ref[hh] for hh in hs]
    for k in range(n_groups + 2):
        jobs = {}
        if k < n_groups:
            jobs["gates"] = gates(k)
        if 0 <= k - 1 < n_groups:
            jobs["solve"] = solve(parts[k - 1])
        if 0 <= k - 2 < n_groups:
            jobs["recur"] = recur(state, parts[k - 2], *solved[k - 2])
        done = dict(zip(jobs, (yield from _interleave(list(jobs.values())))))
        if "gates" in done:
            parts[k] = done["gates"]
        if "solve" in done:
            solved[k - 1] = done["solve"]
        if "recur" in done:
            state, o_g = done["recur"]
            outs += o_g
    for hh in hs:
        st_ref[hh] = state[hh]
    yield
    o = jnp.concatenate(outs, axis=0)
    o = o * lax.rsqrt(seg_sum(o * o) / hd + EPS) * nw_ref[...]
    o_ref[...] = o * _silu(z_ref[...])


_N_S5_IN, _N_SSD_IN, _N_GDN_IN = 5, 10, 10


def _mixers_kernel(ssd_heads, gdn_heads, *refs):
    n_in = _N_S5_IN + _N_SSD_IN + _N_GDN_IN
    s5_in = refs[:_N_S5_IN]
    ssd_in = refs[_N_S5_IN:_N_S5_IN + _N_SSD_IN]
    gdn_in = refs[_N_S5_IN + _N_SSD_IN:n_in]
    y_s5, y_ssd, y_gdn = refs[n_in:n_in + 3]
    s5_s, s5_carry, ssd_pad, ssd_xc, ssd_dt, ssd_st, gdn_pad, gdn_st = refs[n_in + 3:]
    t = y_s5.shape[0]
    first = pl.program_id(1) == 0

    @pl.when(first)
    def _():
        s5_carry[...] = jnp.zeros(s5_carry.shape, F32)
        ssd_st[...] = jnp.zeros(ssd_st.shape, F32)
        gdn_st[...] = jnp.zeros(gdn_st.shape, F32)

    _conv_halo_init(ssd_pad, t, first)
    _conv_halo_init(gdn_pad, t, first)
    _run(_interleave([
        _gdn_stages(gdn_heads, *gdn_in, y_gdn, gdn_pad, gdn_st),
        _delayed(_ssd_stages(ssd_heads, *ssd_in, y_ssd, ssd_pad, ssd_xc, ssd_dt, ssd_st), MIXER_DELAY_ROUNDS),
        _delayed(_s5_stages(*s5_in, y_s5, s5_s, s5_carry), MIXER_DELAY_ROUNDS),
    ]))


def _mixers(u_s5, s5_ops, z_ssd, xbc, small, ssd_p, qkv, z_gdn, gdn_p, bsz, seq):
    bbd, cbd, lam, d_s5 = s5_ops
    conv_w, conv_b, dt_bias, a_log, d_skip, norm_w = ssd_p
    g_conv_w, g_a_log, g_dt_bias, g_norm_w = gdn_p
    n, s5_w = u_s5.shape
    s2 = bbd.shape[1]
    ssd_w, ssd_cdim, ssd_h = z_ssd.shape[1], xbc.shape[1], a_log.shape[0]
    gdn_w, gdn_cdim, gdn_h = z_gdn.shape[1], qkv.shape[1], g_a_log.shape[0]
    t = min(SEQ_BLOCK, seq)
    nblk = seq // t
    tok = lambda b, c: (b * nblk + c, 0)
    const = lambda b, c: (0, 0)
    rows = lambda w: pl.BlockSpec((t, w), tok)
    whole = lambda a: pl.BlockSpec(a.shape, const)
    pad = lambda v, off: jnp.zeros((1, LANES), F32).at[0, off:off + v.shape[0]].set(v)
    lanes = jnp.arange(LANES)[:, None]
    ssd_ex = (lanes == (jnp.arange(ssd_w)[None, :] // SSD_HEADDIM)).astype(BF16)
    head_of = jnp.arange(gdn_w)[None, :] // GDN_HEADDIM
    s5_args = [bbd.astype(BF16), cbd.astype(BF16), lam, d_s5]
    ssd_args = [conv_w, conv_b.reshape(1, ssd_cdim), pad(dt_bias, 0), pad(-jnp.exp(a_log), 0),
                jnp.repeat(d_skip, SSD_HEADDIM).reshape(1, ssd_w), norm_w.reshape(1, ssd_w), ssd_ex]
    gdn_args = [g_conv_w, pad(g_dt_bias, 2 * gdn_h), pad(-jnp.exp(g_a_log), 2 * gdn_h),
                jnp.tile(g_norm_w, gdn_h).reshape(1, gdn_w), (lanes == gdn_h + head_of).astype(BF16),
                (lanes == 2 * gdn_h + head_of).astype(BF16), (head_of.T == head_of).astype(BF16)]
    in_specs = ([rows(s5_w)] + [whole(a) for a in s5_args]
                + [rows(ssd_w), rows(ssd_cdim), rows(LANES)] + [whole(a) for a in ssd_args]
                + [rows(gdn_cdim), rows(gdn_w), rows(LANES)] + [whole(a) for a in gdn_args])
    assert len(in_specs) == _N_S5_IN + _N_SSD_IN + _N_GDN_IN
    return pl.pallas_call(
        functools.partial(_mixers_kernel, ssd_h, gdn_h),
        grid=(bsz, nblk),
        in_specs=in_specs,
        out_specs=[rows(s5_w), rows(ssd_w), rows(gdn_w)],
        out_shape=[jax.ShapeDtypeStruct((n, w), F32) for w in (s5_w, ssd_w, gdn_w)],
        scratch_shapes=[pltpu.VMEM((s2 // LANES, t, LANES), F32),
                        pltpu.VMEM((s2 // LANES, SUBLANES, LANES), F32),
                        pltpu.VMEM((SUBLANES + t, ssd_cdim), F32),
                        pltpu.VMEM((t, ssd_cdim), F32),
                        pltpu.VMEM((t, LANES), F32),
                        pltpu.VMEM((ssd_h, SSD_STATE, SSD_HEADDIM), F32),
                        pltpu.VMEM((SUBLANES + t, gdn_cdim), F32),
                        pltpu.VMEM((gdn_h, GDN_HEADDIM, GDN_HEADDIM), F32)],
        compiler_params=_params("parallel", "arbitrary"),
        name="mixers",
    )(u_s5, *s5_args, z_ssd, xbc, small, *ssd_args, qkv, z_gdn, small, *gdn_args)


def _outproj_kernel(x_ref, s5_ref, ssd_ref, gdn_ref, wglu_ref, s5n_ref, wout_ref, o_ref):
    y = _gelu_tanh(s5_ref[...])
    y = y * _sigmoid(jnp.dot(y.astype(BF16), wglu_ref[...], preferred_element_type=F32))
    y = _rms(y, s5n_ref[...])
    mix = jnp.concatenate([y, ssd_ref[...], gdn_ref[...]], axis=1)
    o_ref[...] = x_ref[...] + jnp.dot(mix.astype(BF16), wout_ref[...], preferred_element_type=F32)


def _outproj(x2, y_s5, y_ssd, y_gdn, w_glu, s5_norm, w_out):
    n, d = x2.shape
    tm = min(OUT_ROW_BLOCK, n)
    ws = [y_s5.shape[1], y_ssd.shape[1], y_gdn.shape[1]]
    row = lambda i: (i, 0)
    const = lambda i: (0, 0)
    return pl.pallas_call(
        _outproj_kernel,
        grid=(n // tm,),
        in_specs=[pl.BlockSpec((tm, d), row)] + [pl.BlockSpec((tm, w), row) for w in ws]
                 + [pl.BlockSpec((ws[0], ws[0]), const), pl.BlockSpec((1, ws[0]), const),
                    pl.BlockSpec((sum(ws), d), const)],
        out_specs=pl.BlockSpec((tm, d), row),
        out_shape=jax.ShapeDtypeStruct((n, d), F32),
        compiler_params=_params("parallel"),
        name="outproj",
    )(x2, y_s5, y_ssd, y_gdn, w_glu.astype(BF16), s5_norm.reshape(1, -1), w_out.astype(BF16))


def _ffn_kernel(final, x_ref, g_ref, wg_ref, wu_ref, wd_ref, gf_ref, o_ref, h_ref, acc_ref):
    f = pl.program_id(1)

    @pl.when(f == 0)
    def _():
        h_ref[...] = _rms(x_ref[...], g_ref[...]).astype(BF16)
        acc_ref[...] = jnp.zeros(acc_ref.shape, F32)

    h = h_ref[...]
    a = jnp.dot(h, wg_ref[...].astype(BF16), preferred_element_type=F32)
    u = jnp.dot(h, wu_ref[...].astype(BF16), preferred_element_type=F32)
    acc_ref[...] += jnp.dot((_silu(a) * u).astype(BF16), wd_ref[...].astype(BF16), preferred_element_type=F32)

    @pl.when(f == pl.num_programs(1) - 1)
    def _():
        out = x_ref[...] + acc_ref[...]
        o_ref[...] = _rms(out, gf_ref[...]) if final else out


def _ffn(x2, gain, w_gate, w_up, w_down, final_gain):
    n, d = x2.shape
    ff = w_gate.shape[1]
    tm = min(FFN_ROW_BLOCK, n)
    tf = min(FFN_COL_BLOCK, ff)
    final = final_gain is not None
    gf = (final_gain if final else gain).reshape(1, d)
    return pl.pallas_call(
        functools.partial(_ffn_kernel, final),
        grid=(n // tm, ff // tf),
        in_specs=[pl.BlockSpec((tm, d), lambda i, f: (i, 0)),
                  pl.BlockSpec((1, d), lambda i, f: (0, 0)),
                  pl.BlockSpec((d, tf), lambda i, f: (0, f)),
                  pl.BlockSpec((d, tf), lambda i, f: (0, f)),
                  pl.BlockSpec((tf, d), lambda i, f: (f, 0)),
                  pl.BlockSpec((1, d), lambda i, f: (0, 0))],
        out_specs=pl.BlockSpec((tm, d), lambda i, f: (i, 0)),
        out_shape=jax.ShapeDtypeStruct((n, d), F32),
        scratch_shapes=[pltpu.VMEM((tm, d), BF16), pltpu.VMEM((tm, d), F32)],
        compiler_params=_params("parallel", "arbitrary"),
        name="ffn",
    )(x2, gain.reshape(1, d), w_gate, w_up, w_down, gf)


_R_E1, _R_E2, _R_W1, _R_W2, _R_RANK1, _R_RANK2 = range(6)


def _router_kernel(n_exp, x_ref, g_ref, wr_ref, tri_ref, hn_ref, rt_ref, cnt_ref, base_ref):
    @pl.when(pl.program_id(0) == 0)
    def _():
        base_ref[...] = jnp.zeros(base_ref.shape, F32)

    hn = _rms(x_ref[...], g_ref[...])
    hn_ref[...] = hn
    h_hi = hn.astype(BF16)
    h_lo = (hn - h_hi.astype(F32)).astype(BF16)
    w_hi = wr_ref[...].astype(BF16)
    w_lo = (wr_ref[...] - w_hi.astype(F32)).astype(BF16)
    logits = _mm(h_hi, w_hi) + _mm(h_lo, w_hi) + _mm(h_hi, w_lo)
    lane = _iota2(logits.shape, 1)
    neg = jnp.float32(-jnp.inf)
    logits = jnp.where(lane < n_exp, logits, neg)
    m1 = jnp.max(logits, axis=-1, keepdims=True)
    i1 = jnp.min(jnp.where(logits == m1, lane, LANES), axis=-1, keepdims=True)
    rest = jnp.where(lane == i1, neg, logits)
    m2 = jnp.max(rest, axis=-1, keepdims=True)
    i2 = jnp.min(jnp.where(rest == m2, lane, LANES), axis=-1, keepdims=True)
    w1 = 1.0 / (1.0 + jnp.exp(m2 - m1))
    w2 = 1.0 / (1.0 + jnp.exp(m1 - m2))
    hit = (lane == i1) | (lane == i2)
    onehot = jnp.where(hit, 1.0, 0.0)
    before = jnp.dot(tri_ref[...], onehot.astype(BF16), preferred_element_type=F32) + base_ref[...]
    r1 = jnp.sum(jnp.where(lane == i1, before, 0.0), axis=-1, keepdims=True)
    r2 = jnp.sum(jnp.where(lane == i2, before, 0.0), axis=-1, keepdims=True)
    cols = {_R_E1: i1.astype(F32), _R_E2: i2.astype(F32), _R_W1: w1, _R_W2: w2, _R_RANK1: r1, _R_RANK2: r2}
    rt = jnp.zeros(logits.shape, F32)
    for k, v in cols.items():
        rt = jnp.where(lane == k, v, rt)
    rt_ref[...] = rt
    base_ref[...] += jnp.sum(onehot, axis=0, keepdims=True)
    cnt_ref[...] = base_ref[...]


def _row_copy(src_ref, src_row, dst_ref, dst_row, sem):
    return pltpu.make_async_copy(src_ref.at[pl.ds(src_row, 1)], dst_ref.at[pl.ds(dst_row, 1)], sem)


def _dispatch_kernel(n_exp, rb, s1_ref, s2_ref, fill_ref, hn_ref, xs_ref, zero_ref, sem, zsem):
    tm = hn_ref.shape[0]
    base = pl.program_id(0) * tm

    @pl.when(pl.program_id(0) == 0)
    def _():
        zero_ref[...] = jnp.zeros(zero_ref.shape, F32)
        for e in range(n_exp):
            lo, hi = fill_ref[e], fill_ref[n_exp + e]

            def start_row(r, carry):
                _row_copy(zero_ref, 0, xs_ref, r, zsem).start()
                return carry

            def wait_row(r, carry):
                _row_copy(zero_ref, 0, xs_ref, r, zsem).wait()
                return carry

            lax.fori_loop(lo, hi, start_row, 0)
            lax.fori_loop(lo, hi, wait_row, 0)

        def block_copy(b):
            return pltpu.make_async_copy(zero_ref, xs_ref.at[pl.ds(pl.multiple_of(b * rb, rb), rb)], zsem)

        def start_block(b, carry):
            block_copy(b).start()
            return carry

        def wait_block(b, carry):
            block_copy(b).wait()
            return carry

        n_blocks = xs_ref.shape[0] // rb
        lax.fori_loop(fill_ref[2 * n_exp], n_blocks, start_block, 0)
        lax.fori_loop(fill_ref[2 * n_exp], n_blocks, wait_block, 0)

    def issue(r, carry):
        _row_copy(hn_ref, r, xs_ref, s1_ref[base + r], sem).start(priority=0)
        _row_copy(hn_ref, r, xs_ref, s2_ref[base + r], sem).start(priority=1)
        return carry

    lax.fori_loop(0, tm, issue, 0, unroll=DMA_ISSUE_UNROLL)
    for _ in range(TOP_K):
        pltpu.make_async_copy(hn_ref, xs_ref.at[pl.ds(0, tm)], sem).wait()


def _expert_ffn_kernel(be_ref, bv_ref, xs_ref, wg_ref, wu_ref, wd_ref, ys_ref, h_ref, acc_ref):
    i = pl.program_id(0)
    f = pl.program_id(1)
    last = pl.num_programs(1) - 1
    valid = bv_ref[i]

    @pl.when(valid > 0)
    def _():
        @pl.when(f == 0)
        def _():
            row = _iota2((xs_ref.shape[0], 1), 0)
            h_ref[...] = jnp.where(row < valid, xs_ref[...], 0.0).astype(BF16)
            acc_ref[...] = jnp.zeros(acc_ref.shape, F32)

        h = h_ref[...]
        a = jnp.dot(h, wg_ref[0].astype(BF16), preferred_element_type=F32)
        u = jnp.dot(h, wu_ref[0].astype(BF16), preferred_element_type=F32)
        acc_ref[...] += jnp.dot((_silu(a) * u).astype(BF16), wd_ref[0].astype(BF16), preferred_element_type=F32)

        @pl.when(f == last)
        def _():
            ys_ref[...] = acc_ref[...]

    @pl.when((valid == 0) & (f == last))
    def _():
        ys_ref[...] = jnp.zeros(ys_ref.shape, F32)


def _combine_kernel(final, s1_ref, s2_ref, x_ref, rt_ref, gf_ref, ys_ref, o_ref, b1_ref, b2_ref, sem):
    tm = x_ref.shape[0]
    i = pl.program_id(0)
    cur = i % 2

    def issue(block, buf):
        def one(r, carry):
            _row_copy(ys_ref, s1_ref[block * tm + r], b1_ref.at[buf], r, sem.at[buf]).start(priority=0)
            _row_copy(ys_ref, s2_ref[block * tm + r], b2_ref.at[buf], r, sem.at[buf]).start(priority=1)
            return carry

        lax.fori_loop(0, tm, one, 0, unroll=DMA_ISSUE_UNROLL)

    @pl.when(i == 0)
    def _():
        issue(0, 0)

    @pl.when(i + 1 < pl.num_programs(0))
    def _():
        issue(i + 1, 1 - cur)

    for b_ref in (b1_ref, b2_ref):
        pltpu.make_async_copy(ys_ref.at[pl.ds(0, tm)], b_ref.at[cur], sem.at[cur]).wait()
    rt = rt_ref[...]
    out = (x_ref[...] + rt[:, _R_W1:_R_W1 + 1] * b1_ref[cur] + rt[:, _R_W2:_R_W2 + 1] * b2_ref[cur])
    o_ref[...] = _rms(out, gf_ref[...]) if final else out


def _moe(x2, gain, w_router, w_gate, w_up, w_down, final_gain):
    n, d = x2.shape
    n_exp, _, ff = w_gate.shape
    final = final_gain is not None
    gf = (final_gain if final else gain).reshape(1, d)
    wr = jnp.zeros((d, LANES), F32).at[:, :n_exp].set(w_router)

    tr = min(ROUTER_BLOCK, n)
    tri = (jnp.arange(tr)[:, None] > jnp.arange(tr)[None, :]).astype(BF16)
    hn, route, counts = pl.pallas_call(
        functools.partial(_router_kernel, n_exp),
        grid=(n // tr,),
        in_specs=[pl.BlockSpec((tr, d), lambda i: (i, 0)),
                  pl.BlockSpec((1, d), lambda i: (0, 0)),
                  pl.BlockSpec((d, LANES), lambda i: (0, 0)),
                  pl.BlockSpec((tr, tr), lambda i: (0, 0))],
        out_specs=[pl.BlockSpec((tr, d), lambda i: (i, 0)),
                   pl.BlockSpec((tr, LANES), lambda i: (i, 0)),
                   pl.BlockSpec((1, LANES), lambda i: (0, 0))],
        out_shape=[jax.ShapeDtypeStruct((n, d), F32),
                   jax.ShapeDtypeStruct((n, LANES), F32),
                   jax.ShapeDtypeStruct((1, LANES), F32)],
        scratch_shapes=[pltpu.VMEM((1, LANES), F32)],
        compiler_params=_params("arbitrary"),
        name="moe_router",
    )(x2, gain.reshape(1, d), wr, tri)

    rb = min(MOE_ROW_BLOCK, TOP_K * n)
    n_slots = TOP_K * n + n_exp * rb
    n_blocks = n_slots // rb
    cnt = counts[0, :n_exp].astype(jnp.int32)
    padded = ((cnt + rb - 1) // rb) * rb
    ends = jnp.cumsum(padded)
    starts = ends - padded
    as_int = lambda k: route[:, k].astype(jnp.int32)
    slot1 = starts[as_int(_R_E1)] + as_int(_R_RANK1)
    slot2 = starts[as_int(_R_E2)] + as_int(_R_RANK2)
    block_row = jnp.arange(n_blocks, dtype=jnp.int32) * rb
    block_exp = jnp.minimum(jnp.sum(block_row[:, None] >= ends[None, :], axis=1), n_exp - 1).astype(jnp.int32)
    block_valid = jnp.clip(cnt[block_exp] - (block_row - starts[block_exp]), 0, rb).astype(jnp.int32)

    tdp = min(DISPATCH_BLOCK, n)
    fill = jnp.concatenate([starts + cnt, ends, ends[-1:] // rb]).astype(jnp.int32)
    xs = pl.pallas_call(
        functools.partial(_dispatch_kernel, n_exp, rb),
        grid_spec=pltpu.PrefetchScalarGridSpec(
            num_scalar_prefetch=3,
            grid=(n // tdp,),
            in_specs=[pl.BlockSpec((tdp, d), lambda i, s1, s2, fl: (i, 0))],
            out_specs=pl.BlockSpec(memory_space=pl.ANY),
            scratch_shapes=[pltpu.VMEM((rb, d), F32), pltpu.SemaphoreType.DMA, pltpu.SemaphoreType.DMA]),
        out_shape=jax.ShapeDtypeStruct((n_slots, d), F32),
        compiler_params=_params("arbitrary"),
        name="moe_dispatch",
    )(slot1, slot2, fill, hn)

    tf = min(FFN_COL_BLOCK, ff)
    ys = pl.pallas_call(
        _expert_ffn_kernel,
        grid_spec=pltpu.PrefetchScalarGridSpec(
            num_scalar_prefetch=2,
            grid=(n_blocks, ff // tf),
            in_specs=[pl.BlockSpec((rb, d), lambda i, f, be, bv: (i, 0)),
                      pl.BlockSpec((1, d, tf), lambda i, f, be, bv: (be[i], 0, f)),
                      pl.BlockSpec((1, d, tf), lambda i, f, be, bv: (be[i], 0, f)),
                      pl.BlockSpec((1, tf, d), lambda i, f, be, bv: (be[i], f, 0))],
            out_specs=pl.BlockSpec((rb, d), lambda i, f, be, bv: (i, 0)),
            scratch_shapes=[pltpu.VMEM((rb, d), BF16), pltpu.VMEM((rb, d), F32)]),
        out_shape=jax.ShapeDtypeStruct((n_slots, d), F32),
        compiler_params=_params("parallel", "arbitrary"),
        name="moe_experts",
    )(block_exp, block_valid, xs, w_gate, w_up, w_down)

    tc = min(COMBINE_BLOCK, n)
    return pl.pallas_call(
        functools.partial(_combine_kernel, final),
        grid_spec=pltpu.PrefetchScalarGridSpec(
            num_scalar_prefetch=2,
            grid=(n // tc,),
            in_specs=[pl.BlockSpec((tc, d), lambda i, s1, s2: (i, 0)),
                      pl.BlockSpec((tc, LANES), lambda i, s1, s2: (i, 0)),
                      pl.BlockSpec((1, d), lambda i, s1, s2: (0, 0)),
                      pl.BlockSpec(memory_space=pl.ANY)],
            out_specs=pl.BlockSpec((tc, d), lambda i, s1, s2: (i, 0)),
            scratch_shapes=[pltpu.VMEM((2, tc, d), F32), pltpu.VMEM((2, tc, d), F32),
                            pltpu.SemaphoreType.DMA((2,))]),
        out_shape=jax.ShapeDtypeStruct((n, d), F32),
        compiler_params=_params("arbitrary"),
        name="moe_combine",
    )(slot1, slot2, x2, route, gf, ys)


def _arrange_in_proj(w_in, s5_w, ssd_w, ssd_cdim, ssd_h, gdn_cdim, gdn_w, gdn_h):
    sizes = (s5_w, ssd_w, ssd_cdim, ssd_h, gdn_cdim, gdn_w, gdn_h, gdn_h)
    offs = [int(o) for o in np.cumsum((0,) + sizes)]
    seg = lambda i: w_in[:, offs[i]:offs[i + 1]]
    small = jnp.concatenate([seg(3), seg(6), seg(7)], axis=1)
    small = jnp.pad(small, ((0, 0), (0, LANES - small.shape[1])))
    cuts = ((offs[0], offs[3]), (offs[4], offs[5]), (offs[5], offs[6]))
    return small, (s5_w, ssd_w, ssd_cdim, gdn_cdim, gdn_w, LANES), cuts


def kernel(x, norm_mix, w_in, w_out, s5_a_re, s5_a_im, s5_b_re, s5_b_im, s5_c_re, s5_c_im, s5_d, s5_log_step, s5_w_glu, s5_norm, ssd_conv_w, ssd_conv_b, ssd_dt_bias, ssd_a_log, ssd_d, ssd_norm, gdn_conv_w, gdn_a_log, gdn_dt_bias, gdn_norm, norm_ffn, ff_w_gate, ff_w_up, ff_w_down, moe_router, moe_w_gate, moe_w_up, moe_w_down, norm_final):
    bsz, seq, d = x.shape
    depth = norm_mix.shape[0]
    s5_w = s5_w_glu.shape[-1]
    ssd_w, ssd_h, ssd_cdim = ssd_norm.shape[-1], ssd_a_log.shape[-1], ssd_conv_w.shape[-1]
    gdn_h, gdn_cdim = gdn_a_log.shape[-1], gdn_conv_w.shape[-1]
    gdn_w = gdn_h * GDN_HEADDIM
    xr = x.reshape(bsz * seq, d)
    for layer in range(depth):
        w_small, sizes, cuts = _arrange_in_proj(w_in[layer], s5_w, ssd_w, ssd_cdim, ssd_h, gdn_cdim, gdn_w, gdn_h)
        u_s5, z_ssd, xbc, qkv, z_gdn, small = _inproj(xr, norm_mix[layer], w_in, layer, w_small, sizes, cuts)
        s5_ops = _s5_operands(s5_a_re[layer], s5_a_im[layer], s5_b_re[layer], s5_b_im[layer], s5_c_re[layer],
                              s5_c_im[layer], s5_d[layer], s5_log_step[layer])
        ssd_p = (ssd_conv_w[layer], ssd_conv_b[layer], ssd_dt_bias[layer], ssd_a_log[layer], ssd_d[layer],
                 ssd_norm[layer])
        gdn_p = (gdn_conv_w[layer], gdn_a_log[layer], gdn_dt_bias[layer], gdn_norm[layer])
        y_s5, y_ssd, y_gdn = _mixers(u_s5, s5_ops, z_ssd, xbc, small, ssd_p, qkv, z_gdn, gdn_p, bsz, seq)
        xr = _outproj(xr, y_s5, y_ssd, y_gdn, s5_w_glu[layer], s5_norm[layer], w_out[layer])
        final_gain = norm_final if layer == depth - 1 else None
        i = layer // 2
        if layer % 2 == 0:
            xr = _ffn(xr, norm_ffn[layer], ff_w_gate[i], ff_w_up[i], ff_w_down[i], final_gain)
        else:
            xr = _moe(xr, norm_ffn[layer], moe_router[i], moe_w_gate[i], moe_w_up[i], moe_w_down[i], final_gain)
    return xr.reshape(bsz, seq, d)
```
